```python
import jax
import jax.numpy as jnp
from jax import lax
import numpy as np

D_MODEL = 2048
BATCH = 8
SEQ = 4096
DEPTH = 4

CTX_LEN = 256
GRID_W = 64
EPS = 1e-6
N_MOD = 6

LRU_WIDTH = 512
LRU_BLOCKS = 4
LRU_BLOCK = LRU_WIDTH // LRU_BLOCKS
CONV_W = 4
LRU_C = 8.0
A_MIN = 0.9
A_MAX = 0.999
HG_HEADS = 4
HG_DK = 128
HG_DV = 128
HG_WIDTH = HG_HEADS * HG_DK
HG_CHUNK = 64
HEAD_DIM = 128
N_Q_HEADS = 8
N_KV_HEADS = 2
Q_PER_KV = N_Q_HEADS // N_KV_HEADS
ATT_WIDTH = N_Q_HEADS * HEAD_DIM
KV_WIDTH = N_KV_HEADS * HEAD_DIM
Q_BLOCK = 128
ROPE_THETA = 10000.0
ROPE_FREQS = HEAD_DIM // 4
ATTN_SCALE = HEAD_DIM ** -0.5
A_IN = 2 * LRU_WIDTH
B_IN = 5 * HG_WIDTH
C_IN = ATT_WIDTH + 2 * KV_WIDTH
IN_WIDTH = A_IN + B_IN + C_IN
MIX_WIDTH = LRU_WIDTH + HG_HEADS * HG_DV + ATT_WIDTH
D_FF = 5632
N_EXPERTS = 8
TOP_K = 2
N_DENSE = (DEPTH + 1) // 2
N_MOE = DEPTH // 2

kernel_name = 'hybrid_rglru_hgrn2_gqa_moe_dit'


def rmsnorm(x, g):
    xf = x.astype(jnp.float32)
    y = xf * lax.rsqrt(jnp.mean(xf * xf, axis=-1, keepdims=True) + EPS)
    return (y * g.astype(jnp.float32)).astype(x.dtype)


def modulate(h, mod, i):
    return h * (1 + mod[:, :, i + 1]) + mod[:, :, i]


def _flip(t):
    return jnp.flip(t, axis=1)


def _ident(t):
    return t


def rope_tables(rows):
    row = jnp.repeat(jnp.arange(rows, dtype=jnp.float32), GRID_W)
    col = jnp.tile(jnp.arange(GRID_W, dtype=jnp.float32), rows)
    inv = ROPE_THETA ** (-jnp.arange(ROPE_FREQS, dtype=jnp.float32) / ROPE_FREQS)
    ang = jnp.stack([row[:, None] * inv, col[:, None] * inv], axis=1)
    return jnp.cos(ang), jnp.sin(ang)


def apply_rope_2d(x, cos, sin):
    xr = x.astype(jnp.float32).reshape(*x.shape[:-1], 2, 2, ROPE_FREQS)
    x1, x2 = xr[..., 0, :], xr[..., 1, :]
    cs, sn = cos[None, :, None], sin[None, :, None]
    out = jnp.stack([x1 * cs - x2 * sn, x2 * cs + x1 * sn], axis=-2)
    return out.reshape(x.shape).astype(x.dtype)


def dwconv_centred(x, w, b):
    y = lax.conv_general_dilated(
        x, w[:, None, :], window_strides=(1,),
        padding=[(CONV_W // 2, CONV_W - 1 - CONV_W // 2)],
        dimension_numbers=('NWC', 'WIO', 'NWC'), feature_group_count=x.shape[-1])
    return y + b


def block_diag(u, w, b):
    ub = u.reshape(*u.shape[:-1], LRU_BLOCKS, LRU_BLOCK)
    return jnp.einsum('btnk,nkj->btnj', ub, w).reshape(u.shape) + b


def _lin_combine(l, r):
    al, bl = l
    ar, br = r
    return al * ar, ar * bl + br


def rglru_scan(u, wa, ba, wi, bi, lam, h0):
    r = jax.nn.sigmoid(block_diag(u, wa, ba))
    i = jax.nn.sigmoid(block_diag(u, wi, bi))
    log_a = -LRU_C * r * jax.nn.softplus(-lam)
    a = jnp.exp(log_a)
    bt = jnp.sqrt(-jnp.expm1(2.0 * log_a)) * (i * u)
    bt = bt.at[:, 0].add(a[:, 0] * h0)
    _, h = lax.associative_scan(_lin_combine, (a, bt), axis=1)
    return h


def mixer_rglru(ua_ctx, ua_lat, conv_w, conv_b, wa, ba, wi, bi, lam, need_ctx):
    y_c, x_c = jnp.split(ua_ctx, 2, axis=-1)
    y_l, x_l = jnp.split(ua_lat, 2, axis=-1)
    x_c = dwconv_centred(x_c, conv_w, conv_b).astype(jnp.float32)
    x_l = dwconv_centred(x_l, conv_w, conv_b).astype(jnp.float32)
    h_c, h_l = [], []
    for d in range(2):
        fl = _flip if d else _ident
        hc = rglru_scan(fl(x_c), wa[d], ba[d], wi[d], bi[d], lam[d], jnp.zeros_like(x_c[:, 0]))
        hl = rglru_scan(fl(x_l), wa[d], ba[d], wi[d], bi[d], lam[d], hc[:, -1])
        h_c.append(fl(hc))
        h_l.append(fl(hl))
    out_l = (jax.nn.gelu(y_l.astype(jnp.float32)) * (h_l[0] + h_l[1])).astype(ua_lat.dtype)
    out_c = None
    if need_ctx:
        out_c = (jax.nn.gelu(y_c.astype(jnp.float32)) * (h_c[0] + h_c[1])).astype(ua_ctx.dtype)
    return out_c, out_l


def hgrn_chunk_scan(q, k, v, g, s0):
    bsz, t = q.shape[:2]
    nch = t // HG_CHUNK

    def to_chunks(z):
        return z.reshape(bsz, nch, HG_CHUNK, HG_HEADS, z.shape[-1]).transpose(1, 0, 3, 2, 4)

    mask = jnp.tril(jnp.ones((HG_CHUNK, HG_CHUNK), dtype=bool))

    def step(s, inp):
        qc, kc, vc, gc = inp
        bc = jnp.cumsum(gc, axis=2)
        o_inter = jnp.einsum('bhtk,bhkv->bhtv', qc * jnp.exp(bc), s)
        diff = bc[:, :, :, None, :] - bc[:, :, None, :, :]
        decay = jnp.exp(jnp.where(mask[:, :, None], diff, -jnp.inf))
        att = jnp.einsum('bhtk,bhsk,bhtsk->bhts', qc, kc, decay)
        o = o_inter + jnp.einsum('bhts,bhsv->bhtv', att, vc)
        blast = bc[:, :, -1:, :]
        s = jnp.exp(blast[:, :, 0, :])[..., None] * s + jnp.einsum(
            'bhsk,bhsv->bhkv', kc * jnp.exp(blast - bc), vc)
        return s, o

    s, o = lax.scan(step, s0, (to_chunks(q), to_chunks(k), to_chunks(v), to_chunks(g)))
    o = o.transpose(1, 0, 3, 2, 4).reshape(bsz, t, HG_HEADS, HG_DV)
    return o, s


def mixer_hgrn(ub_ctx, ub_lat, lb_f, lb_b, gnorm, need_ctx):
    def prep(u):
        uf = u.astype(jnp.float32)
        q, ff, fb, i, g = jnp.split(uf, 5, axis=-1)
        hd = lambda z: z.reshape(z.shape[0], z.shape[1], HG_HEADS, HG_DK)
        return hd(jax.nn.silu(q)), hd(ff), hd(fb), hd(i), g

    def forget(f_logit, lb):
        f = lb + (1 - lb) * jax.nn.sigmoid(f_logit)
        return 1 - f, jnp.log(f)

    q_c, ff_c, fb_c, i_c, og_c = prep(ub_ctx)
    q_l, ff_l, fb_l, i_l, og_l = prep(ub_lat)
    bsz = ub_lat.shape[0]
    o_c, o_l = [], []
    for d, (lb, f_c, f_l) in enumerate(((lb_f, ff_c, ff_l), (lb_b, fb_c, fb_l))):
        fl = _flip if d else _ident
        lbh = lb.astype(jnp.float32).reshape(HG_HEADS, HG_DK)
        k_c, lg_c = forget(f_c, lbh)
        k_l, lg_l = forget(f_l, lbh)
        s0 = jnp.zeros((bsz, HG_HEADS, HG_DK, HG_DV), jnp.float32)
        oc, sc = hgrn_chunk_scan(fl(q_c), fl(k_c), fl(i_c), fl(lg_c), s0)
        ol, _ = hgrn_chunk_scan(fl(q_l), fl(k_l), fl(i_l), fl(lg_l), sc)
        o_c.append(fl(oc))
        o_l.append(fl(ol))

    def readout(o, og, dt):
        y = rmsnorm(o, gnorm).reshape(og.shape) * jax.nn.silu(og)
        return y.astype(dt)

    out_l = readout(o_l[0] + o_l[1], og_l, ub_lat.dtype)
    out_c = readout(o_c[0] + o_c[1], og_c, ub_ctx.dtype) if need_ctx else None
    return out_c, out_l


def gqa_attend(q, k, v):
    bsz, tq = q.shape[:2]
    qg = q.reshape(bsz, tq, N_KV_HEADS, Q_PER_KV, HEAD_DIM)
    s = jnp.einsum('bqngd,bknd->bngqk', qg, k).astype(jnp.float32) * ATTN_SCALE
    p = jax.nn.softmax(s, axis=-1).astype(v.dtype)
    o = jnp.einsum('bngqk,bknd->bqngd', p, v)
    return o.reshape(bsz, tq, ATT_WIDTH)


def mixer_attn(uc_ctx, uc_lat, qn, kn, cos, sin, need_ctx):
    def qkv(u):
        bsz, t = u.shape[:2]
        q, k, v = jnp.split(u, [ATT_WIDTH, ATT_WIDTH + KV_WIDTH], axis=-1)
        q = rmsnorm(q.reshape(bsz, t, N_Q_HEADS, HEAD_DIM), qn)
        k = rmsnorm(k.reshape(bsz, t, N_KV_HEADS, HEAD_DIM), kn)
        return q, k, v.reshape(bsz, t, N_KV_HEADS, HEAD_DIM)

    q_c, k_c, v_c = qkv(uc_ctx)
    q_l, k_l, v_l = qkv(uc_lat)
    q_l = apply_rope_2d(q_l, cos, sin)
    k_l = apply_rope_2d(k_l, cos, sin)
    k_all = jnp.concatenate([k_c, k_l], axis=1)
    v_all = jnp.concatenate([v_c, v_l], axis=1)
    bsz, n = q_l.shape[:2]
    nblk = n // Q_BLOCK
    qb = q_l.reshape(bsz, nblk, Q_BLOCK, N_Q_HEADS, HEAD_DIM).swapaxes(0, 1)
    o_l = lax.map(lambda blk: gqa_attend(blk, k_all, v_all), qb)
    o_l = o_l.swapaxes(0, 1).reshape(bsz, n, ATT_WIDTH)
    o_c = gqa_attend(q_c, k_c, v_c) if need_ctx else None
    return o_c, o_l


def swiglu(h, w1, w3, w2):
    return (jax.nn.silu(h @ w1) * (h @ w3)) @ w2


def moe_swiglu(h, router, w1, w3, w2):
    shp = h.shape
    t = h.reshape(-1, shp[-1])
    logits = (t @ router).astype(jnp.float32)
    top_v, top_i = lax.top_k(logits, TOP_K)
    w = jax.nn.softmax(top_v, axis=-1)
    gates = jnp.einsum('nk,nke->ne', w, jax.nn.one_hot(top_i, N_EXPERTS, dtype=jnp.float32)).astype(h.dtype)
    out = jnp.zeros_like(t)
    for e in range(N_EXPERTS):
        out = out + gates[:, e:e + 1] * swiglu(t, w1[e], w3[e], w2[e])
    return out.reshape(shp)


def setup_inputs(seed: int = 0) -> dict:
    key = jax.random.key(seed)
    ks = jax.random.split(key, 32)
    f32 = jnp.float32
    L = DEPTH

    def nrm(k, shape, scale):
        return jax.random.normal(k, shape, f32) * scale

    a_init = jax.random.uniform(ks[16], (L, 2, LRU_WIDTH), f32, A_MIN, A_MAX)
    a_root = a_init ** (1.0 / LRU_C)
    lru_lambda = jnp.log(a_root) - jnp.log1p(-a_root)
    return {
        'x': nrm(ks[0], (BATCH, SEQ, D_MODEL), 1.0),
        'c': nrm(ks[1], (BATCH, D_MODEL), 1.0),
        'ctx': nrm(ks[2], (BATCH, CTX_LEN, D_MODEL), 1.0),
        'c_ctx': nrm(ks[3], (D_MODEL,), 1.0),
        'w_mod': nrm(ks[4], (L, D_MODEL, N_MOD * D_MODEL), 0.5 * D_MODEL ** -0.5),
        'b_mod': nrm(ks[5], (L, N_MOD * D_MODEL), 0.02),
        'norm1': 1.0 + nrm(ks[6], (L, D_MODEL), 0.02),
        'norm2': 1.0 + nrm(ks[7], (L, D_MODEL), 0.02),
        'w_in': nrm(ks[8], (L, D_MODEL, IN_WIDTH), D_MODEL ** -0.5),
        'w_out': nrm(ks[9], (L, MIX_WIDTH, D_MODEL), MIX_WIDTH ** -0.5),
        'conv_w': nrm(ks[10], (L, CONV_W, LRU_WIDTH), CONV_W ** -0.5),
        'conv_b': nrm(ks[11], (L, LRU_WIDTH), 0.02),
        'lru_wa': nrm(ks[12], (L, 2, LRU_BLOCKS, LRU_BLOCK, LRU_BLOCK), LRU_BLOCK ** -0.5),
        'lru_ba': nrm(ks[13], (L, 2, LRU_WIDTH), 0.02),
        'lru_wi': nrm(ks[14], (L, 2, LRU_BLOCKS, LRU_BLOCK, LRU_BLOCK), LRU_BLOCK ** -0.5),
        'lru_bi': nrm(ks[15], (L, 2, LRU_WIDTH), 0.02),
        'lru_lambda': lru_lambda,
        'hgrn_lb_logits': nrm(ks[17], (2, L, HG_WIDTH), 0.1),
        'hgrn_gnorm': 1.0 + nrm(ks[18], (L, HG_DV), 0.02),
        'q_norm': 1.0 + nrm(ks[19], (L, HEAD_DIM), 0.02),
        'k_norm': 1.0 + nrm(ks[20], (L, HEAD_DIM), 0.02),
        'ffn_w1': nrm(ks[21], (N_DENSE, D_MODEL, D_FF), D_MODEL ** -0.5),
        'ffn_w3': nrm(ks[22], (N_DENSE, D_MODEL, D_FF), D_MODEL ** -0.5),
        'ffn_w2': nrm(ks[23], (N_DENSE, D_FF, D_MODEL), D_FF ** -0.5),
        'router': nrm(ks[24], (N_MOE, D_MODEL, N_EXPERTS), D_MODEL ** -0.5),
        'moe_w1': nrm(ks[25], (N_MOE, N_EXPERTS, D_MODEL, D_FF), D_MODEL ** -0.5),
        'moe_w3': nrm(ks[26], (N_MOE, N_EXPERTS, D_MODEL, D_FF), D_MODEL ** -0.5),
        'moe_w2': nrm(ks[27], (N_MOE, N_EXPERTS, D_FF, D_MODEL), D_FF ** -0.5),
    }


def reference(x, c, ctx, c_ctx, w_mod, b_mod, norm1, norm2, w_in, w_out,
              conv_w, conv_b, lru_wa, lru_ba, lru_wi, lru_bi, lru_lambda,
              hgrn_lb_logits, hgrn_gnorm, q_norm, k_norm,
              ffn_w1, ffn_w3, ffn_w2, router, moe_w1, moe_w3, moe_w2):
    bsz, n, d = x.shape
    n_ctx = ctx.shape[1]
    ROWS = n // GRID_W
    cos, sin = rope_tables(ROWS)
    p = jax.nn.softmax(hgrn_lb_logits.astype(jnp.float32), axis=1)
    lower = jnp.cumsum(p, axis=1) - p[:, :1]
    xl, xc = x, ctx
    for l in range(DEPTH):
        need_ctx = l < DEPTH - 1
        mod_l = (jax.nn.silu(c) @ w_mod[l] + b_mod[l]).reshape(bsz, 1, N_MOD, d)
        mod_c = (jax.nn.silu(c_ctx) @ w_mod[l] + b_mod[l]).reshape(1, 1, N_MOD, d)
        u_l = modulate(rmsnorm(xl, norm1[l]), mod_l, 0) @ w_in[l]
        u_c = modulate(rmsnorm(xc, norm1[l]), mod_c, 0) @ w_in[l]
        ua_l, ub_l, uc_l = jnp.split(u_l, [A_IN, A_IN + B_IN], axis=-1)
        ua_c, ub_c, uc_c = jnp.split(u_c, [A_IN, A_IN + B_IN], axis=-1)
        ra_c, ra_l = mixer_rglru(ua_c, ua_l, conv_w[l], conv_b[l], lru_wa[l], lru_ba[l],
                                 lru_wi[l], lru_bi[l], lru_lambda[l], need_ctx)
        hg_c, hg_l = mixer_hgrn(ub_c, ub_l, lower[0, l], lower[1, l], hgrn_gnorm[l], need_ctx)
        at_c, at_l = mixer_attn(uc_c, uc_l, q_norm[l], k_norm[l], cos, sin, need_ctx)
        xl = xl + mod_l[:, :, 2] * (jnp.concatenate([ra_l, hg_l, at_l], axis=-1) @ w_out[l])
        if need_ctx:
            xc = xc + mod_c[:, :, 2] * (jnp.concatenate([ra_c, hg_c, at_c], axis=-1) @ w_out[l])
            h2 = jnp.concatenate([modulate(rmsnorm(xc, norm2[l]), mod_c, 3),
                                  modulate(rmsnorm(xl, norm2[l]), mod_l, 3)], axis=1)
        else:
            h2 = modulate(rmsnorm(xl, norm2[l]), mod_l, 3)
        j = l // 2
        if l % 2 == 0:
            f = swiglu(h2, ffn_w1[j], ffn_w3[j], ffn_w2[j])
        else:
            f = moe_swiglu(h2, router[j], moe_w1[j], moe_w3[j], moe_w2[j])
        if need_ctx:
            xc = xc + mod_c[:, :, 5] * f[:, :n_ctx]
            xl = xl + mod_l[:, :, 5] * f[:, n_ctx:]
        else:
            xl = xl + mod_l[:, :, 5] * f
    return xl
```

```python
import functools

import jax
import jax.numpy as jnp
from jax import lax
from jax.experimental import pallas as pl
from jax.experimental.pallas import tpu as pltpu

F32 = jnp.float32
BF16 = jnp.bfloat16

EPS = 1e-6
N_MOD = 6
GRID_W = 64
LRU_WIDTH = 512
LRU_BLOCKS = 4
LRU_BLOCK = LRU_WIDTH // LRU_BLOCKS
LRU_C = 8.0
HG_HEADS = 4
HG_DK = 128
HG_WIDTH = HG_HEADS * HG_DK
HG_CHUNK = 64
HG_SUB = 16
HEAD_DIM = 128
N_Q_HEADS = 8
N_KV_HEADS = 2
Q_PER_KV = N_Q_HEADS // N_KV_HEADS
ATT_WIDTH = N_Q_HEADS * HEAD_DIM
KV_WIDTH = N_KV_HEADS * HEAD_DIM
ROPE_THETA = 10000.0
ROPE_FREQS = HEAD_DIM // 4
ATTN_SCALE = HEAD_DIM ** -0.5
A_IN = 2 * LRU_WIDTH
B_IN = 5 * HG_WIDTH
C_IN = ATT_WIDTH + 2 * KV_WIDTH
IN_WIDTH = A_IN + B_IN + C_IN
N_EXPERTS = 8

LANES = 128
SUBLANES = 8
SEQ_CHUNK = 256
TOKEN_TILE = 512
EXPERT_TILE = 512
VMEM_LIMIT = 56 * 1024 * 1024
NEG_BIG = -1e30


def _cparams(*sem):
    return pltpu.CompilerParams(dimension_semantics=sem, vmem_limit_bytes=VMEM_LIMIT)


def _sigmoid(v):
    return 1.0 / (1.0 + jnp.exp(-v))


def _silu(v):
    return v * _sigmoid(v)


def _dot(a, b):
    return jnp.dot(a, b, preferred_element_type=F32)


def _dot_nt(a, b):
    return lax.dot_general(a, b, (((1,), (1,)), ((), ())), preferred_element_type=F32)


def _dot_tn(a, b):
    return lax.dot_general(a, b, (((0,), (0,)), ((), ())), preferred_element_type=F32)


def _mod_row(i, tile, n_lat, seq, nb):
    return jnp.where(i < n_lat // tile, (i * tile) // seq, nb)


def _mod_kernel(c_ref, w_ref, b_ref, o_ref):
    s = _silu(c_ref[...]).astype(BF16)
    o_ref[0] = _dot(s, w_ref[0].astype(BF16)) + b_ref[0]


def _mod_table(cc, w_mod, b_mod):
    nl, d, md = w_mod.shape
    rows = cc.shape[0]
    tn = 1024 if md % 1024 == 0 else md
    return pl.pallas_call(
        _mod_kernel,
        grid=(nl, md // tn),
        in_specs=[
            pl.BlockSpec((rows, d), lambda l, j: (0, 0)),
            pl.BlockSpec((1, d, tn), lambda l, j: (l, 0, j)),
            pl.BlockSpec((1, 1, tn), lambda l, j: (l, 0, j)),
        ],
        out_specs=pl.BlockSpec((1, rows, tn), lambda l, j: (l, 0, j)),
        out_shape=jax.ShapeDtypeStruct((nl, rows, md), F32),
        compiler_params=_cparams("parallel", "parallel"),
        name="mod_table",
    )(cc, w_mod, b_mod.reshape(nl, 1, md))


def _inproj_kernel(x_ref, mod_ref, g_ref, w_ref, o_ref, h_ref):
    @pl.when(pl.program_id(1) == 0)
    def _():
        x = x_ref[...]
        y = x * lax.rsqrt(jnp.mean(x * x, axis=-1, keepdims=True) + EPS) * g_ref[...]
        m = mod_ref[0]
        h_ref[...] = (y * (1.0 + m[1:2]) + m[0:1]).astype(BF16)

    o_ref[...] = _dot(h_ref[...], w_ref[...])


def _inproj(xs, mod, g, w, n_lat, seq, nb):
    t, d = xs.shape
    width = w.shape[1]
    tm = TOKEN_TILE
    tn = 1024
    row = functools.partial(_mod_row, tile=tm, n_lat=n_lat, seq=seq, nb=nb)
    return pl.pallas_call(
        _inproj_kernel,
        grid=(t // tm, width // tn),
        in_specs=[
            pl.BlockSpec((tm, d), lambda i, j: (i, 0)),
            pl.BlockSpec((1, N_MOD, d), lambda i, j: (row(i), 0, 0)),
            pl.BlockSpec((1, d), lambda i, j: (0, 0)),
            pl.BlockSpec((d, tn), lambda i, j: (0, j)),
        ],
        out_specs=pl.BlockSpec((tm, tn), lambda i, j: (i, j)),
        out_shape=jax.ShapeDtypeStruct((t, width), F32),
        scratch_shapes=[pltpu.VMEM((tm, d), BF16)],
        compiler_params=_cparams("parallel", "arbitrary"),
        name="inproj",
    )(xs, mod, g.reshape(1, d), w)


def _seq_pos(j, rev, ncc, nlc):
    is_ctx = j < ncc
    jl = j - ncc
    cc = (ncc - 1 - j) if rev else j
    cl = (nlc - 1 - jl) if rev else jl
    return is_ctx, cc, cl


def _seq_blk(b, j, rev, ncc, nlc, nb):
    is_ctx, cc, cl = _seq_pos(j, rev, ncc, nlc)
    return jnp.where(is_ctx, nb * nlc + b * ncc + cc, b * nlc + cl)


def _scan_rows(a_ref, b_ref, h_ref, hc_ref, rev):
    n_groups = a_ref.shape[0] // SUBLANES
    width = a_ref.shape[1]
    rows = lax.broadcasted_iota(jnp.int32, (SUBLANES, width), 0)

    def body(g, hc):
        gi = (n_groups - 1 - g) if rev else g
        r0 = pl.multiple_of(gi * SUBLANES, SUBLANES)
        a = a_ref[pl.ds(r0, SUBLANES), :]
        b = b_ref[pl.ds(r0, SUBLANES), :]
        for s in (1, 2, 4):
            if rev:
                a_s = pltpu.roll(a, SUBLANES - s, 0)
                b_s = pltpu.roll(b, SUBLANES - s, 0)
                keep = rows < SUBLANES - s
            else:
                a_s = pltpu.roll(a, s, 0)
                b_s = pltpu.roll(b, s, 0)
                keep = rows >= s
            b = a * jnp.where(keep, b_s, 0.0) + b
            a = a * jnp.where(keep, a_s, 1.0)
        h = b + a * hc
        h_ref[pl.ds(r0, SUBLANES), :] = h
        return h[0:1] if rev else h[SUBLANES - 1:SUBLANES]

    hc_ref[...] = lax.fori_loop(0, n_groups, body, hc_ref[...])


def _lru_kernel(rev, ncc, nlc, *refs):
    if rev:
        (u_ref, prev_ref, next_ref, cw_ref, cb_ref, wa_ref, ba_ref, wi_ref, bi_ref, lam_ref,
         hf_ref, o_ref, xe_ref, a_ref, b_ref, h_ref, hc_ref) = refs
    else:
        (u_ref, prev_ref, next_ref, cw_ref, cb_ref, wa_ref, ba_ref, wi_ref, bi_ref, lam_ref,
         o_ref, xe_ref, a_ref, b_ref, hc_ref) = refs
        h_ref = o_ref
    j = pl.program_id(1)
    is_ctx, cc, cl = _seq_pos(j, rev, ncc, nlc)
    c = jnp.where(is_ctx, cc, cl)
    nseq = jnp.where(is_ctx, ncc, nlc)

    @pl.when(j == 0)
    def _():
        hc_ref[...] = jnp.zeros_like(hc_ref)

    ch = SEQ_CHUNK
    w = LRU_WIDTH
    xe_ref[SUBLANES:SUBLANES + ch, :] = u_ref[:, w:2 * w]
    xe_ref[0:SUBLANES, :] = jnp.where(c > 0, prev_ref[...], 0.0)
    xe_ref[SUBLANES + ch:2 * SUBLANES + ch, :] = jnp.where(c < nseq - 1, next_ref[...], 0.0)
    cw = cw_ref[...]
    xc = cb_ref[...]
    for tap in range(4):
        off = SUBLANES - 2 + tap
        xc = xc + cw[tap:tap + 1] * xe_ref[off:off + ch, :]
    xb = xc.astype(BF16)
    ra = jnp.concatenate(
        [_dot(xb[:, n * LRU_BLOCK:(n + 1) * LRU_BLOCK], wa_ref[n]) for n in range(LRU_BLOCKS)], axis=-1)
    ri = jnp.concatenate(
        [_dot(xb[:, n * LRU_BLOCK:(n + 1) * LRU_BLOCK], wi_ref[n]) for n in range(LRU_BLOCKS)], axis=-1)
    r = _sigmoid(ra + ba_ref[...])
    gate_i = _sigmoid(ri + bi_ref[...])
    neg_lam = -lam_ref[...]
    softplus = jnp.maximum(neg_lam, 0.0) + jnp.log1p(jnp.exp(-jnp.abs(neg_lam)))
    log_a = -LRU_C * r * softplus
    a = jnp.exp(log_a)
    a_ref[...] = a
    b_ref[...] = jnp.sqrt(-jnp.tanh(log_a) * (a * a + 1.0)) * (gate_i * xc)
    _scan_rows(a_ref, b_ref, h_ref, hc_ref, rev)
    if rev:
        y = u_ref[:, 0:w]
        gelu = 0.5 * y * (1.0 + jnp.tanh(0.7978845608028654 * (y + 0.044715 * (y * y * y))))
        o_ref[...] = (gelu * (hf_ref[...] + h_ref[...])).astype(BF16)


def _lru_pass(rev, u, cw, cb, wa, ba, wi, bi, lam, hf, nb, ncc, nlc):
    t = u.shape[0]
    ch = SEQ_CHUNK
    w = LRU_WIDTH
    r8 = ch // SUBLANES
    blk = functools.partial(_seq_blk, rev=rev, ncc=ncc, nlc=nlc, nb=nb)
    full2 = lambda b, j: (0, 0)
    in_specs = [
        pl.BlockSpec((ch, 2 * w), lambda b, j: (blk(b, j), 0)),
        pl.BlockSpec((SUBLANES, w), lambda b, j: (jnp.maximum(blk(b, j) * r8 - 1, 0), 1)),
        pl.BlockSpec((SUBLANES, w), lambda b, j: (jnp.minimum(blk(b, j) * r8 + r8, t // SUBLANES - 1), 1)),
        pl.BlockSpec((4, w), full2),
        pl.BlockSpec((1, w), full2),
        pl.BlockSpec((LRU_BLOCKS, LRU_BLOCK, LRU_BLOCK), lambda b, j: (0, 0, 0)),
        pl.BlockSpec((1, w), full2),
        pl.BlockSpec((LRU_BLOCKS, LRU_BLOCK, LRU_BLOCK), lambda b, j: (0, 0, 0)),
        pl.BlockSpec((1, w), full2),
        pl.BlockSpec((1, w), full2),
    ]
    args = [u, u, u, cw, cb.reshape(1, w), wa.astype(BF16), ba.reshape(1, w), wi.astype(BF16),
            bi.reshape(1, w), lam.reshape(1, w)]
    scratch = [pltpu.VMEM((ch + 2 * SUBLANES, w), F32), pltpu.VMEM((ch, w), F32), pltpu.VMEM((ch, w), F32)]
    if rev:
        in_specs.append(pl.BlockSpec((ch, w), lambda b, j: (blk(b, j), 0)))
        args.append(hf)
        scratch.append(pltpu.VMEM((ch, w), F32))
        out_dtype = BF16
    else:
        out_dtype = F32
    scratch.append(pltpu.VMEM((1, w), F32))
    return pl.pallas_call(
        functools.partial(_lru_kernel, rev, ncc, nlc),
        grid=(nb, ncc + nlc),
        in_specs=in_specs,
        out_specs=pl.BlockSpec((ch, w), lambda b, j: (blk(b, j), 0)),
        out_shape=jax.ShapeDtypeStruct((t, w), out_dtype),
        scratch_shapes=scratch,
        compiler_params=_cparams("parallel", "arbitrary"),
        name="lru_bwd" if rev else "lru_fwd",
    )(*args)


def _cumsum_rows(v, rev):
    n = v.shape[0]
    rows = lax.broadcasted_iota(jnp.int32, v.shape, 0)
    s = 1
    while s < n:
        if rev:
            v = v + jnp.where(rows < n - s, pltpu.roll(v, n - s, 0), 0.0)
        else:
            v = v + jnp.where(rows >= s, pltpu.roll(v, s, 0), 0.0)
        s *= 2
    return v


def _hgrn_diag(q, k, v, bc, rev):
    n = q.shape[0]
    rows = lax.broadcasted_iota(jnp.int32, q.shape, 0)
    o = jnp.zeros_like(q)
    for s in range(n):
        keep = (rows <= s) if rev else (rows >= s)
        dec = jnp.exp(jnp.where(keep, bc - bc[s:s + 1], NEG_BIG))
        att = jnp.sum(q * k[s:s + 1] * dec, axis=-1, keepdims=True)
        o = o + att * v[s:s + 1]
    return o


def _hgrn_chunk(qraw, fl, v, lbh, st, rev):
    cs = HG_CHUNK
    sb = HG_SUB
    nsb = cs // sb
    q = _silu(qraw)
    f = lbh + (1.0 - lbh) * _sigmoid(fl)
    k = 1.0 - f
    bc = _cumsum_rows(jnp.log(f), rev)
    blast = bc[0:1] if rev else bc[cs - 1:cs]
    o = _dot_nt((q * jnp.exp(bc)).astype(BF16), st.astype(BF16))
    kdec = k * jnp.exp(blast - bc)
    st_new = st * jnp.exp(blast) + _dot_tn(v.astype(BF16), kdec.astype(BF16))
    vb = v.astype(BF16)
    parts = [None] * nsb
    for p in range(nsb):
        i = (nsb - 1 - p) if rev else p
        lo, hi = i * sb, (i + 1) * sb
        oi = _hgrn_diag(q[lo:hi], k[lo:hi], v[lo:hi], bc[lo:hi], rev)
        if p > 0:
            if rev:
                mref = bc[hi:hi + 1]
                plo, phi = hi, cs
            else:
                mref = bc[lo - 1:lo]
                plo, phi = 0, lo
            qhat = q[lo:hi] * jnp.exp(bc[lo:hi] - mref)
            kp = k[plo:phi] * jnp.exp(mref - bc[plo:phi])
            att = _dot_nt(qhat.astype(BF16), kp.astype(BF16))
            oi = oi + _dot(att.astype(BF16), vb[plo:phi])
        parts[i] = oi
    return o + jnp.concatenate(parts, axis=0), st_new


def _hgrn_kernel(rev, layer, *refs):
    if rev:
        q_ref, f_ref, v_ref, lbl_ref, og_ref, of_ref, gn_ref, o_ref, st_ref = refs
    else:
        q_ref, f_ref, v_ref, lbl_ref, o_ref, st_ref = refs

    @pl.when(pl.program_id(1) == 0)
    def _():
        st_ref[...] = jnp.zeros_like(st_ref)

    lg = lbl_ref[...]
    e = jnp.exp(lg - jnp.max(lg, axis=0, keepdims=True))
    p = e / jnp.sum(e, axis=0, keepdims=True)
    lb = jnp.zeros_like(p[0:1])
    for jl in range(1, layer + 1):
        lb = lb + p[jl:jl + 1]

    cs = HG_CHUNK
    n_chunks = q_ref.shape[0] // cs

    def body(ci, carry):
        cidx = (n_chunks - 1 - ci) if rev else ci
        r0 = pl.multiple_of(cidx * cs, cs)
        for h in range(HG_HEADS):
            cols = slice(h * HG_DK, (h + 1) * HG_DK)
            o, st_new = _hgrn_chunk(q_ref[pl.ds(r0, cs), cols], f_ref[pl.ds(r0, cs), cols],
                                    v_ref[pl.ds(r0, cs), cols], lb[:, cols], st_ref[h], rev)
            st_ref[h] = st_new
            if rev:
                o = o + of_ref[pl.ds(r0, cs), cols]
                y = o * lax.rsqrt(jnp.mean(o * o, axis=-1, keepdims=True) + EPS) * gn_ref[...]
                o_ref[pl.ds(r0, cs), cols] = (y * _silu(og_ref[pl.ds(r0, cs), cols])).astype(BF16)
            else:
                o_ref[pl.ds(r0, cs), cols] = o
        return carry

    lax.fori_loop(0, n_chunks, body, 0)


def _hgrn_pass(rev, layer, u, lb_logits, gnorm, of, nb, ncc, nlc):
    t = u.shape[0]
    ch = SEQ_CHUNK
    w = HG_WIDTH
    nl = lb_logits.shape[0]
    blk = functools.partial(_seq_blk, rev=rev, ncc=ncc, nlc=nlc, nb=nb)
    base = A_IN // w
    col = lambda cb: pl.BlockSpec((ch, w), lambda b, j: (blk(b, j), cb))
    in_specs = [col(base), col(base + 1 + (1 if rev else 0)), col(base + 3),
                pl.BlockSpec((nl, w), lambda b, j: (0, 0))]
    args = [u, u, u, lb_logits]
    if rev:
        in_specs += [col(base + 4), pl.BlockSpec((ch, w), lambda b, j: (blk(b, j), 0)),
                     pl.BlockSpec((1, HG_DK), lambda b, j: (0, 0))]
        args += [u, of, gnorm.reshape(1, HG_DK)]
    return pl.pallas_call(
        functools.partial(_hgrn_kernel, rev, layer),
        grid=(nb, ncc + nlc),
        in_specs=in_specs,
        out_specs=pl.BlockSpec((ch, w), lambda b, j: (blk(b, j), 0)),
        out_shape=jax.ShapeDtypeStruct((t, w), BF16 if rev else F32),
        scratch_shapes=[pltpu.VMEM((HG_HEADS, HG_DK, HG_DK), F32)],
        compiler_params=_cparams("parallel", "arbitrary"),
        name="hgrn_bwd" if rev else "hgrn_fwd",
    )(*args)


def _rope_tables(n):
    rows = n // GRID_W
    row = jnp.repeat(jnp.arange(rows, dtype=F32), GRID_W)
    colp = jnp.tile(jnp.arange(GRID_W, dtype=F32), rows)
    inv = ROPE_THETA ** (-jnp.arange(ROPE_FREQS, dtype=F32) / ROPE_FREQS)
    ar = row[:, None] * inv
    ac = colp[:, None] * inv
    cos = jnp.concatenate([jnp.cos(ar), jnp.cos(ar), jnp.cos(ac), jnp.cos(ac)], axis=-1)
    sin = jnp.concatenate([-jnp.sin(ar), jnp.sin(ar), -jnp.sin(ac), jnp.sin(ac)], axis=-1)
    cos = jnp.concatenate([cos, jnp.ones((SEQ_CHUNK, HEAD_DIM), F32)], axis=0)
    sin = jnp.concatenate([sin, jnp.zeros((SEQ_CHUNK, HEAD_DIM), F32)], axis=0)
    return cos, sin


def _qkv_kernel(qa_ref, qb_ref, kv_ref, cos_ref, sin_ref, qn_ref, kn_ref, q_out, k_out, v_out):
    cos = cos_ref[...]
    sin = sin_ref[...]
    lane = lax.broadcasted_iota(jnp.int32, cos.shape, 1)
    first = (lane % (2 * ROPE_FREQS)) < ROPE_FREQS

    def norm_rope(v, g):
        y = v * lax.rsqrt(jnp.mean(v * v, axis=-1, keepdims=True) + EPS) * g
        partner = jnp.where(first, pltpu.roll(y, HEAD_DIM - ROPE_FREQS, 1), pltpu.roll(y, ROPE_FREQS, 1))
        return (y * cos + partner * sin).astype(BF16)

    half = N_Q_HEADS // 2
    for h in range(half):
        cols = slice(h * HEAD_DIM, (h + 1) * HEAD_DIM)
        q_out[:, cols] = norm_rope(qa_ref[:, cols], qn_ref[...])
        cols_b = slice((half + h) * HEAD_DIM, (half + h + 1) * HEAD_DIM)
        q_out[:, cols_b] = norm_rope(qb_ref[:, cols], qn_ref[...])
    for h in range(N_KV_HEADS):
        cols = slice(h * HEAD_DIM, (h + 1) * HEAD_DIM)
        k_out[:, cols] = norm_rope(kv_ref[:, cols], kn_ref[...])
    v_out[...] = kv_ref[:, KV_WIDTH:2 * KV_WIDTH].astype(BF16)


def _qkv(u, cos, sin, qn, kn, n_lat, nlc):
    t = u.shape[0]
    ch = SEQ_CHUNK
    cw = 512
    base = (A_IN + B_IN) // cw
    tab = lambda i: (jnp.where(i < n_lat // ch, i % nlc, nlc), 0)
    return pl.pallas_call(
        _qkv_kernel,
        grid=(t // ch,),
        in_specs=[
            pl.BlockSpec((ch, cw), lambda i: (i, base)),
            pl.BlockSpec((ch, cw), lambda i: (i, base + 1)),
            pl.BlockSpec((ch, cw), lambda i: (i, base + 2)),
            pl.BlockSpec((ch, HEAD_DIM), tab),
            pl.BlockSpec((ch, HEAD_DIM), tab),
            pl.BlockSpec((1, HEAD_DIM), lambda i: (0, 0)),
            pl.BlockSpec((1, HEAD_DIM), lambda i: (0, 0)),
        ],
        out_specs=[
            pl.BlockSpec((ch, ATT_WIDTH), lambda i: (i, 0)),
            pl.BlockSpec((ch, KV_WIDTH), lambda i: (i, 0)),
            pl.BlockSpec((ch, KV_WIDTH), lambda i: (i, 0)),
        ],
        out_shape=[
            jax.ShapeDtypeStruct((t, ATT_WIDTH), BF16),
            jax.ShapeDtypeStruct((t, KV_WIDTH), BF16),
            jax.ShapeDtypeStruct((t, KV_WIDTH), BF16),
        ],
        compiler_params=_cparams("parallel"),
        name="qkv_prep",
    )(u, u, u, cos, sin, qn.reshape(1, HEAD_DIM), kn.reshape(1, HEAD_DIM))


def _attn_kernel(with_lat, *refs):
    if with_lat:
        q_ref, kc_ref, vc_ref, kl_ref, vl_ref, o_ref, m_ref, l_ref, acc_ref = refs
    else:
        q_ref, kc_ref, vc_ref, o_ref, m_ref, l_ref, acc_ref = refs
    tq = q_ref.shape[0]
    tk = kc_ref.shape[0]

    def step(qs, kt, vt):
        s = _dot_nt(qs, kt) * ATTN_SCALE
        m_prev = m_ref[...]
        m_new = jnp.maximum(m_prev, jnp.max(s, axis=-1, keepdims=True))
        p = jnp.exp(s - m_new[:, 0:1])
        alpha = jnp.exp(m_prev - m_new)
        l_ref[...] = alpha * l_ref[...] + jnp.sum(p, axis=-1, keepdims=True)
        acc_ref[...] = alpha * acc_ref[...] + _dot(p.astype(BF16), vt)
        m_ref[...] = m_new

    for g in range(N_KV_HEADS):
        gcols = slice(g * HEAD_DIM, (g + 1) * HEAD_DIM)
        qs = jnp.concatenate(
            [q_ref[:, (g * Q_PER_KV + h) * HEAD_DIM:(g * Q_PER_KV + h + 1) * HEAD_DIM] for h in range(Q_PER_KV)],
            axis=0)
        m_ref[...] = jnp.full_like(m_ref, -jnp.inf)
        l_ref[...] = jnp.zeros_like(l_ref)
        acc_ref[...] = jnp.zeros_like(acc_ref)
        step(qs, kc_ref[:, gcols], vc_ref[:, gcols])
        if with_lat:
            def body(i, carry):
                r0 = pl.multiple_of(i * tk, tk)
                step(qs, kl_ref[pl.ds(r0, tk), gcols], vl_ref[pl.ds(r0, tk), gcols])
                return carry
            lax.fori_loop(0, kl_ref.shape[0] // tk, body, 0)
        o = acc_ref[...] / l_ref[...]
        for h in range(Q_PER_KV):
            cols = slice((g * Q_PER_KV + h) * HEAD_DIM, (g * Q_PER_KV + h + 1) * HEAD_DIM)
            o_ref[:, cols] = o[h * tq:(h + 1) * tq].astype(BF16)


def _attn(q, k, v, nb, n, nc, latent):
    n_lat = nb * n
    if latent:
        tq = 128
        grid = (nb, n // tq)
        q_spec = pl.BlockSpec((tq, ATT_WIDTH), lambda b, i: (b * (n // tq) + i, 0))
        rows = n_lat
    else:
        tq = nc
        grid = (nb, 1)
        q_spec = pl.BlockSpec((tq, ATT_WIDTH), lambda b, i: (n_lat // nc + b, 0))
        rows = nb * nc
    ctx_spec = pl.BlockSpec((nc, KV_WIDTH), lambda b, i: (n_lat // nc + b, 0))
    in_specs = [q_spec, ctx_spec, ctx_spec]
    args = [q, k, v]
    if latent:
        lat_spec = pl.BlockSpec((n, KV_WIDTH), lambda b, i: (b, 0))
        in_specs += [lat_spec, lat_spec]
        args += [k, v]
    out_map = (lambda b, i: (b * (n // tq) + i, 0)) if latent else (lambda b, i: (b, 0))
    m = Q_PER_KV * tq
    return pl.pallas_call(
        functools.partial(_attn_kernel, latent),
        grid=grid,
        in_specs=in_specs,
        out_specs=pl.BlockSpec((tq, ATT_WIDTH), out_map),
        out_shape=jax.ShapeDtypeStruct((rows, ATT_WIDTH), BF16),
        scratch_shapes=[pltpu.VMEM((m, LANES), F32), pltpu.VMEM((m, LANES), F32), pltpu.VMEM((m, HEAD_DIM), F32)],
        compiler_params=_cparams("parallel", "parallel"),
        name="attn_lat" if latent else "attn_ctx",
    )(*args)


def _outproj_kernel(routed, *refs):
    if routed:
        ra_ref, hg_ref, at_ref, x_ref, mod_ref, w_ref, g_ref, rt_ref, xo_ref, h_ref, lg_ref = refs
    else:
        ra_ref, hg_ref, at_ref, x_ref, mod_ref, w_ref, g_ref, xo_ref, h_ref = refs
    w1 = ra_ref.shape[1]
    w2 = w1 + hg_ref.shape[1]
    mix = (_dot(ra_ref[...], w_ref[0:w1, :]) + _dot(hg_ref[...], w_ref[w1:w2, :])
           + _dot(at_ref[...], w_ref[w2:, :]))
    m = mod_ref[0]
    x = x_ref[...] + m[2:3] * mix
    xo_ref[...] = x
    y = x * lax.rsqrt(jnp.mean(x * x, axis=-1, keepdims=True) + EPS) * g_ref[...]
    h = y * (1.0 + m[4:5]) + m[3:4]
    if routed:
        h_ref[...] = h
        lg_ref[...] = jnp.dot(h, rt_ref[...], preferred_element_type=F32, precision=lax.Precision.HIGHEST)
    else:
        h_ref[...] = h.astype(BF16)


def _outproj(ra, hg, at, xs, mod, w, g, router, n_tok, n_lat, seq, nb):
    d = xs.shape[1]
    tm = 256
    routed = router is not None
    row = functools.partial(_mod_row, tile=tm, n_lat=n_lat, seq=seq, nb=nb)
    tok = lambda width: pl.BlockSpec((tm, width), lambda i: (i, 0))
    in_specs = [tok(ra.shape[1]), tok(hg.shape[1]), tok(at.shape[1]), tok(d),
                pl.BlockSpec((1, N_MOD, d), lambda i: (row(i), 0, 0)),
                pl.BlockSpec(w.shape, lambda i: (0, 0)),
                pl.BlockSpec((1, d), lambda i: (0, 0))]
    args = [ra, hg, at, xs, mod, w, g.reshape(1, d)]
    out_specs = [tok(d), tok(d)]
    out_shape = [jax.ShapeDtypeStruct((n_tok, d), F32), jax.ShapeDtypeStruct((n_tok, d), F32 if routed else BF16)]
    if routed:
        in_specs.append(pl.BlockSpec((d, LANES), lambda i: (0, 0)))
        args.append(router)
        out_specs.append(tok(LANES))
        out_shape.append(jax.ShapeDtypeStruct((n_tok, LANES), F32))
    return pl.pallas_call(
        functools.partial(_outproj_kernel, routed),
        grid=(n_tok // tm,),
        in_specs=in_specs,
        out_specs=out_specs,
        out_shape=out_shape,
        compiler_params=_cparams("parallel"),
        name="outproj",
    )(*args)


def _ffn_kernel(h_ref, x_ref, mod_ref, w1_ref, w3_ref, w2_ref, o_ref, acc_ref):
    k = pl.program_id(1)

    @pl.when(k == 0)
    def _():
        acc_ref[...] = jnp.zeros_like(acc_ref)

    h = h_ref[...]
    a = _dot(h, w1_ref[...])
    z = _silu(a) * _dot(h, w3_ref[...])
    acc_ref[...] += _dot(z.astype(BF16), w2_ref[...])

    @pl.when(k == pl.num_programs(1) - 1)
    def _():
        o_ref[...] = x_ref[...] + mod_ref[0][5:6] * acc_ref[...]


def _ffn(h, xs, mod, w1, w3, w2, n_lat, seq, nb):
    t, d = h.shape
    f = w1.shape[1]
    tm = TOKEN_TILE
    tf = 512
    row = functools.partial(_mod_row, tile=tm, n_lat=n_lat, seq=seq, nb=nb)
    return pl.pallas_call(
        _ffn_kernel,
        grid=(t // tm, f // tf),
        in_specs=[
            pl.BlockSpec((tm, d), lambda i, k: (i, 0)),
            pl.BlockSpec((tm, d), lambda i, k: (i, 0)),
            pl.BlockSpec((1, N_MOD, d), lambda i, k: (row(i), 0, 0)),
            pl.BlockSpec((d, tf), lambda i, k: (0, k)),
            pl.BlockSpec((d, tf), lambda i, k: (0, k)),
            pl.BlockSpec((tf, d), lambda i, k: (k, 0)),
        ],
        out_specs=pl.BlockSpec((tm, d), lambda i, k: (i, 0)),
        out_shape=jax.ShapeDtypeStruct((t, d), F32),
        scratch_shapes=[pltpu.VMEM((tm, d), F32)],
        compiler_params=_cparams("parallel", "arbitrary"),
        name="ffn_dense",
    )(h, xs, mod, w1, w3, w2)


def _router_kernel(lg_ref, meta_ref, wts_ref, cnt_ref, run_ref):
    @pl.when(pl.program_id(0) == 0)
    def _():
        run_ref[...] = jnp.zeros_like(run_ref)

    lg = lg_ref[...]
    tm = lg.shape[0]
    lane = lax.broadcasted_iota(jnp.int32, lg.shape, 1)
    lane_f = lane.astype(F32)
    v = jnp.where(lane < N_EXPERTS, lg, -jnp.inf)
    m1 = jnp.max(v, axis=-1, keepdims=True)
    i1 = jnp.min(jnp.where(v == m1, lane_f, float(LANES)), axis=-1, keepdims=True)
    v2 = jnp.where(lane_f == i1, -jnp.inf, v)
    m2 = jnp.max(v2, axis=-1, keepdims=True)
    i2 = jnp.min(jnp.where(v2 == m2, lane_f, float(LANES)), axis=-1, keepdims=True)
    e = jnp.exp(m2 - m1)
    wt1 = 1.0 / (1.0 + e)
    wt2 = e / (1.0 + e)
    hit1 = lane_f == i1
    hit2 = lane_f == i2
    assign = jnp.where(hit1 | hit2, 1.0, 0.0)
    r = lax.broadcasted_iota(jnp.int32, (tm, tm), 0)
    c = lax.broadcasted_iota(jnp.int32, (tm, tm), 1)
    tri = jnp.where(r > c, 1.0, 0.0).astype(BF16)
    rank = _dot(tri, assign.astype(BF16)) + run_ref[0:1, :]
    r1 = jnp.sum(jnp.where(hit1, rank, 0.0), axis=-1, keepdims=True)
    r2 = jnp.sum(jnp.where(hit2, rank, 0.0), axis=-1, keepdims=True)
    run_ref[...] = run_ref[...] + jnp.sum(assign, axis=0, keepdims=True)
    meta = jnp.where(lane == 0, i1, jnp.where(lane == 1, i2, jnp.where(lane == 2, r1, jnp.where(lane == 3, r2, 0.0))))
    meta_ref[...] = meta.astype(jnp.int32)
    wts_ref[...] = jnp.where(lane == 0, wt1, jnp.where(lane == 1, wt2, 0.0))
    cnt_ref[...] = run_ref[...]


def _router(logits):
    t = logits.shape[0]
    tm = TOKEN_TILE
    tok = pl.BlockSpec((tm, LANES), lambda i: (i, 0))
    return pl.pallas_call(
        _router_kernel,
        grid=(t // tm,),
        in_specs=[tok],
        out_specs=[tok, tok, pl.BlockSpec((SUBLANES, LANES), lambda i: (0, 0))],
        out_shape=[jax.ShapeDtypeStruct((t, LANES), jnp.int32), jax.ShapeDtypeStruct((t, LANES), F32),
                   jax.ShapeDtypeStruct((SUBLANES, LANES), F32)],
        scratch_shapes=[pltpu.VMEM((SUBLANES, LANES), F32)],
        compiler_params=_cparams("arbitrary"),
        name="router",
    )(logits)


def _row_copy(src_ref, src_row, dst_ref, dst_row, sem):
    return pltpu.make_async_copy(src_ref.at[pl.ds(src_row, 1)], dst_ref.at[pl.ds(dst_row, 1)], sem)


def _dispatch_kernel(pos_ref, h_ref, xs_in_ref, xs_ref, pos_smem, sem_p, sem_d):
    del xs_in_ref
    tm = h_ref.shape[0]
    cp = pltpu.make_async_copy(pos_ref.at[0, 0], pos_smem, sem_p)
    cp.start()
    cp.wait()

    def issue(t, carry):
        _row_copy(h_ref, t, xs_ref, pos_smem[t], sem_d).start()
        _row_copy(h_ref, t, xs_ref, pos_smem[tm + t], sem_d).start()
        return carry

    lax.fori_loop(0, tm, issue, 0)

    def drain(t, carry):
        _row_copy(h_ref, 0, xs_ref, 0, sem_d).wait()
        _row_copy(h_ref, 0, xs_ref, 0, sem_d).wait()
        return carry

    lax.fori_loop(0, tm, drain, 0)


def _dispatch(pos, h, xs0):
    t, d = h.shape
    tm = 256
    nt = t // tm
    return pl.pallas_call(
        _dispatch_kernel,
        grid=(nt,),
        in_specs=[
            pl.BlockSpec((1, 1, 2 * tm), lambda i: (i, 0, 0)),
            pl.BlockSpec((tm, d), lambda i: (i, 0)),
            pl.BlockSpec(memory_space=pl.ANY),
        ],
        out_specs=pl.BlockSpec(memory_space=pl.ANY),
        out_shape=jax.ShapeDtypeStruct(xs0.shape, xs0.dtype),
        scratch_shapes=[pltpu.SMEM((2 * tm,), jnp.int32), pltpu.SemaphoreType.DMA(()), pltpu.SemaphoreType.DMA(())],
        input_output_aliases={2: 0},
        compiler_params=_cparams("arbitrary"),
        name="moe_dispatch",
    )(pos, h, xs0)


def _expert_kernel(te_ref, tv_ref, x_ref, w1_ref, w3_ref, w2_ref, y_ref, xb_ref, acc_ref):
    del te_ref
    i = pl.program_id(0)
    k = pl.program_id(1)

    @pl.when(tv_ref[i] > 0)
    def _():
        @pl.when(k == 0)
        def _():
            xb_ref[...] = x_ref[...].astype(BF16)
            acc_ref[...] = jnp.zeros_like(acc_ref)

        xb = xb_ref[...]
        a = _dot(xb, w1_ref[...])
        z = _silu(a) * _dot(xb, w3_ref[...])
        acc_ref[...] += _dot(z.astype(BF16), w2_ref[...])

        @pl.when(k == pl.num_programs(1) - 1)
        def _():
            y_ref[...] = acc_ref[...]

    @pl.when((tv_ref[i] == 0) & (k == 0))
    def _():
        y_ref[...] = jnp.zeros_like(y_ref)


def _experts(tile_expert, tile_valid, tile_row, xs, w1, w3, w2):
    rows, d = xs.shape
    f = w1.shape[2]
    tm = EXPERT_TILE
    tf = 512
    grid_spec = pltpu.PrefetchScalarGridSpec(
        num_scalar_prefetch=3,
        grid=(rows // tm, f // tf),
        in_specs=[
            pl.BlockSpec((tm, d), lambda i, k, te, tv, tr: (tr[i], 0)),
            pl.BlockSpec((None, d, tf), lambda i, k, te, tv, tr: (te[i], 0, jnp.where(tv[i] > 0, k, f // tf - 1))),
            pl.BlockSpec((None, d, tf), lambda i, k, te, tv, tr: (te[i], 0, jnp.where(tv[i] > 0, k, f // tf - 1))),
            pl.BlockSpec((None, tf, d), lambda i, k, te, tv, tr: (te[i], jnp.where(tv[i] > 0, k, f // tf - 1), 0)),
        ],
        out_specs=pl.BlockSpec((tm, d), lambda i, k, te, tv, tr: (i, 0)),
        scratch_shapes=[pltpu.VMEM((tm, d), BF16), pltpu.VMEM((tm, d), F32)],
    )

    def body(te_ref, tv_ref, tr_ref, *rest):
        del tr_ref
        _expert_kernel(te_ref, tv_ref, *rest)

    return pl.pallas_call(
        body,
        grid_spec=grid_spec,
        out_shape=jax.ShapeDtypeStruct((rows, d), F32),
        compiler_params=_cparams("arbitrary", "arbitrary"),
        name="moe_experts",
    )(tile_expert, tile_valid, tile_row, xs, w1, w3, w2)


def _combine_kernel(pos_ref, wts_ref, x_ref, mod_ref, ys_ref, o_ref, pos_smem, y1_ref, y2_ref, sem_p, sem_g):
    tm = x_ref.shape[0]
    cp = pltpu.make_async_copy(pos_ref.at[0, 0], pos_smem, sem_p)
    cp.start()
    cp.wait()

    def issue(t, carry):
        _row_copy(ys_ref, pos_smem[t], y1_ref, t, sem_g).start()
        _row_copy(ys_ref, pos_smem[tm + t], y2_ref, t, sem_g).start()
        return carry

    lax.fori_loop(0, tm, issue, 0)

    def drain(t, carry):
        _row_copy(ys_ref, 0, y1_ref, 0, sem_g).wait()
        _row_copy(ys_ref, 0, y2_ref, 0, sem_g).wait()
        return carry

    lax.fori_loop(0, tm, drain, 0)
    wts = wts_ref[...]
    f = wts[:, 0:1] * y1_ref[...] + wts[:, 1:2] * y2_ref[...]
    o_ref[...] = x_ref[...] + mod_ref[0][5:6] * f


def _combine(pos, wts, xs, mod, ys, n_lat, seq, nb):
    t, d = xs.shape
    tm = 256
    row = functools.partial(_mod_row, tile=tm, n_lat=n_lat, seq=seq, nb=nb)
    return pl.pallas_call(
        _combine_kernel,
        grid=(t // tm,),
        in_specs=[
            pl.BlockSpec((1, 1, 2 * tm), lambda i: (i, 0, 0)),
            pl.BlockSpec((tm, LANES), lambda i: (i, 0)),
            pl.BlockSpec((tm, d), lambda i: (i, 0)),
            pl.BlockSpec((1, N_MOD, d), lambda i: (row(i), 0, 0)),
            pl.BlockSpec(memory_space=pl.ANY),
        ],
        out_specs=pl.BlockSpec((tm, d), lambda i: (i, 0)),
        out_shape=jax.ShapeDtypeStruct((t, d), F32),
        scratch_shapes=[pltpu.SMEM((2 * tm,), jnp.int32), pltpu.VMEM((tm, d), F32), pltpu.VMEM((tm, d), F32),
                        pltpu.SemaphoreType.DMA(()), pltpu.SemaphoreType.DMA(())],
        compiler_params=_cparams("arbitrary"),
        name="moe_combine",
    )(pos, wts, xs, mod, ys)


def _moe(h, logits, xs, mod, w1, w3, w2, n_lat, seq, nb):
    t, d = h.shape
    meta, wts, counts = _router(logits)
    te_rows = EXPERT_TILE
    cnt = counts[0, :N_EXPERTS].astype(jnp.int32)
    padded = ((cnt + te_rows - 1) // te_rows) * te_rows
    ends = jnp.cumsum(padded)
    starts = ends - padded
    pos1 = jnp.take(starts, meta[:, 0]) + meta[:, 2]
    pos2 = jnp.take(starts, meta[:, 1]) + meta[:, 3]
    tmd = 256
    pos = jnp.concatenate([pos1.reshape(t // tmd, 1, tmd), pos2.reshape(t // tmd, 1, tmd)], axis=-1)
    n_tiles = (2 * t) // te_rows + N_EXPERTS
    tile_start = jnp.arange(n_tiles, dtype=jnp.int32) * te_rows
    n_valid = ends[-1] // te_rows
    tile_valid = (tile_start < ends[-1]).astype(jnp.int32)
    tile_row = jnp.minimum(jnp.arange(n_tiles, dtype=jnp.int32), n_valid - 1)
    tile_expert = jnp.sum((tile_row[:, None] * te_rows >= ends[None, :]).astype(jnp.int32), axis=1)
    xs0 = jnp.zeros((n_tiles * te_rows, d), F32)
    x_sorted = _dispatch(pos, h, xs0)
    y_sorted = _experts(tile_expert, tile_valid, tile_row, x_sorted, w1, w3, w2)
    return _combine(pos, wts, xs, mod, y_sorted, n_lat, seq, nb)


def kernel(x, c, ctx, c_ctx, w_mod, b_mod, norm1, norm2, w_in, w_out, conv_w, conv_b, lru_wa, lru_ba, lru_wi, lru_bi, lru_lambda, hgrn_lb_logits, hgrn_gnorm, q_norm, k_norm, ffn_w1, ffn_w3, ffn_w2, router, moe_w1, moe_w3, moe_w2):
    nb, n, d = x.shape
    nc = ctx.shape[1]
    depth = w_mod.shape[0]
    assert n % SEQ_CHUNK == 0 and nc % SEQ_CHUNK == 0 and n % TOKEN_TILE == 0
    assert (nb * nc) % TOKEN_TILE == 0 and n % GRID_W == 0
    n_lat = nb * n
    nlc = n // SEQ_CHUNK
    ncc = nc // SEQ_CHUNK

    xs = jnp.concatenate([x.reshape(n_lat, d), ctx.reshape(nb * nc, d)], axis=0)
    mod_rows = 2 * SUBLANES * ((nb + 1 + 2 * SUBLANES - 1) // (2 * SUBLANES))
    cc = jnp.zeros((mod_rows, d), F32).at[:nb].set(c).at[nb].set(c_ctx)
    mod_all = _mod_table(cc, w_mod, b_mod).reshape(depth, mod_rows, N_MOD, d)
    cos, sin = _rope_tables(n)
    router_p = jnp.pad(router, ((0, 0), (0, 0), (0, LANES - router.shape[-1])))

    for l in range(depth):
        need_ctx = l < depth - 1
        mod = mod_all[l]
        u = _inproj(xs, mod, norm1[l], w_in[l].astype(BF16), n_lat, n, nb)

        lru_args = lambda dd: (conv_w[l], conv_b[l], lru_wa[l, dd], lru_ba[l, dd], lru_wi[l, dd], lru_bi[l, dd],
                               lru_lambda[l, dd])
        hf = _lru_pass(False, u, *lru_args(0), None, nb, ncc, nlc)
        ra = _lru_pass(True, u, *lru_args(1), hf, nb, ncc, nlc)

        of = _hgrn_pass(False, l, u, hgrn_lb_logits[0], hgrn_gnorm[l], None, nb, ncc, nlc)
        hg = _hgrn_pass(True, l, u, hgrn_lb_logits[1], hgrn_gnorm[l], of, nb, ncc, nlc)

        q, k, v = _qkv(u, cos, sin, q_norm[l], k_norm[l], n_lat, nlc)
        at = _attn(q, k, v, nb, n, nc, True)
        if need_ctx:
            at = jnp.concatenate([at, _attn(q, k, v, nb, n, nc, False)], axis=0)

        n_tok = xs.shape[0] if need_ctx else n_lat
        j = l // 2
        routed = l % 2 == 1
        outs = _outproj(ra, hg, at, xs, mod, w_out[l].astype(BF16), norm2[l], router_p[j] if routed else None,
                        n_tok, n_lat, n, nb)
        if routed:
            xn, h2, logits = outs
            xs = _moe(h2, logits, xn, mod, moe_w1[j].astype(BF16), moe_w3[j].astype(BF16),
                      moe_w2[j].astype(BF16), n_lat, n, nb)
        else:
            xn, h2 = outs
            xs = _ffn(h2, xn, mod, ffn_w1[j].astype(BF16), ffn_w3[j].astype(BF16), ffn_w2[j].astype(BF16),
                      n_lat, n, nb)
    return xs[:n_lat].reshape(nb, n, d)
```

```python
import functools

import jax
import jax.numpy as jnp
from jax import lax
from jax.experimental import pallas as pl
from jax.experimental.pallas import tpu as pltpu

F32 = jnp.float32
BF16 = jnp.bfloat16

EPS = 1e-6
N_MOD = 6
GRID_W = 64
LRU_WIDTH = 512
LRU_BLOCKS = 4
LRU_BLOCK = LRU_WIDTH // LRU_BLOCKS
LRU_C = 8.0
HG_HEADS = 4
HG_DK = 128
HG_WIDTH = HG_HEADS * HG_DK
HG_CHUNK = 64
HG_SUB = 16
HEAD_DIM = 128
N_Q_HEADS = 8
N_KV_HEADS = 2
Q_PER_KV = N_Q_HEADS // N_KV_HEADS
ATT_WIDTH = N_Q_HEADS * HEAD_DIM
KV_WIDTH = N_KV_HEADS * HEAD_DIM
ROPE_THETA = 10000.0
ROPE_FREQS = HEAD_DIM // 4
ATTN_SCALE = HEAD_DIM ** -0.5
LOG2E = 1.4426950408889634
Q_SCALE = ATTN_SCALE * LOG2E
A_IN = 2 * LRU_WIDTH
B_IN = 5 * HG_WIDTH
C_IN = ATT_WIDTH + 2 * KV_WIDTH
IN_WIDTH = A_IN + B_IN + C_IN
N_EXPERTS = 8

LANES = 128
SUBLANES = 8
SEQ_CHUNK = 256
TOKEN_TILE = 512
EXPERT_TILE = 512
ATT_KEY_TILE = 512
VMEM_LIMIT = 56 * 1024 * 1024
NEG_BIG = -1e30


def _cparams(*sem):
    return pltpu.CompilerParams(dimension_semantics=sem, vmem_limit_bytes=VMEM_LIMIT)


def _sigmoid(v):
    return 1.0 / (1.0 + jnp.exp(-v))


def _silu(v):
    return v * _sigmoid(v)


def _dot(a, b):
    return jnp.dot(a, b, preferred_element_type=F32)


def _dot_nt(a, b):
    return lax.dot_general(a, b, (((1,), (1,)), ((), ())), preferred_element_type=F32)


def _dot_tn(a, b):
    return lax.dot_general(a, b, (((0,), (0,)), ((), ())), preferred_element_type=F32)


def _mod_row(i, tile, n_lat, seq, nb):
    return jnp.where(i < n_lat // tile, (i * tile) // seq, nb)


def _mod_kernel(c_ref, w_ref, b_ref, o_ref):
    s = _silu(c_ref[...]).astype(BF16)
    o_ref[0] = _dot(s, w_ref[0].astype(BF16)) + b_ref[0]


def _mod_table(cc, w_mod, b_mod):
    nl, d, md = w_mod.shape
    rows = cc.shape[0]
    tn = 1024 if md % 1024 == 0 else md
    return pl.pallas_call(
        _mod_kernel,
        grid=(nl, md // tn),
        in_specs=[
            pl.BlockSpec((rows, d), lambda l, j: (0, 0)),
            pl.BlockSpec((1, d, tn), lambda l, j: (l, 0, j)),
            pl.BlockSpec((1, 1, tn), lambda l, j: (l, 0, j)),
        ],
        out_specs=pl.BlockSpec((1, rows, tn), lambda l, j: (l, 0, j)),
        out_shape=jax.ShapeDtypeStruct((nl, rows, md), F32),
        compiler_params=_cparams("parallel", "parallel"),
        name="mod_table",
    )(cc, w_mod, b_mod.reshape(nl, 1, md))


def _inproj_kernel(x_ref, mod_ref, g_ref, w_ref, o_ref, h_ref):
    @pl.when(pl.program_id(1) == 0)
    def _():
        x = x_ref[...]
        y = x * lax.rsqrt(jnp.mean(x * x, axis=-1, keepdims=True) + EPS) * g_ref[...]
        m = mod_ref[0]
        h_ref[...] = (y * (1.0 + m[1:2]) + m[0:1]).astype(BF16)

    o_ref[...] = _dot(h_ref[...], w_ref[...])


def _inproj(xs, mod, g, w, n_lat, seq, nb):
    t, d = xs.shape
    width = w.shape[1]
    tm = TOKEN_TILE
    tn = 1024
    row = functools.partial(_mod_row, tile=tm, n_lat=n_lat, seq=seq, nb=nb)
    return pl.pallas_call(
        _inproj_kernel,
        grid=(t // tm, width // tn),
        in_specs=[
            pl.BlockSpec((tm, d), lambda i, j: (i, 0)),
            pl.BlockSpec((1, N_MOD, d), lambda i, j: (row(i), 0, 0)),
            pl.BlockSpec((1, d), lambda i, j: (0, 0)),
            pl.BlockSpec((d, tn), lambda i, j: (0, j)),
        ],
        out_specs=pl.BlockSpec((tm, tn), lambda i, j: (i, j)),
        out_shape=jax.ShapeDtypeStruct((t, width), F32),
        scratch_shapes=[pltpu.VMEM((tm, d), BF16)],
        compiler_params=_cparams("parallel", "arbitrary"),
        name="inproj",
    )(xs, mod, g.reshape(1, d), w)


def _seq_pos(j, rev, ncc, nlc):
    is_ctx = j < ncc
    jl = j - ncc
    cc = (ncc - 1 - j) if rev else j
    cl = (nlc - 1 - jl) if rev else jl
    return is_ctx, cc, cl


def _seq_blk(b, j, rev, ncc, nlc, nb):
    is_ctx, cc, cl = _seq_pos(j, rev, ncc, nlc)
    return jnp.where(is_ctx, nb * nlc + b * ncc + cc, b * nlc + cl)


def _scan_rows(a_ref, b_ref, h_ref, hc_ref, rev):
    n_groups = a_ref.shape[0] // SUBLANES
    width = a_ref.shape[1]
    rows = lax.broadcasted_iota(jnp.int32, (SUBLANES, width), 0)

    def body(g, hc):
        gi = (n_groups - 1 - g) if rev else g
        r0 = pl.multiple_of(gi * SUBLANES, SUBLANES)
        a = a_ref[pl.ds(r0, SUBLANES), :]
        b = b_ref[pl.ds(r0, SUBLANES), :]
        for s in (1, 2, 4):
            if rev:
                a_s = pltpu.roll(a, SUBLANES - s, 0)
                b_s = pltpu.roll(b, SUBLANES - s, 0)
                keep = rows < SUBLANES - s
            else:
                a_s = pltpu.roll(a, s, 0)
                b_s = pltpu.roll(b, s, 0)
                keep = rows >= s
            b = a * jnp.where(keep, b_s, 0.0) + b
            a = a * jnp.where(keep, a_s, 1.0)
        h = b + a * hc
        h_ref[pl.ds(r0, SUBLANES), :] = h
        return h[0:1] if rev else h[SUBLANES - 1:SUBLANES]

    hc_ref[...] = lax.fori_loop(0, n_groups, body, hc_ref[...])


def _lru_kernel(rev, ncc, nlc, *refs):
    if rev:
        (u_ref, prev_ref, next_ref, cw_ref, cb_ref, wa_ref, ba_ref, wi_ref, bi_ref, lam_ref,
         hf_ref, o_ref, xe_ref, a_ref, b_ref, h_ref, hc_ref) = refs
    else:
        (u_ref, prev_ref, next_ref, cw_ref, cb_ref, wa_ref, ba_ref, wi_ref, bi_ref, lam_ref,
         o_ref, xe_ref, a_ref, b_ref, hc_ref) = refs
        h_ref = o_ref
    j = pl.program_id(1)
    is_ctx, cc, cl = _seq_pos(j, rev, ncc, nlc)
    c = jnp.where(is_ctx, cc, cl)
    nseq = jnp.where(is_ctx, ncc, nlc)

    @pl.when(j == 0)
    def _():
        hc_ref[...] = jnp.zeros_like(hc_ref)

    ch = SEQ_CHUNK
    w = LRU_WIDTH
    xe_ref[SUBLANES:SUBLANES + ch, :] = u_ref[:, w:2 * w]
    xe_ref[0:SUBLANES, :] = jnp.where(c > 0, prev_ref[...], 0.0)
    xe_ref[SUBLANES + ch:2 * SUBLANES + ch, :] = jnp.where(c < nseq - 1, next_ref[...], 0.0)
    cw = cw_ref[...]
    xc = cb_ref[...]
    for tap in range(4):
        off = SUBLANES - 2 + tap
        xc = xc + cw[tap:tap + 1] * xe_ref[off:off + ch, :]
    xb = xc.astype(BF16)
    ra = jnp.concatenate(
        [_dot(xb[:, n * LRU_BLOCK:(n + 1) * LRU_BLOCK], wa_ref[n]) for n in range(LRU_BLOCKS)], axis=-1)
    ri = jnp.concatenate(
        [_dot(xb[:, n * LRU_BLOCK:(n + 1) * LRU_BLOCK], wi_ref[n]) for n in range(LRU_BLOCKS)], axis=-1)
    r = _sigmoid(ra + ba_ref[...])
    gate_i = _sigmoid(ri + bi_ref[...])
    neg_lam = -lam_ref[...]
    softplus = jnp.maximum(neg_lam, 0.0) + jnp.log1p(jnp.exp(-jnp.abs(neg_lam)))
    log_a = -LRU_C * r * softplus
    a = jnp.exp(log_a)
    a_ref[...] = a
    b_ref[...] = jnp.sqrt(-jnp.tanh(log_a) * (a * a + 1.0)) * (gate_i * xc)
    _scan_rows(a_ref, b_ref, h_ref, hc_ref, rev)
    if rev:
        y = u_ref[:, 0:w]
        gelu = 0.5 * y * (1.0 + jnp.tanh(0.7978845608028654 * (y + 0.044715 * (y * y * y))))
        o_ref[...] = (gelu * (hf_ref[...] + h_ref[...])).astype(BF16)


def _lru_pass(rev, u, cw, cb, wa, ba, wi, bi, lam, hf, nb, ncc, nlc):
    t = u.shape[0]
    ch = SEQ_CHUNK
    w = LRU_WIDTH
    r8 = ch // SUBLANES
    blk = functools.partial(_seq_blk, rev=rev, ncc=ncc, nlc=nlc, nb=nb)
    full2 = lambda b, j: (0, 0)
    in_specs = [
        pl.BlockSpec((ch, 2 * w), lambda b, j: (blk(b, j), 0)),
        pl.BlockSpec((SUBLANES, w), lambda b, j: (jnp.maximum(blk(b, j) * r8 - 1, 0), 1)),
        pl.BlockSpec((SUBLANES, w), lambda b, j: (jnp.minimum(blk(b, j) * r8 + r8, t // SUBLANES - 1), 1)),
        pl.BlockSpec((4, w), full2),
        pl.BlockSpec((1, w), full2),
        pl.BlockSpec((LRU_BLOCKS, LRU_BLOCK, LRU_BLOCK), lambda b, j: (0, 0, 0)),
        pl.BlockSpec((1, w), full2),
        pl.BlockSpec((LRU_BLOCKS, LRU_BLOCK, LRU_BLOCK), lambda b, j: (0, 0, 0)),
        pl.BlockSpec((1, w), full2),
        pl.BlockSpec((1, w), full2),
    ]
    args = [u, u, u, cw, cb.reshape(1, w), wa.astype(BF16), ba.reshape(1, w), wi.astype(BF16),
            bi.reshape(1, w), lam.reshape(1, w)]
    scratch = [pltpu.VMEM((ch + 2 * SUBLANES, w), F32), pltpu.VMEM((ch, w), F32), pltpu.VMEM((ch, w), F32)]
    if rev:
        in_specs.append(pl.BlockSpec((ch, w), lambda b, j: (blk(b, j), 0)))
        args.append(hf)
        scratch.append(pltpu.VMEM((ch, w), F32))
        out_dtype = BF16
    else:
        out_dtype = F32
    scratch.append(pltpu.VMEM((1, w), F32))
    return pl.pallas_call(
        functools.partial(_lru_kernel, rev, ncc, nlc),
        grid=(nb, ncc + nlc),
        in_specs=in_specs,
        out_specs=pl.BlockSpec((ch, w), lambda b, j: (blk(b, j), 0)),
        out_shape=jax.ShapeDtypeStruct((t, w), out_dtype),
        scratch_shapes=scratch,
        compiler_params=_cparams("parallel", "arbitrary"),
        name="lru_bwd" if rev else "lru_fwd",
    )(*args)


def _cumsum_rows(v, rev):
    n = v.shape[0]
    rows = lax.broadcasted_iota(jnp.int32, v.shape, 0)
    s = 1
    while s < n:
        if rev:
            v = v + jnp.where(rows < n - s, pltpu.roll(v, n - s, 0), 0.0)
        else:
            v = v + jnp.where(rows >= s, pltpu.roll(v, s, 0), 0.0)
        s *= 2
    return v


def _hgrn_diag_att(q, k, bc, lo, width, rev):
    n = q.shape[0]
    lane = lax.broadcasted_iota(jnp.int32, (n, width), 1)
    row = lax.broadcasted_iota(jnp.int32, (n, width), 0)
    att = jnp.zeros((n, width), F32)
    for s in range(n):
        w = q * (k[s:s + 1] * jnp.exp2(bc - bc[s:s + 1]))
        att = jnp.where(lane == lo + s, jnp.sum(w, axis=-1, keepdims=True), att)
    keep = (row + lo <= lane) if rev else (row + lo >= lane)
    return jnp.where(keep, att, 0.0)


def _hgrn_chunk(qraw, fl, v, lbh, st, rev):
    cs = HG_CHUNK
    sb = HG_SUB
    q = _silu(qraw)
    f = lbh + (1.0 - lbh) * _sigmoid(fl)
    k = 1.0 - f
    bc = _cumsum_rows(jnp.log(f) * LOG2E, rev)
    blast = bc[0:1] if rev else bc[cs - 1:cs]
    o = _dot_nt((q * jnp.exp2(bc)).astype(BF16), st.astype(BF16))
    kdec = k * jnp.exp2(blast - bc)
    st_new = st * jnp.exp2(blast) + _dot_tn(v.astype(BF16), kdec.astype(BF16))
    row = lax.broadcasted_iota(jnp.int32, (cs, HG_DK), 0)
    att_rows = []
    for i in range(cs // sb):
        lo, hi = i * sb, (i + 1) * sb
        att_i = _hgrn_diag_att(q[lo:hi], k[lo:hi], bc[lo:hi], lo, cs, rev)
        if (hi < cs) if rev else (lo > 0):
            mref = bc[hi:hi + 1] if rev else bc[lo - 1:lo]
            past = (row >= hi) if rev else (row < lo)
            qhat = q[lo:hi] * jnp.exp2(bc[lo:hi] - mref)
            kp = jnp.where(past, k * jnp.exp2(jnp.where(past, mref - bc, 0.0)), 0.0)
            att_i = att_i + _dot_nt(qhat.astype(BF16), kp.astype(BF16))
        att_rows.append(att_i)
    att = jnp.concatenate(att_rows, axis=0)
    return o + _dot(att.astype(BF16), v.astype(BF16)), st_new


def _hgrn_kernel(rev, layer, *refs):
    if rev:
        q_ref, f_ref, v_ref, lbl_ref, og_ref, of_ref, gn_ref, o_ref, st_ref = refs
    else:
        q_ref, f_ref, v_ref, lbl_ref, o_ref, st_ref = refs

    @pl.when(pl.program_id(1) == 0)
    def _():
        st_ref[...] = jnp.zeros_like(st_ref)

    lg = lbl_ref[...]
    e = jnp.exp(lg - jnp.max(lg, axis=0, keepdims=True))
    p = e / jnp.sum(e, axis=0, keepdims=True)
    lb = jnp.zeros_like(p[0:1])
    for jl in range(1, layer + 1):
        lb = lb + p[jl:jl + 1]

    cs = HG_CHUNK
    n_chunks = q_ref.shape[0] // cs

    def body(ci, carry):
        cidx = (n_chunks - 1 - ci) if rev else ci
        r0 = pl.multiple_of(cidx * cs, cs)
        for h in range(HG_HEADS):
            cols = slice(h * HG_DK, (h + 1) * HG_DK)
            o, st_new = _hgrn_chunk(q_ref[pl.ds(r0, cs), cols], f_ref[pl.ds(r0, cs), cols],
                                    v_ref[pl.ds(r0, cs), cols], lb[:, cols], st_ref[h], rev)
            st_ref[h] = st_new
            if rev:
                o = o + of_ref[pl.ds(r0, cs), cols]
                y = o * lax.rsqrt(jnp.mean(o * o, axis=-1, keepdims=True) + EPS) * gn_ref[...]
                o_ref[pl.ds(r0, cs), cols] = (y * _silu(og_ref[pl.ds(r0, cs), cols])).astype(BF16)
            else:
                o_ref[pl.ds(r0, cs), cols] = o
        return carry

    lax.fori_loop(0, n_chunks, body, 0)


def _hgrn_pass(rev, layer, u, lb_logits, gnorm, of, nb, ncc, nlc):
    t = u.shape[0]
    ch = SEQ_CHUNK
    w = HG_WIDTH
    nl = lb_logits.shape[0]
    blk = functools.partial(_seq_blk, rev=rev, ncc=ncc, nlc=nlc, nb=nb)
    base = A_IN // w
    col = lambda cb: pl.BlockSpec((ch, w), lambda b, j: (blk(b, j), cb))
    in_specs = [col(base), col(base + 1 + (1 if rev else 0)), col(base + 3),
                pl.BlockSpec((nl, w), lambda b, j: (0, 0))]
    args = [u, u, u, lb_logits]
    if rev:
        in_specs += [col(base + 4), pl.BlockSpec((ch, w), lambda b, j: (blk(b, j), 0)),
                     pl.BlockSpec((1, HG_DK), lambda b, j: (0, 0))]
        args += [u, of, gnorm.reshape(1, HG_DK)]
    return pl.pallas_call(
        functools.partial(_hgrn_kernel, rev, layer),
        grid=(nb, ncc + nlc),
        in_specs=in_specs,
        out_specs=pl.BlockSpec((ch, w), lambda b, j: (blk(b, j), 0)),
        out_shape=jax.ShapeDtypeStruct((t, w), BF16 if rev else F32),
        scratch_shapes=[pltpu.VMEM((HG_HEADS, HG_DK, HG_DK), F32)],
        compiler_params=_cparams("parallel", "arbitrary"),
        name="hgrn_bwd" if rev else "hgrn_fwd",
    )(*args)


def _rope_tables(n):
    rows = n // GRID_W
    row = jnp.repeat(jnp.arange(rows, dtype=F32), GRID_W)
    colp = jnp.tile(jnp.arange(GRID_W, dtype=F32), rows)
    inv = ROPE_THETA ** (-jnp.arange(ROPE_FREQS, dtype=F32) / ROPE_FREQS)
    ar = row[:, None] * inv
    ac = colp[:, None] * inv
    cos = jnp.concatenate([jnp.cos(ar), jnp.cos(ar), jnp.cos(ac), jnp.cos(ac)], axis=-1)
    sin = jnp.concatenate([-jnp.sin(ar), jnp.sin(ar), -jnp.sin(ac), jnp.sin(ac)], axis=-1)
    cos = jnp.concatenate([cos, jnp.ones((SEQ_CHUNK, HEAD_DIM), F32)], axis=0)
    sin = jnp.concatenate([sin, jnp.zeros((SEQ_CHUNK, HEAD_DIM), F32)], axis=0)
    return cos, sin


def _qkv_kernel(qa_ref, qb_ref, kv_ref, cos_ref, sin_ref, qn_ref, kn_ref, q_out, k_out, vt_out):
    cos = cos_ref[...]
    sin = sin_ref[...]
    lane = lax.broadcasted_iota(jnp.int32, cos.shape, 1)
    first = (lane % (2 * ROPE_FREQS)) < ROPE_FREQS

    def norm_rope(v, g, scale):
        y = v * lax.rsqrt(jnp.mean(v * v, axis=-1, keepdims=True) + EPS) * g
        partner = jnp.where(first, pltpu.roll(y, HEAD_DIM - ROPE_FREQS, 1), pltpu.roll(y, ROPE_FREQS, 1))
        out = y * cos + partner * sin
        if scale is not None:
            out = out * scale
        return out.astype(BF16)

    half = N_Q_HEADS // 2
    for h in range(half):
        cols = slice(h * HEAD_DIM, (h + 1) * HEAD_DIM)
        q_out[:, cols] = norm_rope(qa_ref[:, cols], qn_ref[...], Q_SCALE)
        cols_b = slice((half + h) * HEAD_DIM, (half + h + 1) * HEAD_DIM)
        q_out[:, cols_b] = norm_rope(qb_ref[:, cols], qn_ref[...], Q_SCALE)
    for h in range(N_KV_HEADS):
        cols = slice(h * HEAD_DIM, (h + 1) * HEAD_DIM)
        k_out[:, cols] = norm_rope(kv_ref[:, cols], kn_ref[...], None)
        vt_out[cols, :] = kv_ref[:, KV_WIDTH + h * HEAD_DIM:KV_WIDTH + (h + 1) * HEAD_DIM].T.astype(BF16)


def _qkv(u, cos, sin, qn, kn, n_lat, nlc):
    t = u.shape[0]
    ch = SEQ_CHUNK
    cw = 512
    base = (A_IN + B_IN) // cw
    tab = lambda i: (jnp.where(i < n_lat // ch, i % nlc, nlc), 0)
    return pl.pallas_call(
        _qkv_kernel,
        grid=(t // ch,),
        in_specs=[
            pl.BlockSpec((ch, cw), lambda i: (i, base)),
            pl.BlockSpec((ch, cw), lambda i: (i, base + 1)),
            pl.BlockSpec((ch, cw), lambda i: (i, base + 2)),
            pl.BlockSpec((ch, HEAD_DIM), tab),
            pl.BlockSpec((ch, HEAD_DIM), tab),
            pl.BlockSpec((1, HEAD_DIM), lambda i: (0, 0)),
            pl.BlockSpec((1, HEAD_DIM), lambda i: (0, 0)),
        ],
        out_specs=[
            pl.BlockSpec((ch, ATT_WIDTH), lambda i: (i, 0)),
            pl.BlockSpec((ch, KV_WIDTH), lambda i: (i, 0)),
            pl.BlockSpec((KV_WIDTH, ch), lambda i: (0, i)),
        ],
        out_shape=[
            jax.ShapeDtypeStruct((t, ATT_WIDTH), BF16),
            jax.ShapeDtypeStruct((t, KV_WIDTH), BF16),
            jax.ShapeDtypeStruct((KV_WIDTH, t), BF16),
        ],
        compiler_params=_cparams("parallel"),
        name="qkv_prep",
    )(u, u, u, cos, sin, qn.reshape(1, HEAD_DIM), kn.reshape(1, HEAD_DIM))


def _row_groups(v, op):
    return op(v.reshape(v.shape[0] // SUBLANES, SUBLANES, v.shape[1]), axis=0)


def _attn_kernel(with_lat, *refs):
    if with_lat:
        q_ref, kc_ref, vtc_ref, kl_ref, vtl_ref, o_ref, acc_ref, s_ref = refs
        n_tiles = kl_ref.shape[0] // ATT_KEY_TILE
    else:
        q_ref, kc_ref, vtc_ref, o_ref, acc_ref = refs
        n_tiles = 0
    tq = q_ref.shape[0]
    tk = ATT_KEY_TILE

    for g in range(N_KV_HEADS):
        gcols = slice(g * HEAD_DIM, (g + 1) * HEAD_DIM)
        qs = jnp.concatenate(
            [q_ref[:, (g * Q_PER_KV + h) * HEAD_DIM:(g * Q_PER_KV + h + 1) * HEAD_DIM] for h in range(Q_PER_KV)],
            axis=0)

        def softmax_pv(s, vtt, carry):
            m_new = jnp.max(_row_groups(s, jnp.max), axis=0, keepdims=True)
            if carry is not None:
                m_new = jnp.maximum(carry[0], m_new)
            p = jnp.exp2(s - m_new)
            pv = _dot(vtt, p.astype(BF16))
            if carry is None:
                acc_ref[...] = pv
                return m_new, _row_groups(p, jnp.sum)
            alpha = jnp.exp2(carry[0] - m_new)
            acc_ref[...] = alpha * acc_ref[...] + pv
            return m_new, alpha * carry[1] + _row_groups(p, jnp.sum)

        def scores(t):
            r0 = t * tk if isinstance(t, int) else pl.multiple_of(t * tk, tk)
            return _dot_nt(kl_ref[pl.ds(r0, tk), gcols], qs)

        def consume(slot, t, carry):
            r0 = t * tk if isinstance(t, int) else pl.multiple_of(t * tk, tk)
            return softmax_pv(s_ref[slot], vtl_ref[gcols, pl.ds(r0, tk)], carry)

        if n_tiles:
            s_ref[0] = scores(0)
        carry = softmax_pv(_dot_nt(kc_ref[:, gcols], qs), vtc_ref[gcols, :], None)
        n_pairs = max(n_tiles // 2 - 1, 0)
        if n_pairs:
            def pair(j, carry):
                t0 = 2 * j
                s_ref[1] = scores(t0 + 1)
                carry = consume(0, t0, carry)
                s_ref[0] = scores(t0 + 2)
                return consume(1, t0 + 1, carry)
            carry = lax.fori_loop(0, n_pairs, pair, carry)
        for t in range(2 * n_pairs, n_tiles):
            if t + 1 < n_tiles:
                s_ref[(t + 1) % 2] = scores(t + 1)
            carry = consume(t % 2, t, carry)
        ot = acc_ref[...] * (1.0 / jnp.sum(carry[1], axis=0, keepdims=True))
        for h in range(Q_PER_KV):
            cols = slice((g * Q_PER_KV + h) * HEAD_DIM, (g * Q_PER_KV + h + 1) * HEAD_DIM)
            o_ref[:, cols] = ot[:, h * tq:(h + 1) * tq].T.astype(BF16)


def _attn(q, k, vt, nb, n, nc, latent):
    n_lat = nb * n
    if latent:
        tq = 128
        grid = (nb, n // tq)
        q_spec = pl.BlockSpec((tq, ATT_WIDTH), lambda b, i: (b * (n // tq) + i, 0))
        rows = n_lat
        n_keys = nc + n
    else:
        tq = nc
        grid = (nb, 1)
        q_spec = pl.BlockSpec((tq, ATT_WIDTH), lambda b, i: (n_lat // nc + b, 0))
        rows = nb * nc
        n_keys = nc
    in_specs = [q_spec,
                pl.BlockSpec((nc, KV_WIDTH), lambda b, i: (n_lat // nc + b, 0)),
                pl.BlockSpec((KV_WIDTH, nc), lambda b, i: (0, n_lat // nc + b))]
    args = [q, k, vt]
    if latent:
        in_specs += [pl.BlockSpec((n, KV_WIDTH), lambda b, i: (b, 0)),
                     pl.BlockSpec((KV_WIDTH, n), lambda b, i: (0, b))]
        args += [k, vt]
    out_map = (lambda b, i: (b * (n // tq) + i, 0)) if latent else (lambda b, i: (b, 0))
    m = Q_PER_KV * tq
    scratch = [pltpu.VMEM((HEAD_DIM, m), F32)]
    if latent:
        scratch.append(pltpu.VMEM((2, ATT_KEY_TILE, m), F32))
    return pl.pallas_call(
        functools.partial(_attn_kernel, latent),
        grid=grid,
        in_specs=in_specs,
        out_specs=pl.BlockSpec((tq, ATT_WIDTH), out_map),
        out_shape=jax.ShapeDtypeStruct((rows, ATT_WIDTH), BF16),
        scratch_shapes=scratch,
        compiler_params=_cparams("parallel", "parallel"),
        name="attn_lat" if latent else "attn_ctx",
    )(*args)


def _outproj_kernel(routed, *refs):
    if routed:
        ra_ref, hg_ref, at_ref, x_ref, mod_ref, w_ref, g_ref, rt_ref, xo_ref, h_ref, lg_ref = refs
    else:
        ra_ref, hg_ref, at_ref, x_ref, mod_ref, w_ref, g_ref, xo_ref, h_ref = refs
    w1 = ra_ref.shape[1]
    w2 = w1 + hg_ref.shape[1]
    mix = (_dot(ra_ref[...], w_ref[0:w1, :]) + _dot(hg_ref[...], w_ref[w1:w2, :])
           + _dot(at_ref[...], w_ref[w2:, :]))
    m = mod_ref[0]
    x = x_ref[...] + m[2:3] * mix
    xo_ref[...] = x
    y = x * lax.rsqrt(jnp.mean(x * x, axis=-1, keepdims=True) + EPS) * g_ref[...]
    h = y * (1.0 + m[4:5]) + m[3:4]
    if routed:
        h_ref[...] = h
        h_hi = h.astype(BF16)
        h_lo = (h - h_hi.astype(F32)).astype(BF16)
        lg = _dot(h_hi, rt_ref[...]) + _dot(h_lo, rt_ref[...])
        lg_ref[...] = lg[:, 0:LANES] + lg[:, LANES:2 * LANES]
    else:
        h_ref[...] = h.astype(BF16)


def _outproj(ra, hg, at, xs, mod, w, g, router, n_tok, n_lat, seq, nb):
    d = xs.shape[1]
    tm = 256
    routed = router is not None
    row = functools.partial(_mod_row, tile=tm, n_lat=n_lat, seq=seq, nb=nb)
    tok = lambda width: pl.BlockSpec((tm, width), lambda i: (i, 0))
    in_specs = [tok(ra.shape[1]), tok(hg.shape[1]), tok(at.shape[1]), tok(d),
                pl.BlockSpec((1, N_MOD, d), lambda i: (row(i), 0, 0)),
                pl.BlockSpec(w.shape, lambda i: (0, 0)),
                pl.BlockSpec((1, d), lambda i: (0, 0))]
    args = [ra, hg, at, xs, mod, w, g.reshape(1, d)]
    out_specs = [tok(d), tok(d)]
    out_shape = [jax.ShapeDtypeStruct((n_tok, d), F32), jax.ShapeDtypeStruct((n_tok, d), F32 if routed else BF16)]
    if routed:
        in_specs.append(pl.BlockSpec((d, 2 * LANES), lambda i: (0, 0)))
        args.append(router)
        out_specs.append(tok(LANES))
        out_shape.append(jax.ShapeDtypeStruct((n_tok, LANES), F32))
    return pl.pallas_call(
        functools.partial(_outproj_kernel, routed),
        grid=(n_tok // tm,),
        in_specs=in_specs,
        out_specs=out_specs,
        out_shape=out_shape,
        compiler_params=_cparams("parallel"),
        name="outproj",
    )(*args)


def _ffn_kernel(h_ref, x_ref, mod_ref, w1_ref, w3_ref, w2_ref, o_ref, acc_ref):
    k = pl.program_id(1)

    @pl.when(k == 0)
    def _():
        acc_ref[...] = jnp.zeros_like(acc_ref)

    h = h_ref[...]
    a = _dot(h, w1_ref[...])
    z = _silu(a) * _dot(h, w3_ref[...])
    acc_ref[...] += _dot(z.astype(BF16), w2_ref[...])

    @pl.when(k == pl.num_programs(1) - 1)
    def _():
        o_ref[...] = x_ref[...] + mod_ref[0][5:6] * acc_ref[...]


def _ffn(h, xs, mod, w1, w3, w2, n_lat, seq, nb):
    t, d = h.shape
    f = w1.shape[1]
    tm = TOKEN_TILE
    tf = 512
    row = functools.partial(_mod_row, tile=tm, n_lat=n_lat, seq=seq, nb=nb)
    return pl.pallas_call(
        _ffn_kernel,
        grid=(t // tm, f // tf),
        in_specs=[
            pl.BlockSpec((tm, d), lambda i, k: (i, 0)),
            pl.BlockSpec((tm, d), lambda i, k: (i, 0)),
            pl.BlockSpec((1, N_MOD, d), lambda i, k: (row(i), 0, 0)),
            pl.BlockSpec((d, tf), lambda i, k: (0, k)),
            pl.BlockSpec((d, tf), lambda i, k: (0, k)),
            pl.BlockSpec((tf, d), lambda i, k: (k, 0)),
        ],
        out_specs=pl.BlockSpec((tm, d), lambda i, k: (i, 0)),
        out_shape=jax.ShapeDtypeStruct((t, d), F32),
        scratch_shapes=[pltpu.VMEM((tm, d), F32)],
        compiler_params=_cparams("parallel", "arbitrary"),
        name="ffn_dense",
    )(h, xs, mod, w1, w3, w2)


def _router_kernel(lg_ref, meta_ref, wts_ref, cnt_ref, run_ref):
    @pl.when(pl.program_id(0) == 0)
    def _():
        run_ref[...] = jnp.zeros_like(run_ref)

    lg = lg_ref[...]
    tm = lg.shape[0]
    lane = lax.broadcasted_iota(jnp.int32, lg.shape, 1)
    lane_f = lane.astype(F32)
    v = jnp.where(lane < N_EXPERTS, lg, -jnp.inf)
    m1 = jnp.max(v, axis=-1, keepdims=True)
    i1 = jnp.min(jnp.where(v == m1, lane_f, float(LANES)), axis=-1, keepdims=True)
    v2 = jnp.where(lane_f == i1, -jnp.inf, v)
    m2 = jnp.max(v2, axis=-1, keepdims=True)
    i2 = jnp.min(jnp.where(v2 == m2, lane_f, float(LANES)), axis=-1, keepdims=True)
    e = jnp.exp(m2 - m1)
    wt1 = 1.0 / (1.0 + e)
    wt2 = e / (1.0 + e)
    hit1 = lane_f == i1
    hit2 = lane_f == i2
    assign = jnp.where(hit1 | hit2, 1.0, 0.0)
    r = lax.broadcasted_iota(jnp.int32, (tm, tm), 0)
    c = lax.broadcasted_iota(jnp.int32, (tm, tm), 1)
    tri = jnp.where(r > c, 1.0, 0.0).astype(BF16)
    rank = _dot(tri, assign.astype(BF16)) + run_ref[0:1, :]
    r1 = jnp.sum(jnp.where(hit1, rank, 0.0), axis=-1, keepdims=True)
    r2 = jnp.sum(jnp.where(hit2, rank, 0.0), axis=-1, keepdims=True)
    run_ref[...] = run_ref[...] + jnp.sum(assign, axis=0, keepdims=True)
    meta = jnp.where(lane == 0, i1, jnp.where(lane == 1, i2, jnp.where(lane == 2, r1, jnp.where(lane == 3, r2, 0.0))))
    meta_ref[...] = meta.astype(jnp.int32)
    wts_ref[...] = jnp.where(lane == 0, wt1, jnp.where(lane == 1, wt2, 0.0))
    cnt_ref[...] = run_ref[...]


def _router(logits):
    t = logits.shape[0]
    tm = TOKEN_TILE
    tok = pl.BlockSpec((tm, LANES), lambda i: (i, 0))
    return pl.pallas_call(
        _router_kernel,
        grid=(t // tm,),
        in_specs=[tok],
        out_specs=[tok, tok, pl.BlockSpec((SUBLANES, LANES), lambda i: (0, 0))],
        out_shape=[jax.ShapeDtypeStruct((t, LANES), jnp.int32), jax.ShapeDtypeStruct((t, LANES), F32),
                   jax.ShapeDtypeStruct((SUBLANES, LANES), F32)],
        scratch_shapes=[pltpu.VMEM((SUBLANES, LANES), F32)],
        compiler_params=_cparams("arbitrary"),
        name="router",
    )(logits)


def _row_copy(src_ref, src_row, dst_ref, dst_row, sem):
    return pltpu.make_async_copy(src_ref.at[pl.ds(src_row, 1)], dst_ref.at[pl.ds(dst_row, 1)], sem)


def _dispatch_kernel(pos_ref, h_ref, xs_in_ref, xs_ref, pos_smem, sem_p, sem_d):
    del xs_in_ref
    tm = h_ref.shape[0]
    cp = pltpu.make_async_copy(pos_ref.at[0, 0], pos_smem, sem_p)
    cp.start()
    cp.wait()

    def issue(t, carry):
        _row_copy(h_ref, t, xs_ref, pos_smem[t], sem_d).start()
        _row_copy(h_ref, t, xs_ref, pos_smem[tm + t], sem_d).start()
        return carry

    lax.fori_loop(0, tm, issue, 0)

    def drain(t, carry):
        _row_copy(h_ref, 0, xs_ref, 0, sem_d).wait()
        _row_copy(h_ref, 0, xs_ref, 0, sem_d).wait()
        return carry

    lax.fori_loop(0, tm, drain, 0)


def _dispatch(pos, h, xs0):
    t, d = h.shape
    tm = 256
    nt = t // tm
    return pl.pallas_call(
        _dispatch_kernel,
        grid=(nt,),
        in_specs=[
            pl.BlockSpec((1, 1, 2 * tm), lambda i: (i, 0, 0)),
            pl.BlockSpec((tm, d), lambda i: (i, 0)),
            pl.BlockSpec(memory_space=pl.ANY),
        ],
        out_specs=pl.BlockSpec(memory_space=pl.ANY),
        out_shape=jax.ShapeDtypeStruct(xs0.shape, xs0.dtype),
        scratch_shapes=[pltpu.SMEM((2 * tm,), jnp.int32), pltpu.SemaphoreType.DMA(()), pltpu.SemaphoreType.DMA(())],
        input_output_aliases={2: 0},
        compiler_params=_cparams("arbitrary"),
        name="moe_dispatch",
    )(pos, h, xs0)


def _expert_kernel(te_ref, tv_ref, x_ref, w1_ref, w3_ref, w2_ref, y_ref, xb_ref, acc_ref):
    del te_ref
    i = pl.program_id(0)
    k = pl.program_id(1)

    @pl.when(tv_ref[i] > 0)
    def _():
        @pl.when(k == 0)
        def _():
            xb_ref[...] = x_ref[...].astype(BF16)
            acc_ref[...] = jnp.zeros_like(acc_ref)

        xb = xb_ref[...]
        a = _dot(xb, w1_ref[...])
        z = _silu(a) * _dot(xb, w3_ref[...])
        acc_ref[...] += _dot(z.astype(BF16), w2_ref[...])

        @pl.when(k == pl.num_programs(1) - 1)
        def _():
            y_ref[...] = acc_ref[...]

    @pl.when((tv_ref[i] == 0) & (k == 0))
    def _():
        y_ref[...] = jnp.zeros_like(y_ref)


def _experts(tile_expert, tile_valid, tile_row, xs, w1, w3, w2):
    rows, d = xs.shape
    f = w1.shape[2]
    tm = EXPERT_TILE
    tf = 512
    grid_spec = pltpu.PrefetchScalarGridSpec(
        num_scalar_prefetch=3,
        grid=(rows // tm, f // tf),
        in_specs=[
            pl.BlockSpec((tm, d), lambda i, k, te, tv, tr: (tr[i], 0)),
            pl.BlockSpec((None, d, tf), lambda i, k, te, tv, tr: (te[i], 0, jnp.where(tv[i] > 0, k, f // tf - 1))),
            pl.BlockSpec((None, d, tf), lambda i, k, te, tv, tr: (te[i], 0, jnp.where(tv[i] > 0, k, f // tf - 1))),
            pl.BlockSpec((None, tf, d), lambda i, k, te, tv, tr: (te[i], jnp.where(tv[i] > 0, k, f // tf - 1), 0)),
        ],
        out_specs=pl.BlockSpec((tm, d), lambda i, k, te, tv, tr: (i, 0)),
        scratch_shapes=[pltpu.VMEM((tm, d), BF16), pltpu.VMEM((tm, d), F32)],
    )

    def body(te_ref, tv_ref, tr_ref, *rest):
        del tr_ref
        _expert_kernel(te_ref, tv_ref, *rest)

    return pl.pallas_call(
        body,
        grid_spec=grid_spec,
        out_shape=jax.ShapeDtypeStruct((rows, d), F32),
        compiler_params=_cparams("arbitrary", "arbitrary"),
        name="moe_experts",
    )(tile_expert, tile_valid, tile_row, xs, w1, w3, w2)


def _combine_kernel(pos_ref, wts_ref, x_ref, mod_ref, ys_ref, o_ref, pos_smem, y1_ref, y2_ref, sem_p, sem_g):
    tm = x_ref.shape[0]
    cp = pltpu.make_async_copy(pos_ref.at[0, 0], pos_smem, sem_p)
    cp.start()
    cp.wait()

    def issue(t, carry):
        _row_copy(ys_ref, pos_smem[t], y1_ref, t, sem_g).start()
        _row_copy(ys_ref, pos_smem[tm + t], y2_ref, t, sem_g).start()
        return carry

    lax.fori_loop(0, tm, issue, 0)

    def drain(t, carry):
        _row_copy(ys_ref, 0, y1_ref, 0, sem_g).wait()
        _row_copy(ys_ref, 0, y2_ref, 0, sem_g).wait()
        return carry

    lax.fori_loop(0, tm, drain, 0)
    wts = wts_ref[...]
    f = wts[:, 0:1] * y1_ref[...] + wts[:, 1:2] * y2_ref[...]
    o_ref[...] = x_ref[...] + mod_ref[0][5:6] * f


def _combine(pos, wts, xs, mod, ys, n_lat, seq, nb):
    t, d = xs.shape
    tm = 256
    row = functools.partial(_mod_row, tile=tm, n_lat=n_lat, seq=seq, nb=nb)
    return pl.pallas_call(
        _combine_kernel,
        grid=(t // tm,),
        in_specs=[
            pl.BlockSpec((1, 1, 2 * tm), lambda i: (i, 0, 0)),
            pl.BlockSpec((tm, LANES), lambda i: (i, 0)),
            pl.BlockSpec((tm, d), lambda i: (i, 0)),
            pl.BlockSpec((1, N_MOD, d), lambda i: (row(i), 0, 0)),
            pl.BlockSpec(memory_space=pl.ANY),
        ],
        out_specs=pl.BlockSpec((tm, d), lambda i: (i, 0)),
        out_shape=jax.ShapeDtypeStruct((t, d), F32),
        scratch_shapes=[pltpu.SMEM((2 * tm,), jnp.int32), pltpu.VMEM((tm, d), F32), pltpu.VMEM((tm, d), F32),
                        pltpu.SemaphoreType.DMA(()), pltpu.SemaphoreType.DMA(())],
        compiler_params=_cparams("arbitrary"),
        name="moe_combine",
    )(pos, wts, xs, mod, ys)


def _moe(h, logits, xs, mod, w1, w3, w2, n_lat, seq, nb):
    t, d = h.shape
    meta, wts, counts = _router(logits)
    te_rows = EXPERT_TILE
    cnt = counts[0, :N_EXPERTS].astype(jnp.int32)
    padded = ((cnt + te_rows - 1) // te_rows) * te_rows
    ends = jnp.cumsum(padded)
    starts = ends - padded
    pos1 = jnp.take(starts, meta[:, 0]) + meta[:, 2]
    pos2 = jnp.take(starts, meta[:, 1]) + meta[:, 3]
    tmd = 256
    pos = jnp.concatenate([pos1.reshape(t // tmd, 1, tmd), pos2.reshape(t // tmd, 1, tmd)], axis=-1)
    n_tiles = (2 * t) // te_rows + N_EXPERTS
    tile_start = jnp.arange(n_tiles, dtype=jnp.int32) * te_rows
    n_valid = ends[-1] // te_rows
    tile_valid = (tile_start < ends[-1]).astype(jnp.int32)
    tile_row = jnp.minimum(jnp.arange(n_tiles, dtype=jnp.int32), n_valid - 1)
    tile_expert = jnp.sum((tile_row[:, None] * te_rows >= ends[None, :]).astype(jnp.int32), axis=1)
    xs0 = jnp.zeros((n_tiles * te_rows, d), F32)
    x_sorted = _dispatch(pos, h, xs0)
    y_sorted = _experts(tile_expert, tile_valid, tile_row, x_sorted, w1, w3, w2)
    return _combine(pos, wts, xs, mod, y_sorted, n_lat, seq, nb)


def kernel(x, c, ctx, c_ctx, w_mod, b_mod, norm1, norm2, w_in, w_out, conv_w, conv_b, lru_wa, lru_ba, lru_wi, lru_bi, lru_lambda, hgrn_lb_logits, hgrn_gnorm, q_norm, k_norm, ffn_w1, ffn_w3, ffn_w2, router, moe_w1, moe_w3, moe_w2):
    nb, n, d = x.shape
    nc = ctx.shape[1]
    depth = w_mod.shape[0]
    assert n % SEQ_CHUNK == 0 and nc % SEQ_CHUNK == 0 and n % TOKEN_TILE == 0
    assert (nb * nc) % TOKEN_TILE == 0 and n % GRID_W == 0
    n_lat = nb * n
    nlc = n // SEQ_CHUNK
    ncc = nc // SEQ_CHUNK

    xs = jnp.concatenate([x.reshape(n_lat, d), ctx.reshape(nb * nc, d)], axis=0)
    mod_rows = 2 * SUBLANES * ((nb + 1 + 2 * SUBLANES - 1) // (2 * SUBLANES))
    cc = jnp.zeros((mod_rows, d), F32).at[:nb].set(c).at[nb].set(c_ctx)
    mod_all = _mod_table(cc, w_mod, b_mod).reshape(depth, mod_rows, N_MOD, d)
    cos, sin = _rope_tables(n)
    router_f = jnp.pad(router, ((0, 0), (0, 0), (0, LANES - router.shape[-1])))
    router_hi = router_f.astype(BF16)
    router_lo = (router_f - router_hi.astype(F32)).astype(BF16)
    router_p = jnp.concatenate([router_hi, router_lo], axis=-1)

    for l in range(depth):
        need_ctx = l < depth - 1
        mod = mod_all[l]
        u = _inproj(xs, mod, norm1[l], w_in[l].astype(BF16), n_lat, n, nb)

        lru_args = lambda dd: (conv_w[l], conv_b[l], lru_wa[l, dd], lru_ba[l, dd], lru_wi[l, dd], lru_bi[l, dd],
                               lru_lambda[l, dd])
        hf = _lru_pass(False, u, *lru_args(0), None, nb, ncc, nlc)
        ra = _lru_pass(True, u, *lru_args(1), hf, nb, ncc, nlc)

        of = _hgrn_pass(False, l, u, hgrn_lb_logits[0], hgrn_gnorm[l], None, nb, ncc, nlc)
        hg = _hgrn_pass(True, l, u, hgrn_lb_logits[1], hgrn_gnorm[l], of, nb, ncc, nlc)

        q, k, v = _qkv(u, cos, sin, q_norm[l], k_norm[l], n_lat, nlc)
        at = _attn(q, k, v, nb, n, nc, True)
        if need_ctx:
            at = jnp.concatenate([at, _attn(q, k, v, nb, n, nc, False)], axis=0)

        n_tok = xs.shape[0] if need_ctx else n_lat
        j = l // 2
        routed = l % 2 == 1
        outs = _outproj(ra, hg, at, xs, mod, w_out[l].astype(BF16), norm2[l], router_p[j] if routed else None,
                        n_tok, n_lat, n, nb)
        if routed:
            xn, h2, logits = outs
            xs = _moe(h2, logits, xn, mod, moe_w1[j].astype(BF16), moe_w3[j].astype(BF16),
                      moe_w2[j].astype(BF16), n_lat, n, nb)
        else:
            xn, h2 = outs
            xs = _ffn(h2, xn, mod, ffn_w1[j].astype(BF16), ffn_w3[j].astype(BF16), ffn_w2[j].astype(BF16),
                      n_lat, n, nb)
    return xs[:n_lat].reshape(nb, n, d)
```

```python
import functools

import jax
import jax.numpy as jnp
from jax import lax
from jax.experimental import pallas as pl
from jax.experimental.pallas import tpu as pltpu

F32 = jnp.float32
BF16 = jnp.bfloat16

EPS = 1e-6
N_MOD = 6
GRID_W = 64
LRU_WIDTH = 512
LRU_BLOCKS = 4
LRU_BLOCK = LRU_WIDTH // LRU_BLOCKS
LRU_C = 8.0
HG_HEADS = 4
HG_DK = 128
HG_WIDTH = HG_HEADS * HG_DK
HG_CHUNK = 64
HG_SUB = 8
HEAD_DIM = 128
N_Q_HEADS = 8
N_KV_HEADS = 2
Q_PER_KV = N_Q_HEADS // N_KV_HEADS
ATT_WIDTH = N_Q_HEADS * HEAD_DIM
KV_WIDTH = N_KV_HEADS * HEAD_DIM
ROPE_THETA = 10000.0
ROPE_FREQS = HEAD_DIM // 4
ATTN_SCALE = HEAD_DIM ** -0.5
LOG2E = 1.4426950408889634
Q_SCALE = ATTN_SCALE * LOG2E
A_IN = 2 * LRU_WIDTH
B_IN = 5 * HG_WIDTH
C_IN = ATT_WIDTH + 2 * KV_WIDTH
IN_WIDTH = A_IN + B_IN + C_IN
N_EXPERTS = 8

LANES = 128
SUBLANES = 8
SEQ_CHUNK = 256
TOKEN_TILE = 512
EXPERT_TILE = 512
ATT_KEY_TILE = 1024
ATT_QUERY_TILE = 128
CAST_BLOCK_ELEMS = 1024 * 1024
VMEM_LIMIT = 56 * 1024 * 1024
NEG_BIG = -1e30


def _cparams(*sem):
    return pltpu.CompilerParams(dimension_semantics=sem, vmem_limit_bytes=VMEM_LIMIT)


def _sigmoid(v):
    return 1.0 / (1.0 + jnp.exp(-v))


def _silu(v):
    return v * _sigmoid(v)


def _dot(a, b):
    return jnp.dot(a, b, preferred_element_type=F32)


def _dot_nt(a, b):
    return lax.dot_general(a, b, (((1,), (1,)), ((), ())), preferred_element_type=F32)


def _dot_tn(a, b):
    return lax.dot_general(a, b, (((0,), (0,)), ((), ())), preferred_element_type=F32)


def _mod_row(i, tile, n_lat, seq, nb):
    return jnp.where(i < n_lat // tile, (i * tile) // seq, nb)


def _cast_kernel(w_ref, o_ref):
    o_ref[...] = w_ref[...].astype(BF16)


def _to_bf16(w, index):
    inner = w.shape[1:]
    cols = inner[-1]
    rows = 1
    for s in inner[:-1]:
        rows *= s
    tr = 2 * SUBLANES
    while tr * 2 * cols <= CAST_BLOCK_ELEMS and rows % (tr * 2) == 0:
        tr *= 2
    assert rows % tr == 0
    steps = rows // tr
    out = pl.pallas_call(
        _cast_kernel,
        grid=(steps,),
        in_specs=[pl.BlockSpec((tr, cols), lambda i: (index * steps + i, 0))],
        out_specs=pl.BlockSpec((tr, cols), lambda i: (i, 0)),
        out_shape=jax.ShapeDtypeStruct((rows, cols), BF16),
        compiler_params=_cparams("parallel"),
        name="to_bf16",
    )(w.reshape(w.shape[0] * rows, cols))
    return out.reshape(inner)


def _mod_kernel(c_ref, w_ref, b_ref, o_ref):
    s = _silu(c_ref[...]).astype(BF16)
    o_ref[0] = _dot(s, w_ref[0].astype(BF16)) + b_ref[0]


def _mod_table(cc, w_mod, b_mod):
    nl, d, md = w_mod.shape
    rows = cc.shape[0]
    tn = 1024 if md % 1024 == 0 else md
    return pl.pallas_call(
        _mod_kernel,
        grid=(nl, md // tn),
        in_specs=[
            pl.BlockSpec((rows, d), lambda l, j: (0, 0)),
            pl.BlockSpec((1, d, tn), lambda l, j: (l, 0, j)),
            pl.BlockSpec((1, 1, tn), lambda l, j: (l, 0, j)),
        ],
        out_specs=pl.BlockSpec((1, rows, tn), lambda l, j: (l, 0, j)),
        out_shape=jax.ShapeDtypeStruct((nl, rows, md), F32),
        compiler_params=_cparams("parallel", "parallel"),
        name="mod_table",
    )(cc, w_mod, b_mod.reshape(nl, 1, md))


def _stream_specs(tm, d, lat_tiles, ctx_base):
    lat = pl.BlockSpec((tm, d), lambda i, *_: (jnp.minimum(i, lat_tiles - 1), 0))
    ctx = pl.BlockSpec((tm, d), lambda i, *_: (ctx_base + jnp.maximum(i - lat_tiles, 0), 0))
    return lat, ctx


def _inproj_kernel(lat_tiles, xl_ref, xc_ref, mod_ref, g_ref, w_ref, o_ref, h_ref):
    i = pl.program_id(0)
    first = pl.program_id(1) == 0

    def norm_mod(x_ref):
        x = x_ref[...]
        y = x * lax.rsqrt(jnp.mean(x * x, axis=-1, keepdims=True) + EPS) * g_ref[...]
        m = mod_ref[0]
        h_ref[...] = (y * (1.0 + m[1:2]) + m[0:1]).astype(BF16)

    @pl.when(first & (i < lat_tiles))
    def _():
        norm_mod(xl_ref)

    @pl.when(first & (i >= lat_tiles))
    def _():
        norm_mod(xc_ref)

    o_ref[...] = _dot(h_ref[...], w_ref[...])


def _inproj(x_lat, x_ctx, ctx_base, n_tok, mod, g, w, n_lat, seq, nb):
    d = x_lat.shape[1]
    width = w.shape[1]
    tm = TOKEN_TILE
    tn = 1024
    row = functools.partial(_mod_row, tile=tm, n_lat=n_lat, seq=seq, nb=nb)
    lat_spec, ctx_spec = _stream_specs(tm, d, n_lat // tm, ctx_base // tm)
    return pl.pallas_call(
        functools.partial(_inproj_kernel, n_lat // tm),
        grid=(n_tok // tm, width // tn),
        in_specs=[
            lat_spec,
            ctx_spec,
            pl.BlockSpec((1, N_MOD, d), lambda i, j: (row(i), 0, 0)),
            pl.BlockSpec((1, d), lambda i, j: (0, 0)),
            pl.BlockSpec((d, tn), lambda i, j: (0, j)),
        ],
        out_specs=pl.BlockSpec((tm, tn), lambda i, j: (i, j)),
        out_shape=jax.ShapeDtypeStruct((n_tok, width), F32),
        scratch_shapes=[pltpu.VMEM((tm, d), BF16)],
        compiler_params=_cparams("parallel", "arbitrary"),
        name="inproj",
    )(x_lat, x_ctx, mod, g.reshape(1, d), w)


def _seq_pos(j, rev, ncc, nlc):
    is_ctx = j < ncc
    jl = j - ncc
    cc = (ncc - 1 - j) if rev else j
    cl = (nlc - 1 - jl) if rev else jl
    return is_ctx, cc, cl


def _seq_blk(b, j, rev, ncc, nlc, nb):
    is_ctx, cc, cl = _seq_pos(j, rev, ncc, nlc)
    return jnp.where(is_ctx, nb * nlc + b * ncc + cc, b * nlc + cl)


def _scan_rows(a_ref, b_ref, h_ref, hc_ref, rev):
    n_groups = a_ref.shape[0] // SUBLANES
    width = a_ref.shape[1]
    rows = lax.broadcasted_iota(jnp.int32, (SUBLANES, width), 0)

    def body(g, hc):
        gi = (n_groups - 1 - g) if rev else g
        r0 = pl.multiple_of(gi * SUBLANES, SUBLANES)
        a = a_ref[pl.ds(r0, SUBLANES), :]
        b = b_ref[pl.ds(r0, SUBLANES), :]
        for s in (1, 2, 4):
            if rev:
                a_s = pltpu.roll(a, SUBLANES - s, 0)
                b_s = pltpu.roll(b, SUBLANES - s, 0)
                keep = rows < SUBLANES - s
            else:
                a_s = pltpu.roll(a, s, 0)
                b_s = pltpu.roll(b, s, 0)
                keep = rows >= s
            b = a * jnp.where(keep, b_s, 0.0) + b
            a = a * jnp.where(keep, a_s, 1.0)
        h = b + a * hc
        h_ref[pl.ds(r0, SUBLANES), :] = h
        return h[0:1] if rev else h[SUBLANES - 1:SUBLANES]

    hc_ref[...] = lax.fori_loop(0, n_groups, body, hc_ref[...])


def _lru_kernel(rev, ncc, nlc, *refs):
    if rev:
        (u_ref, prev_ref, next_ref, cw_ref, cb_ref, wa_ref, ba_ref, wi_ref, bi_ref, lam_ref,
         hf_ref, o_ref, xe_ref, a_ref, b_ref, h_ref, hc_ref) = refs
    else:
        (u_ref, prev_ref, next_ref, cw_ref, cb_ref, wa_ref, ba_ref, wi_ref, bi_ref, lam_ref,
         o_ref, xe_ref, a_ref, b_ref, hc_ref) = refs
        h_ref = o_ref
    j = pl.program_id(1)
    is_ctx, cc, cl = _seq_pos(j, rev, ncc, nlc)
    c = jnp.where(is_ctx, cc, cl)
    nseq = jnp.where(is_ctx, ncc, nlc)

    @pl.when(j == 0)
    def _():
        hc_ref[...] = jnp.zeros_like(hc_ref)

    ch = SEQ_CHUNK
    w = LRU_WIDTH
    xe_ref[SUBLANES:SUBLANES + ch, :] = u_ref[:, w:2 * w]
    xe_ref[0:SUBLANES, :] = jnp.where(c > 0, prev_ref[...], 0.0)
    xe_ref[SUBLANES + ch:2 * SUBLANES + ch, :] = jnp.where(c < nseq - 1, next_ref[...], 0.0)
    cw = cw_ref[...]
    xc = cb_ref[...]
    for tap in range(4):
        off = SUBLANES - 2 + tap
        xc = xc + cw[tap:tap + 1] * xe_ref[off:off + ch, :]
    xb = xc.astype(BF16)
    ra = jnp.concatenate(
        [_dot(xb[:, n * LRU_BLOCK:(n + 1) * LRU_BLOCK], wa_ref[n]) for n in range(LRU_BLOCKS)], axis=-1)
    ri = jnp.concatenate(
        [_dot(xb[:, n * LRU_BLOCK:(n + 1) * LRU_BLOCK], wi_ref[n]) for n in range(LRU_BLOCKS)], axis=-1)
    r = _sigmoid(ra + ba_ref[...])
    gate_i = _sigmoid(ri + bi_ref[...])
    neg_lam = -lam_ref[...]
    softplus = jnp.maximum(neg_lam, 0.0) + jnp.log1p(jnp.exp(-jnp.abs(neg_lam)))
    log_a = -LRU_C * r * softplus
    a = jnp.exp(log_a)
    a_ref[...] = a
    b_ref[...] = jnp.sqrt(-jnp.tanh(log_a) * (a * a + 1.0)) * (gate_i * xc)
    _scan_rows(a_ref, b_ref, h_ref, hc_ref, rev)
    if rev:
        y = u_ref[:, 0:w]
        gelu = 0.5 * y * (1.0 + jnp.tanh(0.7978845608028654 * (y + 0.044715 * (y * y * y))))
        o_ref[...] = (gelu * (hf_ref[...] + h_ref[...])).astype(BF16)


def _lru_pass(rev, u, cw, cb, wa, ba, wi, bi, lam, hf, nb, ncc, nlc):
    t = u.shape[0]
    ch = SEQ_CHUNK
    w = LRU_WIDTH
    r8 = ch // SUBLANES
    blk = functools.partial(_seq_blk, rev=rev, ncc=ncc, nlc=nlc, nb=nb)
    full2 = lambda b, j: (0, 0)
    in_specs = [
        pl.BlockSpec((ch, 2 * w), lambda b, j: (blk(b, j), 0)),
        pl.BlockSpec((SUBLANES, w), lambda b, j: (jnp.maximum(blk(b, j) * r8 - 1, 0), 1)),
        pl.BlockSpec((SUBLANES, w), lambda b, j: (jnp.minimum(blk(b, j) * r8 + r8, t // SUBLANES - 1), 1)),
        pl.BlockSpec((4, w), full2),
        pl.BlockSpec((1, w), full2),
        pl.BlockSpec((LRU_BLOCKS, LRU_BLOCK, LRU_BLOCK), lambda b, j: (0, 0, 0)),
        pl.BlockSpec((1, w), full2),
        pl.BlockSpec((LRU_BLOCKS, LRU_BLOCK, LRU_BLOCK), lambda b, j: (0, 0, 0)),
        pl.BlockSpec((1, w), full2),
        pl.BlockSpec((1, w), full2),
    ]
    args = [u, u, u, cw, cb.reshape(1, w), wa.astype(BF16), ba.reshape(1, w), wi.astype(BF16),
            bi.reshape(1, w), lam.reshape(1, w)]
    scratch = [pltpu.VMEM((ch + 2 * SUBLANES, w), F32), pltpu.VMEM((ch, w), F32), pltpu.VMEM((ch, w), F32)]
    if rev:
        in_specs.append(pl.BlockSpec((ch, w), lambda b, j: (blk(b, j), 0)))
        args.append(hf)
        scratch.append(pltpu.VMEM((ch, w), F32))
        out_dtype = BF16
    else:
        out_dtype = F32
    scratch.append(pltpu.VMEM((1, w), F32))
    return pl.pallas_call(
        functools.partial(_lru_kernel, rev, ncc, nlc),
        grid=(nb, ncc + nlc),
        in_specs=in_specs,
        out_specs=pl.BlockSpec((ch, w), lambda b, j: (blk(b, j), 0)),
        out_shape=jax.ShapeDtypeStruct((t, w), out_dtype),
        scratch_shapes=scratch,
        compiler_params=_cparams("parallel", "arbitrary"),
        name="lru_bwd" if rev else "lru_fwd",
    )(*args)


def _cumsum_rows(v, rev):
    n = v.shape[0]
    rows = lax.broadcasted_iota(jnp.int32, v.shape, 0)
    s = 1
    while s < n:
        if rev:
            v = v + jnp.where(rows < n - s, pltpu.roll(v, n - s, 0), 0.0)
        else:
            v = v + jnp.where(rows >= s, pltpu.roll(v, s, 0), 0.0)
        s *= 2
    return v


def _hgrn_diag_att(q, k, bc, lo, width, rev):
    n = q.shape[0]
    lane = lax.broadcasted_iota(jnp.int32, (n, width), 1)
    row = lax.broadcasted_iota(jnp.int32, (n, width), 0)
    att = jnp.zeros((n, width), F32)
    for s in range(n):
        w = q * (k[s:s + 1] * jnp.exp2(bc - bc[s:s + 1]))
        att = jnp.where(lane == lo + s, jnp.sum(w, axis=-1, keepdims=True), att)
    keep = (row + lo <= lane) if rev else (row + lo >= lane)
    return jnp.where(keep, att, 0.0)


def _hgrn_chunk(qraw, fl, v, lbh, st, rev):
    cs = HG_CHUNK
    sb = HG_SUB
    q = _silu(qraw)
    f = lbh + (1.0 - lbh) * _sigmoid(fl)
    k = 1.0 - f
    bc = _cumsum_rows(jnp.log(f) * LOG2E, rev)
    blast = bc[0:1] if rev else bc[cs - 1:cs]
    o = _dot_nt((q * jnp.exp2(bc)).astype(BF16), st.astype(BF16))
    kdec = k * jnp.exp2(blast - bc)
    st_new = st * jnp.exp2(blast) + _dot_tn(v.astype(BF16), kdec.astype(BF16))
    row = lax.broadcasted_iota(jnp.int32, (cs, HG_DK), 0)
    att_rows = []
    for i in range(cs // sb):
        lo, hi = i * sb, (i + 1) * sb
        att_i = _hgrn_diag_att(q[lo:hi], k[lo:hi], bc[lo:hi], lo, cs, rev)
        if (hi < cs) if rev else (lo > 0):
            mref = bc[hi:hi + 1] if rev else bc[lo - 1:lo]
            past = (row >= hi) if rev else (row < lo)
            qhat = q[lo:hi] * jnp.exp2(bc[lo:hi] - mref)
            kp = jnp.where(past, k * jnp.exp2(jnp.where(past, mref - bc, 0.0)), 0.0)
            att_i = att_i + _dot_nt(qhat.astype(BF16), kp.astype(BF16))
        att_rows.append(att_i)
    att = jnp.concatenate(att_rows, axis=0)
    return o + _dot(att.astype(BF16), v.astype(BF16)), st_new


def _hgrn_kernel(rev, layer, *refs):
    if rev:
        q_ref, f_ref, v_ref, lbl_ref, og_ref, of_ref, gn_ref, o_ref, st_ref = refs
    else:
        q_ref, f_ref, v_ref, lbl_ref, o_ref, st_ref = refs

    @pl.when(pl.program_id(1) == 0)
    def _():
        st_ref[...] = jnp.zeros_like(st_ref)

    lg = lbl_ref[...]
    e = jnp.exp(lg - jnp.max(lg, axis=0, keepdims=True))
    p = e / jnp.sum(e, axis=0, keepdims=True)
    lb = jnp.zeros_like(p[0:1])
    for jl in range(1, layer + 1):
        lb = lb + p[jl:jl + 1]

    cs = HG_CHUNK
    n_chunks = q_ref.shape[0] // cs

    def body(ci, carry):
        cidx = (n_chunks - 1 - ci) if rev else ci
        r0 = pl.multiple_of(cidx * cs, cs)
        for h in range(HG_HEADS):
            cols = slice(h * HG_DK, (h + 1) * HG_DK)
            o, st_new = _hgrn_chunk(q_ref[pl.ds(r0, cs), cols], f_ref[pl.ds(r0, cs), cols],
                                    v_ref[pl.ds(r0, cs), cols], lb[:, cols], st_ref[h], rev)
            st_ref[h] = st_new
            if rev:
                o = o + of_ref[pl.ds(r0, cs), cols]
                y = o * lax.rsqrt(jnp.mean(o * o, axis=-1, keepdims=True) + EPS) * gn_ref[...]
                o_ref[pl.ds(r0, cs), cols] = (y * _silu(og_ref[pl.ds(r0, cs), cols])).astype(BF16)
            else:
                o_ref[pl.ds(r0, cs), cols] = o
        return carry

    lax.fori_loop(0, n_chunks, body, 0)


def _hgrn_pass(rev, layer, u, lb_logits, gnorm, of, nb, ncc, nlc):
    t = u.shape[0]
    ch = SEQ_CHUNK
    w = HG_WIDTH
    nl = lb_logits.shape[0]
    blk = functools.partial(_seq_blk, rev=rev, ncc=ncc, nlc=nlc, nb=nb)
    base = A_IN // w
    col = lambda cb: pl.BlockSpec((ch, w), lambda b, j: (blk(b, j), cb))
    in_specs = [col(base), col(base + 1 + (1 if rev else 0)), col(base + 3),
                pl.BlockSpec((nl, w), lambda b, j: (0, 0))]
    args = [u, u, u, lb_logits]
    if rev:
        in_specs += [col(base + 4), pl.BlockSpec((ch, w), lambda b, j: (blk(b, j), 0)),
                     pl.BlockSpec((1, HG_DK), lambda b, j: (0, 0))]
        args += [u, of, gnorm.reshape(1, HG_DK)]
    return pl.pallas_call(
        functools.partial(_hgrn_kernel, rev, layer),
        grid=(nb, ncc + nlc),
        in_specs=in_specs,
        out_specs=pl.BlockSpec((ch, w), lambda b, j: (blk(b, j), 0)),
        out_shape=jax.ShapeDtypeStruct((t, w), BF16 if rev else F32),
        scratch_shapes=[pltpu.VMEM((HG_HEADS, HG_DK, HG_DK), F32)],
        compiler_params=_cparams("parallel", "arbitrary"),
        name="hgrn_bwd" if rev else "hgrn_fwd",
    )(*args)


def _rope_tables(n):
    rows = n // GRID_W
    row = jnp.repeat(jnp.arange(rows, dtype=F32), GRID_W)
    colp = jnp.tile(jnp.arange(GRID_W, dtype=F32), rows)
    inv = ROPE_THETA ** (-jnp.arange(ROPE_FREQS, dtype=F32) / ROPE_FREQS)
    ar = row[:, None] * inv
    ac = colp[:, None] * inv
    cos = jnp.concatenate([jnp.cos(ar), jnp.cos(ar), jnp.cos(ac), jnp.cos(ac)], axis=-1)
    sin = jnp.concatenate([-jnp.sin(ar), jnp.sin(ar), -jnp.sin(ac), jnp.sin(ac)], axis=-1)
    cos = jnp.concatenate([cos, jnp.ones((SEQ_CHUNK, HEAD_DIM), F32)], axis=0)
    sin = jnp.concatenate([sin, jnp.zeros((SEQ_CHUNK, HEAD_DIM), F32)], axis=0)
    return cos, sin


def _qkv_kernel(qa_ref, qb_ref, kv_ref, cos_ref, sin_ref, qn_ref, kn_ref, q_out, k_out, vt_out):
    cos = cos_ref[...]
    sin = sin_ref[...]
    lane = lax.broadcasted_iota(jnp.int32, cos.shape, 1)
    first = (lane % (2 * ROPE_FREQS)) < ROPE_FREQS

    def norm_rope(v, g, scale):
        y = v * lax.rsqrt(jnp.mean(v * v, axis=-1, keepdims=True) + EPS) * g
        partner = jnp.where(first, pltpu.roll(y, HEAD_DIM - ROPE_FREQS, 1), pltpu.roll(y, ROPE_FREQS, 1))
        out = y * cos + partner * sin
        if scale is not None:
            out = out * scale
        return out.astype(BF16)

    half = N_Q_HEADS // 2
    for h in range(half):
        cols = slice(h * HEAD_DIM, (h + 1) * HEAD_DIM)
        q_out[:, cols] = norm_rope(qa_ref[:, cols], qn_ref[...], Q_SCALE)
        cols_b = slice((half + h) * HEAD_DIM, (half + h + 1) * HEAD_DIM)
        q_out[:, cols_b] = norm_rope(qb_ref[:, cols], qn_ref[...], Q_SCALE)
    for h in range(N_KV_HEADS):
        cols = slice(h * HEAD_DIM, (h + 1) * HEAD_DIM)
        k_out[:, cols] = norm_rope(kv_ref[:, cols], kn_ref[...], None)
        vt_out[cols, :] = kv_ref[:, KV_WIDTH + h * HEAD_DIM:KV_WIDTH + (h + 1) * HEAD_DIM].T.astype(BF16)


def _qkv(u, cos, sin, qn, kn, n_lat, nlc):
    t = u.shape[0]
    ch = SEQ_CHUNK
    cw = 512
    base = (A_IN + B_IN) // cw
    tab = lambda i: (jnp.where(i < n_lat // ch, i % nlc, nlc), 0)
    return pl.pallas_call(
        _qkv_kernel,
        grid=(t // ch,),
        in_specs=[
            pl.BlockSpec((ch, cw), lambda i: (i, base)),
            pl.BlockSpec((ch, cw), lambda i: (i, base + 1)),
            pl.BlockSpec((ch, cw), lambda i: (i, base + 2)),
            pl.BlockSpec((ch, HEAD_DIM), tab),
            pl.BlockSpec((ch, HEAD_DIM), tab),
            pl.BlockSpec((1, HEAD_DIM), lambda i: (0, 0)),
            pl.BlockSpec((1, HEAD_DIM), lambda i: (0, 0)),
        ],
        out_specs=[
            pl.BlockSpec((ch, ATT_WIDTH), lambda i: (i, 0)),
            pl.BlockSpec((ch, KV_WIDTH), lambda i: (i, 0)),
            pl.BlockSpec((KV_WIDTH, ch), lambda i: (0, i)),
        ],
        out_shape=[
            jax.ShapeDtypeStruct((t, ATT_WIDTH), BF16),
            jax.ShapeDtypeStruct((t, KV_WIDTH), BF16),
            jax.ShapeDtypeStruct((KV_WIDTH, t), BF16),
        ],
        compiler_params=_cparams("parallel"),
        name="qkv_prep",
    )(u, u, u, cos, sin, qn.reshape(1, HEAD_DIM), kn.reshape(1, HEAD_DIM))


def _row_groups(v, op):
    return op(v.reshape(v.shape[0] // SUBLANES, SUBLANES, v.shape[1]), axis=0)


def _attn_body(with_lat, q_ref, kc_ref, vtc_ref, kl_ref, vtl_ref, o_ref, acc_ref, s_ref):
    tq = q_ref.shape[0]
    tk = s_ref.shape[1]
    n_tiles = kl_ref.shape[0] // tk if with_lat else 0

    for g in range(N_KV_HEADS):
        gcols = slice(g * HEAD_DIM, (g + 1) * HEAD_DIM)
        qs = jnp.concatenate(
            [q_ref[:, (g * Q_PER_KV + h) * HEAD_DIM:(g * Q_PER_KV + h + 1) * HEAD_DIM] for h in range(Q_PER_KV)],
            axis=0)

        def softmax_pv(s, vtt, carry):
            m_new = jnp.max(_row_groups(s, jnp.max), axis=0, keepdims=True)
            if carry is not None:
                m_new = jnp.maximum(carry[0], m_new)
            p = jnp.exp2(s - m_new)
            pv = _dot(vtt, p.astype(BF16))
            if carry is None:
                acc_ref[...] = pv
                return m_new, _row_groups(p, jnp.sum)
            alpha = jnp.exp2(carry[0] - m_new)
            acc_ref[...] = alpha * acc_ref[...] + pv
            return m_new, alpha * carry[1] + _row_groups(p, jnp.sum)

        def scores(t):
            r0 = t * tk if isinstance(t, int) else pl.multiple_of(t * tk, tk)
            return _dot_nt(kl_ref[pl.ds(r0, tk), gcols], qs)

        def consume(slot, t, carry):
            r0 = t * tk if isinstance(t, int) else pl.multiple_of(t * tk, tk)
            return softmax_pv(s_ref[slot], vtl_ref[gcols, pl.ds(r0, tk)], carry)

        if n_tiles:
            s_ref[0] = scores(0)
        carry = softmax_pv(_dot_nt(kc_ref[:, gcols], qs), vtc_ref[gcols, :], None)
        n_pairs = max(n_tiles // 2 - 1, 0)
        if n_pairs:
            def pair(j, carry):
                t0 = 2 * j
                s_ref[1] = scores(t0 + 1)
                carry = consume(0, t0, carry)
                s_ref[0] = scores(t0 + 2)
                return consume(1, t0 + 1, carry)
            carry = lax.fori_loop(0, n_pairs, pair, carry)
        for t in range(2 * n_pairs, n_tiles):
            if t + 1 < n_tiles:
                s_ref[(t + 1) % 2] = scores(t + 1)
            carry = consume(t % 2, t, carry)
        ot = acc_ref[...] * (1.0 / jnp.sum(carry[1], axis=0, keepdims=True))
        for h in range(Q_PER_KV):
            cols = slice((g * Q_PER_KV + h) * HEAD_DIM, (g * Q_PER_KV + h + 1) * HEAD_DIM)
            o_ref[:, cols] = ot[:, h * tq:(h + 1) * tq].T.astype(BF16)


def _attn_kernel(lat_steps, ctx_steps, *refs):
    if ctx_steps == 0:
        _attn_body(True, *refs)
        return
    i = pl.program_id(1)

    @pl.when(i < lat_steps)
    def _():
        _attn_body(True, *refs)

    @pl.when(i >= lat_steps)
    def _():
        _attn_body(False, *refs)


def _attn(q, k, vt, nb, n, nc, with_ctx_queries):
    n_lat = nb * n
    tq = ATT_QUERY_TILE
    lat_steps = n // tq
    ctx_steps = nc // tq if with_ctx_queries else 0
    rows = n_lat + (nb * nc if with_ctx_queries else 0)

    def q_map(b, i):
        return (jnp.where(i < lat_steps, b * lat_steps + i, n_lat // tq + b * ctx_steps + (i - lat_steps)), 0)

    m = Q_PER_KV * tq
    tk = min(ATT_KEY_TILE, n)
    assert n % tk == 0 and n % tq == 0 and nc % tq == 0
    return pl.pallas_call(
        functools.partial(_attn_kernel, lat_steps, ctx_steps),
        grid=(nb, lat_steps + ctx_steps),
        in_specs=[pl.BlockSpec((tq, ATT_WIDTH), q_map),
                  pl.BlockSpec((nc, KV_WIDTH), lambda b, i: (n_lat // nc + b, 0)),
                  pl.BlockSpec((KV_WIDTH, nc), lambda b, i: (0, n_lat // nc + b)),
                  pl.BlockSpec((n, KV_WIDTH), lambda b, i: (b, 0)),
                  pl.BlockSpec((KV_WIDTH, n), lambda b, i: (0, b))],
        out_specs=pl.BlockSpec((tq, ATT_WIDTH), q_map),
        out_shape=jax.ShapeDtypeStruct((rows, ATT_WIDTH), BF16),
        scratch_shapes=[pltpu.VMEM((HEAD_DIM, m), F32), pltpu.VMEM((2, tk, m), F32)],
        compiler_params=_cparams("parallel", "arbitrary"),
        name="attention",
    )(q, k, vt, k, vt)


def _outproj_kernel(routed, lat_tiles, *refs):
    if routed:
        ra_ref, hg_ref, at_ref, xl_ref, xc_ref, mod_ref, w_ref, g_ref, rt_ref, xo_ref, h_ref, lg_ref = refs
    else:
        ra_ref, hg_ref, at_ref, xl_ref, xc_ref, mod_ref, w_ref, g_ref, xo_ref, h_ref = refs
    w1 = ra_ref.shape[1]
    w2 = w1 + hg_ref.shape[1]
    mix = (_dot(ra_ref[...], w_ref[0:w1, :]) + _dot(hg_ref[...], w_ref[w1:w2, :])
           + _dot(at_ref[...], w_ref[w2:, :]))
    m = mod_ref[0]
    x_in = jnp.where(pl.program_id(0) < lat_tiles, xl_ref[...], xc_ref[...])
    x = x_in + m[2:3] * mix
    xo_ref[...] = x
    y = x * lax.rsqrt(jnp.mean(x * x, axis=-1, keepdims=True) + EPS) * g_ref[...]
    h = y * (1.0 + m[4:5]) + m[3:4]
    if routed:
        h_ref[...] = h
        h_hi = h.astype(BF16)
        h_lo = (h - h_hi.astype(F32)).astype(BF16)
        lg = _dot(h_hi, rt_ref[...]) + _dot(h_lo, rt_ref[...])
        lg_ref[...] = lg[:, 0:LANES] + lg[:, LANES:2 * LANES]
    else:
        h_ref[...] = h.astype(BF16)


def _outproj(ra, hg, at, x_lat, x_ctx, ctx_base, mod, w, g, router, n_tok, n_lat, seq, nb):
    d = x_lat.shape[1]
    tm = 256
    routed = router is not None
    row = functools.partial(_mod_row, tile=tm, n_lat=n_lat, seq=seq, nb=nb)
    tok = lambda width: pl.BlockSpec((tm, width), lambda i: (i, 0))
    lat_spec, ctx_spec = _stream_specs(tm, d, n_lat // tm, ctx_base // tm)
    in_specs = [tok(ra.shape[1]), tok(hg.shape[1]), tok(at.shape[1]), lat_spec, ctx_spec,
                pl.BlockSpec((1, N_MOD, d), lambda i: (row(i), 0, 0)),
                pl.BlockSpec(w.shape, lambda i: (0, 0)),
                pl.BlockSpec((1, d), lambda i: (0, 0))]
    args = [ra, hg, at, x_lat, x_ctx, mod, w, g.reshape(1, d)]
    out_specs = [tok(d), tok(d)]
    out_shape = [jax.ShapeDtypeStruct((n_tok, d), F32), jax.ShapeDtypeStruct((n_tok, d), F32 if routed else BF16)]
    if routed:
        in_specs.append(pl.BlockSpec((d, 2 * LANES), lambda i: (0, 0)))
        args.append(router)
        out_specs.append(tok(LANES))
        out_shape.append(jax.ShapeDtypeStruct((n_tok, LANES), F32))
    return pl.pallas_call(
        functools.partial(_outproj_kernel, routed, n_lat // tm),
        grid=(n_tok // tm,),
        in_specs=in_specs,
        out_specs=out_specs,
        out_shape=out_shape,
        compiler_params=_cparams("parallel"),
        name="outproj",
    )(*args)


def _ffn_kernel(h_ref, x_ref, mod_ref, w1_ref, w3_ref, w2_ref, o_ref, acc_ref):
    k = pl.program_id(1)

    @pl.when(k == 0)
    def _():
        acc_ref[...] = jnp.zeros_like(acc_ref)

    h = h_ref[...]
    a = _dot(h, w1_ref[...])
    z = _silu(a) * _dot(h, w3_ref[...])
    acc_ref[...] += _dot(z.astype(BF16), w2_ref[...])

    @pl.when(k == pl.num_programs(1) - 1)
    def _():
        o_ref[...] = x_ref[...] + mod_ref[0][5:6] * acc_ref[...]


def _ffn(h, xs, mod, w1, w3, w2, n_lat, seq, nb):
    t, d = h.shape
    f = w1.shape[1]
    tm = TOKEN_TILE
    tf = 512
    row = functools.partial(_mod_row, tile=tm, n_lat=n_lat, seq=seq, nb=nb)
    return pl.pallas_call(
        _ffn_kernel,
        grid=(t // tm, f // tf),
        in_specs=[
            pl.BlockSpec((tm, d), lambda i, k: (i, 0)),
            pl.BlockSpec((tm, d), lambda i, k: (i, 0)),
            pl.BlockSpec((1, N_MOD, d), lambda i, k: (row(i), 0, 0)),
            pl.BlockSpec((d, tf), lambda i, k: (0, k)),
            pl.BlockSpec((d, tf), lambda i, k: (0, k)),
            pl.BlockSpec((tf, d), lambda i, k: (k, 0)),
        ],
        out_specs=pl.BlockSpec((tm, d), lambda i, k: (i, 0)),
        out_shape=jax.ShapeDtypeStruct((t, d), F32),
        scratch_shapes=[pltpu.VMEM((tm, d), F32)],
        compiler_params=_cparams("parallel", "arbitrary"),
        name="ffn_dense",
    )(h, xs, mod, w1, w3, w2)


def _router_kernel(lg_ref, meta_ref, wts_ref, cnt_ref, run_ref):
    @pl.when(pl.program_id(0) == 0)
    def _():
        run_ref[...] = jnp.zeros_like(run_ref)

    lg = lg_ref[...]
    tm = lg.shape[0]
    lane = lax.broadcasted_iota(jnp.int32, lg.shape, 1)
    lane_f = lane.astype(F32)
    v = jnp.where(lane < N_EXPERTS, lg, -jnp.inf)
    m1 = jnp.max(v, axis=-1, keepdims=True)
    i1 = jnp.min(jnp.where(v == m1, lane_f, float(LANES)), axis=-1, keepdims=True)
    v2 = jnp.where(lane_f == i1, -jnp.inf, v)
    m2 = jnp.max(v2, axis=-1, keepdims=True)
    i2 = jnp.min(jnp.where(v2 == m2, lane_f, float(LANES)), axis=-1, keepdims=True)
    e = jnp.exp(m2 - m1)
    wt1 = 1.0 / (1.0 + e)
    wt2 = e / (1.0 + e)
    hit1 = lane_f == i1
    hit2 = lane_f == i2
    assign = jnp.where(hit1 | hit2, 1.0, 0.0)
    r = lax.broadcasted_iota(jnp.int32, (tm, tm), 0)
    c = lax.broadcasted_iota(jnp.int32, (tm, tm), 1)
    tri = jnp.where(r > c, 1.0, 0.0).astype(BF16)
    rank = _dot(tri, assign.astype(BF16)) + run_ref[0:1, :]
    r1 = jnp.sum(jnp.where(hit1, rank, 0.0), axis=-1, keepdims=True)
    r2 = jnp.sum(jnp.where(hit2, rank, 0.0), axis=-1, keepdims=True)
    run_ref[...] = run_ref[...] + jnp.sum(assign, axis=0, keepdims=True)
    meta = jnp.where(lane == 0, i1, jnp.where(lane == 1, i2, jnp.where(lane == 2, r1, jnp.where(lane == 3, r2, 0.0))))
    meta_ref[...] = meta.astype(jnp.int32)
    wts_ref[...] = jnp.where(lane == 0, wt1, jnp.where(lane == 1, wt2, 0.0))
    cnt_ref[...] = run_ref[...]


def _router(logits):
    t = logits.shape[0]
    tm = TOKEN_TILE
    tok = pl.BlockSpec((tm, LANES), lambda i: (i, 0))
    return pl.pallas_call(
        _router_kernel,
        grid=(t // tm,),
        in_specs=[tok],
        out_specs=[tok, tok, pl.BlockSpec((SUBLANES, LANES), lambda i: (0, 0))],
        out_shape=[jax.ShapeDtypeStruct((t, LANES), jnp.int32), jax.ShapeDtypeStruct((t, LANES), F32),
                   jax.ShapeDtypeStruct((SUBLANES, LANES), F32)],
        scratch_shapes=[pltpu.VMEM((SUBLANES, LANES), F32)],
        compiler_params=_cparams("arbitrary"),
        name="router",
    )(logits)


def _row_copy(src_ref, src_row, dst_ref, dst_row, sem):
    return pltpu.make_async_copy(src_ref.at[pl.ds(src_row, 1)], dst_ref.at[pl.ds(dst_row, 1)], sem)


def _dispatch_kernel(pos_ref, h_ref, xs_in_ref, xs_ref, pos_smem, sem_p, sem_d):
    del xs_in_ref
    tm = h_ref.shape[0]
    cp = pltpu.make_async_copy(pos_ref.at[0, 0], pos_smem, sem_p)
    cp.start()
    cp.wait()

    def issue(t, carry):
        _row_copy(h_ref, t, xs_ref, pos_smem[t], sem_d).start()
        _row_copy(h_ref, t, xs_ref, pos_smem[tm + t], sem_d).start()
        return carry

    lax.fori_loop(0, tm, issue, 0)

    def drain(t, carry):
        _row_copy(h_ref, 0, xs_ref, 0, sem_d).wait()
        _row_copy(h_ref, 0, xs_ref, 0, sem_d).wait()
        return carry

    lax.fori_loop(0, tm, drain, 0)


def _dispatch(pos, h, xs0):
    t, d = h.shape
    tm = 256
    nt = t // tm
    return pl.pallas_call(
        _dispatch_kernel,
        grid=(nt,),
        in_specs=[
            pl.BlockSpec((1, 1, 2 * tm), lambda i: (i, 0, 0)),
            pl.BlockSpec((tm, d), lambda i: (i, 0)),
            pl.BlockSpec(memory_space=pl.ANY),
        ],
        out_specs=pl.BlockSpec(memory_space=pl.ANY),
        out_shape=jax.ShapeDtypeStruct(xs0.shape, xs0.dtype),
        scratch_shapes=[pltpu.SMEM((2 * tm,), jnp.int32), pltpu.SemaphoreType.DMA(()), pltpu.SemaphoreType.DMA(())],
        input_output_aliases={2: 0},
        compiler_params=_cparams("arbitrary"),
        name="moe_dispatch",
    )(pos, h, xs0)


def _expert_kernel(te_ref, tv_ref, x_ref, w1_ref, w3_ref, w2_ref, y_ref, xb_ref, acc_ref):
    del te_ref
    i = pl.program_id(0)
    k = pl.program_id(1)

    @pl.when(tv_ref[i] > 0)
    def _():
        @pl.when(k == 0)
        def _():
            xb_ref[...] = x_ref[...].astype(BF16)
            acc_ref[...] = jnp.zeros_like(acc_ref)

        xb = xb_ref[...]
        a = _dot(xb, w1_ref[...])
        z = _silu(a) * _dot(xb, w3_ref[...])
        acc_ref[...] += _dot(z.astype(BF16), w2_ref[...])

        @pl.when(k == pl.num_programs(1) - 1)
        def _():
            y_ref[...] = acc_ref[...]

    @pl.when((tv_ref[i] == 0) & (k == 0))
    def _():
        y_ref[...] = jnp.zeros_like(y_ref)


def _experts(tile_expert, tile_valid, tile_row, xs, w1, w3, w2):
    rows, d = xs.shape
    f = w1.shape[2]
    tm = EXPERT_TILE
    tf = 512
    grid_spec = pltpu.PrefetchScalarGridSpec(
        num_scalar_prefetch=3,
        grid=(rows // tm, f // tf),
        in_specs=[
            pl.BlockSpec((tm, d), lambda i, k, te, tv, tr: (tr[i], 0)),
            pl.BlockSpec((None, d, tf), lambda i, k, te, tv, tr: (te[i], 0, jnp.where(tv[i] > 0, k, f // tf - 1))),
            pl.BlockSpec((None, d, tf), lambda i, k, te, tv, tr: (te[i], 0, jnp.where(tv[i] > 0, k, f // tf - 1))),
            pl.BlockSpec((None, tf, d), lambda i, k, te, tv, tr: (te[i], jnp.where(tv[i] > 0, k, f // tf - 1), 0)),
        ],
        out_specs=pl.BlockSpec((tm, d), lambda i, k, te, tv, tr: (i, 0)),
        scratch_shapes=[pltpu.VMEM((tm, d), BF16), pltpu.VMEM((tm, d), F32)],
    )

    def body(te_ref, tv_ref, tr_ref, *rest):
        del tr_ref
        _expert_kernel(te_ref, tv_ref, *rest)

    return pl.pallas_call(
        body,
        grid_spec=grid_spec,
        out_shape=jax.ShapeDtypeStruct((rows, d), F32),
        compiler_params=_cparams("arbitrary", "arbitrary"),
        name="moe_experts",
    )(tile_expert, tile_valid, tile_row, xs, w1, w3, w2)


def _combine_kernel(pos_ref, wts_ref, x_ref, mod_ref, ys_ref, o_ref, pos_smem, y1_ref, y2_ref, sem_p, sem_g):
    tm = x_ref.shape[0]
    cp = pltpu.make_async_copy(pos_ref.at[0, 0], pos_smem, sem_p)
    cp.start()
    cp.wait()

    def issue(t, carry):
        _row_copy(ys_ref, pos_smem[t], y1_ref, t, sem_g).start()
        _row_copy(ys_ref, pos_smem[tm + t], y2_ref, t, sem_g).start()
        return carry

    lax.fori_loop(0, tm, issue, 0)

    def drain(t, carry):
        _row_copy(ys_ref, 0, y1_ref, 0, sem_g).wait()
        _row_copy(ys_ref, 0, y2_ref, 0, sem_g).wait()
        return carry

    lax.fori_loop(0, tm, drain, 0)
    wts = wts_ref[...]
    f = wts[:, 0:1] * y1_ref[...] + wts[:, 1:2] * y2_ref[...]
    o_ref[...] = x_ref[...] + mod_ref[0][5:6] * f


def _combine(pos, wts, xs, mod, ys, n_lat, seq, nb):
    t, d = xs.shape
    tm = 256
    row = functools.partial(_mod_row, tile=tm, n_lat=n_lat, seq=seq, nb=nb)
    return pl.pallas_call(
        _combine_kernel,
        grid=(t // tm,),
        in_specs=[
            pl.BlockSpec((1, 1, 2 * tm), lambda i: (i, 0, 0)),
            pl.BlockSpec((tm, LANES), lambda i: (i, 0)),
            pl.BlockSpec((tm, d), lambda i: (i, 0)),
            pl.BlockSpec((1, N_MOD, d), lambda i: (row(i), 0, 0)),
            pl.BlockSpec(memory_space=pl.ANY),
        ],
        out_specs=pl.BlockSpec((tm, d), lambda i: (i, 0)),
        out_shape=jax.ShapeDtypeStruct((t, d), F32),
        scratch_shapes=[pltpu.SMEM((2 * tm,), jnp.int32), pltpu.VMEM((tm, d), F32), pltpu.VMEM((tm, d), F32),
                        pltpu.SemaphoreType.DMA(()), pltpu.SemaphoreType.DMA(())],
        compiler_params=_cparams("arbitrary"),
        name="moe_combine",
    )(pos, wts, xs, mod, ys)


def _moe(h, logits, xs, mod, w1, w3, w2, n_lat, seq, nb):
    t, d = h.shape
    meta, wts, counts = _router(logits)
    te_rows = EXPERT_TILE
    cnt = counts[0, :N_EXPERTS].astype(jnp.int32)
    padded = ((cnt + te_rows - 1) // te_rows) * te_rows
    ends = jnp.cumsum(padded)
    starts = ends - padded
    pos1 = jnp.take(starts, meta[:, 0]) + meta[:, 2]
    pos2 = jnp.take(starts, meta[:, 1]) + meta[:, 3]
    tmd = 256
    pos = jnp.concatenate([pos1.reshape(t // tmd, 1, tmd), pos2.reshape(t // tmd, 1, tmd)], axis=-1)
    n_tiles = (2 * t) // te_rows + N_EXPERTS
    tile_start = jnp.arange(n_tiles, dtype=jnp.int32) * te_rows
    n_valid = ends[-1] // te_rows
    tile_valid = (tile_start < ends[-1]).astype(jnp.int32)
    tile_row = jnp.minimum(jnp.arange(n_tiles, dtype=jnp.int32), n_valid - 1)
    tile_expert = jnp.sum((tile_row[:, None] * te_rows >= ends[None, :]).astype(jnp.int32), axis=1)
    xs0 = jnp.zeros((n_tiles * te_rows, d), F32)
    x_sorted = _dispatch(pos, h, xs0)
    y_sorted = _experts(tile_expert, tile_valid, tile_row, x_sorted, w1, w3, w2)
    return _combine(pos, wts, xs, mod, y_sorted, n_lat, seq, nb)


def kernel(x, c, ctx, c_ctx, w_mod, b_mod, norm1, norm2, w_in, w_out, conv_w, conv_b, lru_wa, lru_ba, lru_wi, lru_bi, lru_lambda, hgrn_lb_logits, hgrn_gnorm, q_norm, k_norm, ffn_w1, ffn_w3, ffn_w2, router, moe_w1, moe_w3, moe_w2):
    nb, n, d = x.shape
    nc = ctx.shape[1]
    depth = w_mod.shape[0]
    assert n % SEQ_CHUNK == 0 and nc % SEQ_CHUNK == 0 and n % TOKEN_TILE == 0
    assert (nb * nc) % TOKEN_TILE == 0 and n % GRID_W == 0
    n_lat = nb * n
    nlc = n // SEQ_CHUNK
    ncc = nc // SEQ_CHUNK

    n_all = n_lat + nb * nc
    x_lat, x_ctx, ctx_base = x.reshape(n_lat, d), ctx.reshape(nb * nc, d), 0
    mod_rows = 2 * SUBLANES * ((nb + 1 + 2 * SUBLANES - 1) // (2 * SUBLANES))
    cc = jnp.zeros((mod_rows, d), F32).at[:nb].set(c).at[nb].set(c_ctx)
    mod_all = _mod_table(cc, w_mod, b_mod).reshape(depth, mod_rows, N_MOD, d)
    cos, sin = _rope_tables(n)
    router_f = jnp.pad(router, ((0, 0), (0, 0), (0, LANES - router.shape[-1])))
    router_hi = router_f.astype(BF16)
    router_lo = (router_f - router_hi.astype(F32)).astype(BF16)
    router_p = jnp.concatenate([router_hi, router_lo], axis=-1)

    for l in range(depth):
        need_ctx = l < depth - 1
        mod = mod_all[l]
        u = _inproj(x_lat, x_ctx, ctx_base, n_all, mod, norm1[l], _to_bf16(w_in, l), n_lat, n, nb)

        lru_args = lambda dd: (conv_w[l], conv_b[l], lru_wa[l, dd], lru_ba[l, dd], lru_wi[l, dd], lru_bi[l, dd],
                               lru_lambda[l, dd])
        hf = _lru_pass(False, u, *lru_args(0), None, nb, ncc, nlc)
        ra = _lru_pass(True, u, *lru_args(1), hf, nb, ncc, nlc)

        of = _hgrn_pass(False, l, u, hgrn_lb_logits[0], hgrn_gnorm[l], None, nb, ncc, nlc)
        hg = _hgrn_pass(True, l, u, hgrn_lb_logits[1], hgrn_gnorm[l], of, nb, ncc, nlc)

        q, k, vt = _qkv(u, cos, sin, q_norm[l], k_norm[l], n_lat, nlc)
        at = _attn(q, k, vt, nb, n, nc, need_ctx)

        n_tok = n_all if need_ctx else n_lat
        j = l // 2
        routed = l % 2 == 1
        outs = _outproj(ra, hg, at, x_lat, x_ctx, ctx_base, mod, _to_bf16(w_out, l), norm2[l],
                        router_p[j] if routed else None, n_tok, n_lat, n, nb)
        if routed:
            xn, h2, logits = outs
            xs = _moe(h2, logits, xn, mod, _to_bf16(moe_w1, j), _to_bf16(moe_w3, j), _to_bf16(moe_w2, j),
                      n_lat, n, nb)
        else:
            xn, h2 = outs
            xs = _ffn(h2, xn, mod, _to_bf16(ffn_w1, j), _to_bf16(ffn_w3, j), _to_bf16(ffn_w2, j), n_lat, n, nb)
        x_lat, x_ctx, ctx_base = xs, xs, n_lat
    return xs[:n_lat].reshape(nb, n, d)
```

```python
import functools

import jax
import jax.numpy as jnp
from jax import lax
from jax.experimental import pallas as pl
from jax.experimental.pallas import tpu as pltpu

F32 = jnp.float32
BF16 = jnp.bfloat16

EPS = 1e-6
N_MOD = 6
GRID_W = 64
LRU_WIDTH = 512
LRU_BLOCKS = 4
LRU_BLOCK = LRU_WIDTH // LRU_BLOCKS
LRU_C = 8.0
HG_HEADS = 4
HG_DK = 128
HG_WIDTH = HG_HEADS * HG_DK
HG_CHUNK = 64
HG_SUB = 8
HEAD_DIM = 128
N_Q_HEADS = 8
N_KV_HEADS = 2
Q_PER_KV = N_Q_HEADS // N_KV_HEADS
ATT_WIDTH = N_Q_HEADS * HEAD_DIM
KV_WIDTH = N_KV_HEADS * HEAD_DIM
ROPE_THETA = 10000.0
ROPE_FREQS = HEAD_DIM // 4
ATTN_SCALE = HEAD_DIM ** -0.5
LOG2E = 1.4426950408889634
Q_SCALE = ATTN_SCALE * LOG2E
A_IN = 2 * LRU_WIDTH
B_IN = 5 * HG_WIDTH
C_IN = ATT_WIDTH + 2 * KV_WIDTH
IN_WIDTH = A_IN + B_IN + C_IN
N_EXPERTS = 8

LANES = 128
SUBLANES = 8
SEQ_CHUNK = 256
TOKEN_TILE = 512
EXPERT_TILE = 512
ATT_KEY_TILE = 1024
ATT_QUERY_TILE = 256
CAST_BLOCK_ELEMS = 1024 * 1024
VMEM_LIMIT = 56 * 1024 * 1024
NEG_BIG = -1e30


def _cparams(*sem):
    return pltpu.CompilerParams(dimension_semantics=sem, vmem_limit_bytes=VMEM_LIMIT)


def _sigmoid(v):
    return 1.0 / (1.0 + jnp.exp(-v))


def _silu(v):
    return v * _sigmoid(v)


def _dot(a, b):
    return jnp.dot(a, b, preferred_element_type=F32)


def _dot_nt(a, b):
    return lax.dot_general(a, b, (((1,), (1,)), ((), ())), preferred_element_type=F32)


def _dot_tn(a, b):
    return lax.dot_general(a, b, (((0,), (0,)), ((), ())), preferred_element_type=F32)


def _mod_row(i, tile, n_lat, seq, nb):
    return jnp.where(i < n_lat // tile, (i * tile) // seq, nb)


def _cast_kernel(w_ref, o_ref):
    o_ref[...] = w_ref[...].astype(BF16)


def _to_bf16(w, index):
    inner = w.shape[1:]
    cols = inner[-1]
    rows = 1
    for s in inner[:-1]:
        rows *= s
    tr = 2 * SUBLANES
    while tr * 2 * cols <= CAST_BLOCK_ELEMS and rows % (tr * 2) == 0:
        tr *= 2
    assert rows % tr == 0
    steps = rows // tr
    out = pl.pallas_call(
        _cast_kernel,
        grid=(steps,),
        in_specs=[pl.BlockSpec((tr, cols), lambda i: (index * steps + i, 0))],
        out_specs=pl.BlockSpec((tr, cols), lambda i: (i, 0)),
        out_shape=jax.ShapeDtypeStruct((rows, cols), BF16),
        compiler_params=_cparams("parallel"),
        name="to_bf16",
    )(w.reshape(w.shape[0] * rows, cols))
    return out.reshape(inner)


def _mod_kernel(c_ref, w_ref, b_ref, o_ref):
    s = _silu(c_ref[...]).astype(BF16)
    o_ref[0] = _dot(s, w_ref[0].astype(BF16)) + b_ref[0]


def _mod_table(cc, w_mod, b_mod):
    nl, d, md = w_mod.shape
    rows = cc.shape[0]
    tn = 1024 if md % 1024 == 0 else md
    return pl.pallas_call(
        _mod_kernel,
        grid=(nl, md // tn),
        in_specs=[
            pl.BlockSpec((rows, d), lambda l, j: (0, 0)),
            pl.BlockSpec((1, d, tn), lambda l, j: (l, 0, j)),
            pl.BlockSpec((1, 1, tn), lambda l, j: (l, 0, j)),
        ],
        out_specs=pl.BlockSpec((1, rows, tn), lambda l, j: (l, 0, j)),
        out_shape=jax.ShapeDtypeStruct((nl, rows, md), F32),
        compiler_params=_cparams("parallel", "parallel"),
        name="mod_table",
    )(cc, w_mod, b_mod.reshape(nl, 1, md))


def _stream_specs(tm, d, lat_tiles, ctx_base):
    lat = pl.BlockSpec((tm, d), lambda i, *_: (jnp.minimum(i, lat_tiles - 1), 0))
    ctx = pl.BlockSpec((tm, d), lambda i, *_: (ctx_base + jnp.maximum(i - lat_tiles, 0), 0))
    return lat, ctx


def _inproj_kernel(lat_tiles, xl_ref, xc_ref, mod_ref, g_ref, w_ref, o_ref, h_ref):
    i = pl.program_id(0)
    first = pl.program_id(1) == 0

    def norm_mod(x_ref):
        x = x_ref[...]
        y = x * lax.rsqrt(jnp.mean(x * x, axis=-1, keepdims=True) + EPS) * g_ref[...]
        m = mod_ref[0]
        h_ref[...] = (y * (1.0 + m[1:2]) + m[0:1]).astype(BF16)

    @pl.when(first & (i < lat_tiles))
    def _():
        norm_mod(xl_ref)

    @pl.when(first & (i >= lat_tiles))
    def _():
        norm_mod(xc_ref)

    o_ref[...] = _dot(h_ref[...], w_ref[...])


def _inproj(x_lat, x_ctx, ctx_base, n_tok, mod, g, w, n_lat, seq, nb):
    d = x_lat.shape[1]
    width = w.shape[1]
    tm = TOKEN_TILE
    tn = 1024
    row = functools.partial(_mod_row, tile=tm, n_lat=n_lat, seq=seq, nb=nb)
    lat_spec, ctx_spec = _stream_specs(tm, d, n_lat // tm, ctx_base // tm)
    return pl.pallas_call(
        functools.partial(_inproj_kernel, n_lat // tm),
        grid=(n_tok // tm, width // tn),
        in_specs=[
            lat_spec,
            ctx_spec,
            pl.BlockSpec((1, N_MOD, d), lambda i, j: (row(i), 0, 0)),
            pl.BlockSpec((1, d), lambda i, j: (0, 0)),
            pl.BlockSpec((d, tn), lambda i, j: (0, j)),
        ],
        out_specs=pl.BlockSpec((tm, tn), lambda i, j: (i, j)),
        out_shape=jax.ShapeDtypeStruct((n_tok, width), F32),
        scratch_shapes=[pltpu.VMEM((tm, d), BF16)],
        compiler_params=_cparams("parallel", "arbitrary"),
        name="inproj",
    )(x_lat, x_ctx, mod, g.reshape(1, d), w)


def _seq_pos(j, rev, ncc, nlc):
    is_ctx = j < ncc
    jl = j - ncc
    cc = (ncc - 1 - j) if rev else j
    cl = (nlc - 1 - jl) if rev else jl
    return is_ctx, cc, cl


def _seq_blk(b, j, rev, ncc, nlc, nb):
    is_ctx, cc, cl = _seq_pos(j, rev, ncc, nlc)
    return jnp.where(is_ctx, nb * nlc + b * ncc + cc, b * nlc + cl)


def _scan_rows(a_ref, b_ref, h_ref, hc_ref, rev):
    n_groups = a_ref.shape[0] // SUBLANES
    width = a_ref.shape[1]
    rows = lax.broadcasted_iota(jnp.int32, (SUBLANES, width), 0)

    def body(g, hc):
        gi = (n_groups - 1 - g) if rev else g
        r0 = pl.multiple_of(gi * SUBLANES, SUBLANES)
        a = a_ref[pl.ds(r0, SUBLANES), :]
        b = b_ref[pl.ds(r0, SUBLANES), :]
        for s in (1, 2, 4):
            if rev:
                a_s = pltpu.roll(a, SUBLANES - s, 0)
                b_s = pltpu.roll(b, SUBLANES - s, 0)
                keep = rows < SUBLANES - s
            else:
                a_s = pltpu.roll(a, s, 0)
                b_s = pltpu.roll(b, s, 0)
                keep = rows >= s
            b = a * jnp.where(keep, b_s, 0.0) + b
            a = a * jnp.where(keep, a_s, 1.0)
        h = b + a * hc
        h_ref[pl.ds(r0, SUBLANES), :] = h
        return h[0:1] if rev else h[SUBLANES - 1:SUBLANES]

    hc_ref[...] = lax.fori_loop(0, n_groups, body, hc_ref[...])


def _lru_kernel(rev, ncc, nlc, *refs):
    if rev:
        (u_ref, prev_ref, next_ref, cw_ref, cb_ref, wa_ref, ba_ref, wi_ref, bi_ref, lam_ref,
         hf_ref, o_ref, xe_ref, a_ref, b_ref, h_ref, hc_ref) = refs
    else:
        (u_ref, prev_ref, next_ref, cw_ref, cb_ref, wa_ref, ba_ref, wi_ref, bi_ref, lam_ref,
         o_ref, xe_ref, a_ref, b_ref, hc_ref) = refs
        h_ref = o_ref
    j = pl.program_id(1)
    is_ctx, cc, cl = _seq_pos(j, rev, ncc, nlc)
    c = jnp.where(is_ctx, cc, cl)
    nseq = jnp.where(is_ctx, ncc, nlc)

    @pl.when(j == 0)
    def _():
        hc_ref[...] = jnp.zeros_like(hc_ref)

    ch = SEQ_CHUNK
    w = LRU_WIDTH
    xe_ref[SUBLANES:SUBLANES + ch, :] = u_ref[:, w:2 * w]
    xe_ref[0:SUBLANES, :] = jnp.where(c > 0, prev_ref[...], 0.0)
    xe_ref[SUBLANES + ch:2 * SUBLANES + ch, :] = jnp.where(c < nseq - 1, next_ref[...], 0.0)
    cw = cw_ref[...]
    xc = cb_ref[...]
    for tap in range(4):
        off = SUBLANES - 2 + tap
        xc = xc + cw[tap:tap + 1] * xe_ref[off:off + ch, :]
    xb = xc.astype(BF16)
    ra = jnp.concatenate(
        [_dot(xb[:, n * LRU_BLOCK:(n + 1) * LRU_BLOCK], wa_ref[n]) for n in range(LRU_BLOCKS)], axis=-1)
    ri = jnp.concatenate(
        [_dot(xb[:, n * LRU_BLOCK:(n + 1) * LRU_BLOCK], wi_ref[n]) for n in range(LRU_BLOCKS)], axis=-1)
    r = _sigmoid(ra + ba_ref[...])
    gate_i = _sigmoid(ri + bi_ref[...])
    neg_lam = -lam_ref[...]
    softplus = jnp.maximum(neg_lam, 0.0) + jnp.log1p(jnp.exp(-jnp.abs(neg_lam)))
    log_a = -LRU_C * r * softplus
    a = jnp.exp(log_a)
    a_ref[...] = a
    b_ref[...] = jnp.sqrt(-jnp.tanh(log_a) * (a * a + 1.0)) * (gate_i * xc)
    _scan_rows(a_ref, b_ref, h_ref, hc_ref, rev)
    if rev:
        y = u_ref[:, 0:w]
        gelu = 0.5 * y * (1.0 + jnp.tanh(0.7978845608028654 * (y + 0.044715 * (y * y * y))))
        o_ref[...] = (gelu * (hf_ref[...] + h_ref[...])).astype(BF16)


def _lru_pass(rev, u, cw, cb, wa, ba, wi, bi, lam, hf, nb, ncc, nlc):
    t = u.shape[0]
    ch = SEQ_CHUNK
    w = LRU_WIDTH
    r8 = ch // SUBLANES
    blk = functools.partial(_seq_blk, rev=rev, ncc=ncc, nlc=nlc, nb=nb)
    full2 = lambda b, j: (0, 0)
    in_specs = [
        pl.BlockSpec((ch, 2 * w), lambda b, j: (blk(b, j), 0)),
        pl.BlockSpec((SUBLANES, w), lambda b, j: (jnp.maximum(blk(b, j) * r8 - 1, 0), 1)),
        pl.BlockSpec((SUBLANES, w), lambda b, j: (jnp.minimum(blk(b, j) * r8 + r8, t // SUBLANES - 1), 1)),
        pl.BlockSpec((4, w), full2),
        pl.BlockSpec((1, w), full2),
        pl.BlockSpec((LRU_BLOCKS, LRU_BLOCK, LRU_BLOCK), lambda b, j: (0, 0, 0)),
        pl.BlockSpec((1, w), full2),
        pl.BlockSpec((LRU_BLOCKS, LRU_BLOCK, LRU_BLOCK), lambda b, j: (0, 0, 0)),
        pl.BlockSpec((1, w), full2),
        pl.BlockSpec((1, w), full2),
    ]
    args = [u, u, u, cw, cb.reshape(1, w), wa.astype(BF16), ba.reshape(1, w), wi.astype(BF16),
            bi.reshape(1, w), lam.reshape(1, w)]
    scratch = [pltpu.VMEM((ch + 2 * SUBLANES, w), F32), pltpu.VMEM((ch, w), F32), pltpu.VMEM((ch, w), F32)]
    if rev:
        in_specs.append(pl.BlockSpec((ch, w), lambda b, j: (blk(b, j), 0)))
        args.append(hf)
        scratch.append(pltpu.VMEM((ch, w), F32))
        out_dtype = BF16
    else:
        out_dtype = F32
    scratch.append(pltpu.VMEM((1, w), F32))
    return pl.pallas_call(
        functools.partial(_lru_kernel, rev, ncc, nlc),
        grid=(nb, ncc + nlc),
        in_specs=in_specs,
        out_specs=pl.BlockSpec((ch, w), lambda b, j: (blk(b, j), 0)),
        out_shape=jax.ShapeDtypeStruct((t, w), out_dtype),
        scratch_shapes=scratch,
        compiler_params=_cparams("parallel", "arbitrary"),
        name="lru_bwd" if rev else "lru_fwd",
    )(*args)


def _cumsum_rows(v, rev):
    n = v.shape[0]
    rows = lax.broadcasted_iota(jnp.int32, v.shape, 0)
    s = 1
    while s < n:
        if rev:
            v = v + jnp.where(rows < n - s, pltpu.roll(v, n - s, 0), 0.0)
        else:
            v = v + jnp.where(rows >= s, pltpu.roll(v, s, 0), 0.0)
        s *= 2
    return v


def _hgrn_diag_att(q, k, bc, lo, width, rev):
    n = q.shape[0]
    lane = lax.broadcasted_iota(jnp.int32, (n, width), 1)
    row = lax.broadcasted_iota(jnp.int32, (n, width), 0)
    att = jnp.zeros((n, width), F32)
    for s in range(n):
        w = q * (k[s:s + 1] * jnp.exp2(bc - bc[s:s + 1]))
        att = jnp.where(lane == lo + s, jnp.sum(w, axis=-1, keepdims=True), att)
    keep = (row + lo <= lane) if rev else (row + lo >= lane)
    return jnp.where(keep, att, 0.0)


def _hgrn_chunk(qraw, fl, v, lbh, st, rev):
    cs = HG_CHUNK
    sb = HG_SUB
    q = _silu(qraw)
    f = lbh + (1.0 - lbh) * _sigmoid(fl)
    k = 1.0 - f
    bc = _cumsum_rows(jnp.log(f) * LOG2E, rev)
    blast = bc[0:1] if rev else bc[cs - 1:cs]
    o = _dot_nt((q * jnp.exp2(bc)).astype(BF16), st.astype(BF16))
    kdec = k * jnp.exp2(blast - bc)
    st_new = st * jnp.exp2(blast) + _dot_tn(v.astype(BF16), kdec.astype(BF16))
    row = lax.broadcasted_iota(jnp.int32, (cs, HG_DK), 0)
    att_rows = []
    for i in range(cs // sb):
        lo, hi = i * sb, (i + 1) * sb
        att_i = _hgrn_diag_att(q[lo:hi], k[lo:hi], bc[lo:hi], lo, cs, rev)
        if (hi < cs) if rev else (lo > 0):
            mref = bc[hi:hi + 1] if rev else bc[lo - 1:lo]
            past = (row >= hi) if rev else (row < lo)
            qhat = q[lo:hi] * jnp.exp2(bc[lo:hi] - mref)
            kp = jnp.where(past, k * jnp.exp2(jnp.where(past, mref - bc, 0.0)), 0.0)
            att_i = att_i + _dot_nt(qhat.astype(BF16), kp.astype(BF16))
        att_rows.append(att_i)
    att = jnp.concatenate(att_rows, axis=0)
    return o + _dot(att.astype(BF16), v.astype(BF16)), st_new


def _hgrn_kernel(rev, layer, *refs):
    if rev:
        q_ref, f_ref, v_ref, lbl_ref, og_ref, of_ref, gn_ref, o_ref, st_ref = refs
    else:
        q_ref, f_ref, v_ref, lbl_ref, o_ref, st_ref = refs

    @pl.when(pl.program_id(1) == 0)
    def _():
        st_ref[...] = jnp.zeros_like(st_ref)

    lg = lbl_ref[...]
    e = jnp.exp(lg - jnp.max(lg, axis=0, keepdims=True))
    p = e / jnp.sum(e, axis=0, keepdims=True)
    lb = jnp.zeros_like(p[0:1])
    for jl in range(1, layer + 1):
        lb = lb + p[jl:jl + 1]

    cs = HG_CHUNK
    n_chunks = q_ref.shape[0] // cs

    def body(ci, carry):
        cidx = (n_chunks - 1 - ci) if rev else ci
        r0 = pl.multiple_of(cidx * cs, cs)
        for h in range(HG_HEADS):
            cols = slice(h * HG_DK, (h + 1) * HG_DK)
            o, st_new = _hgrn_chunk(q_ref[pl.ds(r0, cs), cols], f_ref[pl.ds(r0, cs), cols],
                                    v_ref[pl.ds(r0, cs), cols], lb[:, cols], st_ref[h], rev)
            st_ref[h] = st_new
            if rev:
                o = o + of_ref[pl.ds(r0, cs), cols]
                y = o * lax.rsqrt(jnp.mean(o * o, axis=-1, keepdims=True) + EPS) * gn_ref[...]
                o_ref[pl.ds(r0, cs), cols] = (y * _silu(og_ref[pl.ds(r0, cs), cols])).astype(BF16)
            else:
                o_ref[pl.ds(r0, cs), cols] = o
        return carry

    lax.fori_loop(0, n_chunks, body, 0)


def _hgrn_pass(rev, layer, u, lb_logits, gnorm, of, nb, ncc, nlc):
    t = u.shape[0]
    ch = SEQ_CHUNK
    w = HG_WIDTH
    nl = lb_logits.shape[0]
    blk = functools.partial(_seq_blk, rev=rev, ncc=ncc, nlc=nlc, nb=nb)
    base = A_IN // w
    col = lambda cb: pl.BlockSpec((ch, w), lambda b, j: (blk(b, j), cb))
    in_specs = [col(base), col(base + 1 + (1 if rev else 0)), col(base + 3),
                pl.BlockSpec((nl, w), lambda b, j: (0, 0))]
    args = [u, u, u, lb_logits]
    if rev:
        in_specs += [col(base + 4), pl.BlockSpec((ch, w), lambda b, j: (blk(b, j), 0)),
                     pl.BlockSpec((1, HG_DK), lambda b, j: (0, 0))]
        args += [u, of, gnorm.reshape(1, HG_DK)]
    return pl.pallas_call(
        functools.partial(_hgrn_kernel, rev, layer),
        grid=(nb, ncc + nlc),
        in_specs=in_specs,
        out_specs=pl.BlockSpec((ch, w), lambda b, j: (blk(b, j), 0)),
        out_shape=jax.ShapeDtypeStruct((t, w), BF16 if rev else F32),
        scratch_shapes=[pltpu.VMEM((HG_HEADS, HG_DK, HG_DK), F32)],
        compiler_params=_cparams("parallel", "arbitrary"),
        name="hgrn_bwd" if rev else "hgrn_fwd",
    )(*args)


def _rope_tables(n):
    rows = n // GRID_W
    row = jnp.repeat(jnp.arange(rows, dtype=F32), GRID_W)
    colp = jnp.tile(jnp.arange(GRID_W, dtype=F32), rows)
    inv = ROPE_THETA ** (-jnp.arange(ROPE_FREQS, dtype=F32) / ROPE_FREQS)
    ar = row[:, None] * inv
    ac = colp[:, None] * inv
    cos = jnp.concatenate([jnp.cos(ar), jnp.cos(ar), jnp.cos(ac), jnp.cos(ac)], axis=-1)
    sin = jnp.concatenate([-jnp.sin(ar), jnp.sin(ar), -jnp.sin(ac), jnp.sin(ac)], axis=-1)
    cos = jnp.concatenate([cos, jnp.ones((SEQ_CHUNK, HEAD_DIM), F32)], axis=0)
    sin = jnp.concatenate([sin, jnp.zeros((SEQ_CHUNK, HEAD_DIM), F32)], axis=0)
    return cos, sin


def _qkv_kernel(qa_ref, qb_ref, kv_ref, cos_ref, sin_ref, qn_ref, kn_ref, q_out, k_out, vt_out):
    cos = cos_ref[...]
    sin = sin_ref[...]
    lane = lax.broadcasted_iota(jnp.int32, cos.shape, 1)
    first = (lane % (2 * ROPE_FREQS)) < ROPE_FREQS

    def norm_rope(v, g, scale):
        y = v * lax.rsqrt(jnp.mean(v * v, axis=-1, keepdims=True) + EPS) * g
        partner = jnp.where(first, pltpu.roll(y, HEAD_DIM - ROPE_FREQS, 1), pltpu.roll(y, ROPE_FREQS, 1))
        out = y * cos + partner * sin
        if scale is not None:
            out = out * scale
        return out.astype(BF16)

    half = N_Q_HEADS // 2
    for h in range(half):
        cols = slice(h * HEAD_DIM, (h + 1) * HEAD_DIM)
        q_out[:, cols] = norm_rope(qa_ref[:, cols], qn_ref[...], Q_SCALE)
        cols_b = slice((half + h) * HEAD_DIM, (half + h + 1) * HEAD_DIM)
        q_out[:, cols_b] = norm_rope(qb_ref[:, cols], qn_ref[...], Q_SCALE)
    for h in range(N_KV_HEADS):
        cols = slice(h * HEAD_DIM, (h + 1) * HEAD_DIM)
        k_out[:, cols] = norm_rope(kv_ref[:, cols], kn_ref[...], None)
        vt_out[cols, :] = kv_ref[:, KV_WIDTH + h * HEAD_DIM:KV_WIDTH + (h + 1) * HEAD_DIM].T.astype(BF16)


def _qkv(u, cos, sin, qn, kn, n_lat, nlc):
    t = u.shape[0]
    ch = SEQ_CHUNK
    cw = 512
    base = (A_IN + B_IN) // cw
    tab = lambda i: (jnp.where(i < n_lat // ch, i % nlc, nlc), 0)
    return pl.pallas_call(
        _qkv_kernel,
        grid=(t // ch,),
        in_specs=[
            pl.BlockSpec((ch, cw), lambda i: (i, base)),
            pl.BlockSpec((ch, cw), lambda i: (i, base + 1)),
            pl.BlockSpec((ch, cw), lambda i: (i, base + 2)),
            pl.BlockSpec((ch, HEAD_DIM), tab),
            pl.BlockSpec((ch, HEAD_DIM), tab),
            pl.BlockSpec((1, HEAD_DIM), lambda i: (0, 0)),
            pl.BlockSpec((1, HEAD_DIM), lambda i: (0, 0)),
        ],
        out_specs=[
            pl.BlockSpec((ch, ATT_WIDTH), lambda i: (i, 0)),
            pl.BlockSpec((ch, KV_WIDTH), lambda i: (i, 0)),
            pl.BlockSpec((KV_WIDTH, ch), lambda i: (0, i)),
        ],
        out_shape=[
            jax.ShapeDtypeStruct((t, ATT_WIDTH), BF16),
            jax.ShapeDtypeStruct((t, KV_WIDTH), BF16),
            jax.ShapeDtypeStruct((KV_WIDTH, t), BF16),
        ],
        compiler_params=_cparams("parallel"),
        name="qkv_prep",
    )(u, u, u, cos, sin, qn.reshape(1, HEAD_DIM), kn.reshape(1, HEAD_DIM))


def _row_groups(v, op):
    return op(v.reshape(v.shape[0] // SUBLANES, SUBLANES, v.shape[1]), axis=0)


def _attn_body(with_lat, q_ref, kc_ref, vtc_ref, kl_ref, vtl_ref, o_ref, acc_ref, s_ref):
    tq = q_ref.shape[0]
    tk = s_ref.shape[1]
    n_tiles = kl_ref.shape[0] // tk if with_lat else 0

    for g in range(N_KV_HEADS):
        gcols = slice(g * HEAD_DIM, (g + 1) * HEAD_DIM)
        qs = jnp.concatenate(
            [q_ref[:, (g * Q_PER_KV + h) * HEAD_DIM:(g * Q_PER_KV + h + 1) * HEAD_DIM] for h in range(Q_PER_KV)],
            axis=0)

        def softmax_pv(s, vtt, carry):
            m_new = jnp.max(_row_groups(s, jnp.max), axis=0, keepdims=True)
            if carry is not None:
                m_new = jnp.maximum(carry[0], m_new)
            p = jnp.exp2(s - m_new)
            pv = _dot(vtt, p.astype(BF16))
            if carry is None:
                acc_ref[...] = pv
                return m_new, _row_groups(p, jnp.sum)
            alpha = jnp.exp2(carry[0] - m_new)
            acc_ref[...] = alpha * acc_ref[...] + pv
            return m_new, alpha * carry[1] + _row_groups(p, jnp.sum)

        def scores(t):
            r0 = t * tk if isinstance(t, int) else pl.multiple_of(t * tk, tk)
            return _dot_nt(kl_ref[pl.ds(r0, tk), gcols], qs)

        def consume(slot, t, carry):
            r0 = t * tk if isinstance(t, int) else pl.multiple_of(t * tk, tk)
            return softmax_pv(s_ref[slot], vtl_ref[gcols, pl.ds(r0, tk)], carry)

        if n_tiles:
            s_ref[0] = scores(0)
        carry = softmax_pv(_dot_nt(kc_ref[:, gcols], qs), vtc_ref[gcols, :], None)
        n_pairs = max(n_tiles // 2 - 1, 0)
        if n_pairs:
            def pair(j, carry):
                t0 = 2 * j
                s_ref[1] = scores(t0 + 1)
                carry = consume(0, t0, carry)
                s_ref[0] = scores(t0 + 2)
                return consume(1, t0 + 1, carry)
            carry = lax.fori_loop(0, n_pairs, pair, carry)
        for t in range(2 * n_pairs, n_tiles):
            if t + 1 < n_tiles:
                s_ref[(t + 1) % 2] = scores(t + 1)
            carry = consume(t % 2, t, carry)
        ot = acc_ref[...] * (1.0 / jnp.sum(carry[1], axis=0, keepdims=True))
        for h in range(Q_PER_KV):
            cols = slice((g * Q_PER_KV + h) * HEAD_DIM, (g * Q_PER_KV + h + 1) * HEAD_DIM)
            o_ref[:, cols] = ot[:, h * tq:(h + 1) * tq].T.astype(BF16)


def _attn_kernel(lat_steps, ctx_steps, *refs):
    if ctx_steps == 0:
        _attn_body(True, *refs)
        return
    i = pl.program_id(1)

    @pl.when(i < lat_steps)
    def _():
        _attn_body(True, *refs)

    @pl.when(i >= lat_steps)
    def _():
        _attn_body(False, *refs)


def _attn(q, k, vt, nb, n, nc, with_ctx_queries):
    n_lat = nb * n
    tq = ATT_QUERY_TILE
    lat_steps = n // tq
    ctx_steps = nc // tq if with_ctx_queries else 0
    rows = n_lat + (nb * nc if with_ctx_queries else 0)

    def q_map(b, i):
        return (jnp.where(i < lat_steps, b * lat_steps + i, n_lat // tq + b * ctx_steps + (i - lat_steps)), 0)

    m = Q_PER_KV * tq
    tk = min(ATT_KEY_TILE, n)
    assert n % tk == 0 and n % tq == 0 and nc % tq == 0
    return pl.pallas_call(
        functools.partial(_attn_kernel, lat_steps, ctx_steps),
        grid=(nb, lat_steps + ctx_steps),
        in_specs=[pl.BlockSpec((tq, ATT_WIDTH), q_map),
                  pl.BlockSpec((nc, KV_WIDTH), lambda b, i: (n_lat // nc + b, 0)),
                  pl.BlockSpec((KV_WIDTH, nc), lambda b, i: (0, n_lat // nc + b)),
                  pl.BlockSpec((n, KV_WIDTH), lambda b, i: (b, 0)),
                  pl.BlockSpec((KV_WIDTH, n), lambda b, i: (0, b))],
        out_specs=pl.BlockSpec((tq, ATT_WIDTH), q_map),
        out_shape=jax.ShapeDtypeStruct((rows, ATT_WIDTH), BF16),
        scratch_shapes=[pltpu.VMEM((HEAD_DIM, m), F32), pltpu.VMEM((2, tk, m), F32)],
        compiler_params=_cparams("parallel", "arbitrary"),
        name="attention",
    )(q, k, vt, k, vt)


def _outproj_kernel(routed, lat_tiles, *refs):
    if routed:
        ra_ref, hg_ref, at_ref, xl_ref, xc_ref, mod_ref, w_ref, g_ref, rt_ref, xo_ref, h_ref, lg_ref = refs
    else:
        ra_ref, hg_ref, at_ref, xl_ref, xc_ref, mod_ref, w_ref, g_ref, xo_ref, h_ref = refs
    w1 = ra_ref.shape[1]
    w2 = w1 + hg_ref.shape[1]
    mix = (_dot(ra_ref[...], w_ref[0:w1, :]) + _dot(hg_ref[...], w_ref[w1:w2, :])
           + _dot(at_ref[...], w_ref[w2:, :]))
    m = mod_ref[0]
    x_in = jnp.where(pl.program_id(0) < lat_tiles, xl_ref[...], xc_ref[...])
    x = x_in + m[2:3] * mix
    xo_ref[...] = x
    y = x * lax.rsqrt(jnp.mean(x * x, axis=-1, keepdims=True) + EPS) * g_ref[...]
    h = y * (1.0 + m[4:5]) + m[3:4]
    if routed:
        h_ref[...] = h
        h_hi = h.astype(BF16)
        h_lo = (h - h_hi.astype(F32)).astype(BF16)
        lg = _dot(h_hi, rt_ref[...]) + _dot(h_lo, rt_ref[...])
        lg_ref[...] = lg[:, 0:LANES] + lg[:, LANES:2 * LANES]
    else:
        h_ref[...] = h.astype(BF16)


def _outproj(ra, hg, at, x_lat, x_ctx, ctx_base, mod, w, g, router, n_tok, n_lat, seq, nb):
    d = x_lat.shape[1]
    tm = 256
    routed = router is not None
    row = functools.partial(_mod_row, tile=tm, n_lat=n_lat, seq=seq, nb=nb)
    tok = lambda width: pl.BlockSpec((tm, width), lambda i: (i, 0))
    lat_spec, ctx_spec = _stream_specs(tm, d, n_lat // tm, ctx_base // tm)
    in_specs = [tok(ra.shape[1]), tok(hg.shape[1]), tok(at.shape[1]), lat_spec, ctx_spec,
                pl.BlockSpec((1, N_MOD, d), lambda i: (row(i), 0, 0)),
                pl.BlockSpec(w.shape, lambda i: (0, 0)),
                pl.BlockSpec((1, d), lambda i: (0, 0))]
    args = [ra, hg, at, x_lat, x_ctx, mod, w, g.reshape(1, d)]
    out_specs = [tok(d), tok(d)]
    out_shape = [jax.ShapeDtypeStruct((n_tok, d), F32), jax.ShapeDtypeStruct((n_tok, d), F32 if routed else BF16)]
    if routed:
        in_specs.append(pl.BlockSpec((d, 2 * LANES), lambda i: (0, 0)))
        args.append(router)
        out_specs.append(tok(LANES))
        out_shape.append(jax.ShapeDtypeStruct((n_tok, LANES), F32))
    return pl.pallas_call(
        functools.partial(_outproj_kernel, routed, n_lat // tm),
        grid=(n_tok // tm,),
        in_specs=in_specs,
        out_specs=out_specs,
        out_shape=out_shape,
        compiler_params=_cparams("parallel"),
        name="outproj",
    )(*args)


def _ffn_kernel(h_ref, x_ref, mod_ref, w1_ref, w3_ref, w2_ref, o_ref, acc_ref):
    k = pl.program_id(1)

    @pl.when(k == 0)
    def _():
        acc_ref[...] = jnp.zeros_like(acc_ref)

    h = h_ref[...]
    a = _dot(h, w1_ref[...])
    z = _silu(a) * _dot(h, w3_ref[...])
    acc_ref[...] += _dot(z.astype(BF16), w2_ref[...])

    @pl.when(k == pl.num_programs(1) - 1)
    def _():
        o_ref[...] = x_ref[...] + mod_ref[0][5:6] * acc_ref[...]


def _ffn(h, xs, mod, w1, w3, w2, n_lat, seq, nb):
    t, d = h.shape
    f = w1.shape[1]
    tm = TOKEN_TILE
    tf = 512
    row = functools.partial(_mod_row, tile=tm, n_lat=n_lat, seq=seq, nb=nb)
    return pl.pallas_call(
        _ffn_kernel,
        grid=(t // tm, f // tf),
        in_specs=[
            pl.BlockSpec((tm, d), lambda i, k: (i, 0)),
            pl.BlockSpec((tm, d), lambda i, k: (i, 0)),
            pl.BlockSpec((1, N_MOD, d), lambda i, k: (row(i), 0, 0)),
            pl.BlockSpec((d, tf), lambda i, k: (0, k)),
            pl.BlockSpec((d, tf), lambda i, k: (0, k)),
            pl.BlockSpec((tf, d), lambda i, k: (k, 0)),
        ],
        out_specs=pl.BlockSpec((tm, d), lambda i, k: (i, 0)),
        out_shape=jax.ShapeDtypeStruct((t, d), F32),
        scratch_shapes=[pltpu.VMEM((tm, d), F32)],
        compiler_params=_cparams("parallel", "arbitrary"),
        name="ffn_dense",
    )(h, xs, mod, w1, w3, w2)


def _router_kernel(lg_ref, meta_ref, wts_ref, cnt_ref, run_ref):
    @pl.when(pl.program_id(0) == 0)
    def _():
        run_ref[...] = jnp.zeros_like(run_ref)

    lg = lg_ref[...]
    tm = lg.shape[0]
    lane = lax.broadcasted_iota(jnp.int32, lg.shape, 1)
    lane_f = lane.astype(F32)
    v = jnp.where(lane < N_EXPERTS, lg, -jnp.inf)
    m1 = jnp.max(v, axis=-1, keepdims=True)
    i1 = jnp.min(jnp.where(v == m1, lane_f, float(LANES)), axis=-1, keepdims=True)
    v2 = jnp.where(lane_f == i1, -jnp.inf, v)
    m2 = jnp.max(v2, axis=-1, keepdims=True)
    i2 = jnp.min(jnp.where(v2 == m2, lane_f, float(LANES)), axis=-1, keepdims=True)
    e = jnp.exp(m2 - m1)
    wt1 = 1.0 / (1.0 + e)
    wt2 = e / (1.0 + e)
    hit1 = lane_f == i1
    hit2 = lane_f == i2
    assign = jnp.where(hit1 | hit2, 1.0, 0.0)
    r = lax.broadcasted_iota(jnp.int32, (tm, tm), 0)
    c = lax.broadcasted_iota(jnp.int32, (tm, tm), 1)
    tri = jnp.where(r > c, 1.0, 0.0).astype(BF16)
    rank = _dot(tri, assign.astype(BF16)) + run_ref[0:1, :]
    r1 = jnp.sum(jnp.where(hit1, rank, 0.0), axis=-1, keepdims=True)
    r2 = jnp.sum(jnp.where(hit2, rank, 0.0), axis=-1, keepdims=True)
    run_ref[...] = run_ref[...] + jnp.sum(assign, axis=0, keepdims=True)
    meta = jnp.where(lane == 0, i1, jnp.where(lane == 1, i2, jnp.where(lane == 2, r1, jnp.where(lane == 3, r2, 0.0))))
    meta_ref[...] = meta.astype(jnp.int32)
    wts_ref[...] = jnp.where(lane == 0, wt1, jnp.where(lane == 1, wt2, 0.0))
    cnt_ref[...] = run_ref[...]


def _router(logits):
    t = logits.shape[0]
    tm = TOKEN_TILE
    tok = pl.BlockSpec((tm, LANES), lambda i: (i, 0))
    return pl.pallas_call(
        _router_kernel,
        grid=(t // tm,),
        in_specs=[tok],
        out_specs=[tok, tok, pl.BlockSpec((SUBLANES, LANES), lambda i: (0, 0))],
        out_shape=[jax.ShapeDtypeStruct((t, LANES), jnp.int32), jax.ShapeDtypeStruct((t, LANES), F32),
                   jax.ShapeDtypeStruct((SUBLANES, LANES), F32)],
        scratch_shapes=[pltpu.VMEM((SUBLANES, LANES), F32)],
        compiler_params=_cparams("arbitrary"),
        name="router",
    )(logits)


def _row_copy(src_ref, src_row, dst_ref, dst_row, sem):
    return pltpu.make_async_copy(src_ref.at[pl.ds(src_row, 1)], dst_ref.at[pl.ds(dst_row, 1)], sem)


def _dispatch_kernel(pos_ref, h_ref, xs_in_ref, xs_ref, pos_smem, hbuf, sem_p, sem_h, sem_d):
    del xs_in_ref
    i = pl.program_id(0)
    n = pl.num_programs(0)
    tm = hbuf.shape[1]
    cur = i % 3

    def tile_copy(tile, buf):
        return pltpu.make_async_copy(h_ref.at[pl.ds(pl.multiple_of(tile * tm, tm), tm)], hbuf.at[buf], sem_h.at[buf])

    @pl.when(i == 0)
    def _():
        tile_copy(0, 0).start()

    @pl.when(i < n - 1)
    def _():
        tile_copy(i + 1, (i + 1) % 3).start()

    cp = pltpu.make_async_copy(pos_ref.at[0, 0], pos_smem, sem_p)
    cp.start()
    cp.wait()
    tile_copy(i, cur).wait()

    def issue(t, carry):
        _row_copy(hbuf.at[cur], t, xs_ref, pos_smem[t], sem_d.at[cur]).start()
        _row_copy(hbuf.at[cur], t, xs_ref, pos_smem[tm + t], sem_d.at[cur]).start()
        return carry

    lax.fori_loop(0, tm, issue, 0, unroll=8)

    def drain(buf):
        def body(t, carry):
            _row_copy(hbuf.at[buf], 0, xs_ref, 0, sem_d.at[buf]).wait()
            _row_copy(hbuf.at[buf], 0, xs_ref, 0, sem_d.at[buf]).wait()
            return carry
        lax.fori_loop(0, tm, body, 0, unroll=8)

    @pl.when(i > 0)
    def _():
        drain((i + 2) % 3)

    @pl.when(i == n - 1)
    def _():
        drain(cur)


def _dispatch(pos, h, xs0):
    t, d = h.shape
    tm = 256
    nt = t // tm
    return pl.pallas_call(
        _dispatch_kernel,
        grid=(nt,),
        in_specs=[
            pl.BlockSpec((1, 1, 2 * tm), lambda i: (i, 0, 0)),
            pl.BlockSpec(memory_space=pl.ANY),
            pl.BlockSpec(memory_space=pl.ANY),
        ],
        out_specs=pl.BlockSpec(memory_space=pl.ANY),
        out_shape=jax.ShapeDtypeStruct(xs0.shape, xs0.dtype),
        scratch_shapes=[pltpu.SMEM((2 * tm,), jnp.int32), pltpu.VMEM((3, tm, d), F32),
                        pltpu.SemaphoreType.DMA(()), pltpu.SemaphoreType.DMA((3,)), pltpu.SemaphoreType.DMA((3,))],
        input_output_aliases={2: 0},
        compiler_params=_cparams("arbitrary"),
        name="moe_dispatch",
    )(pos, h, xs0)


def _expert_kernel(te_ref, tv_ref, x_ref, w1_ref, w3_ref, w2_ref, y_ref, xb_ref, acc_ref):
    del te_ref
    i = pl.program_id(0)
    k = pl.program_id(1)

    @pl.when(tv_ref[i] > 0)
    def _():
        @pl.when(k == 0)
        def _():
            xb_ref[...] = x_ref[...].astype(BF16)
            acc_ref[...] = jnp.zeros_like(acc_ref)

        xb = xb_ref[...]
        a = _dot(xb, w1_ref[...])
        z = _silu(a) * _dot(xb, w3_ref[...])
        acc_ref[...] += _dot(z.astype(BF16), w2_ref[...])

        @pl.when(k == pl.num_programs(1) - 1)
        def _():
            y_ref[...] = acc_ref[...]

    @pl.when((tv_ref[i] == 0) & (k == 0))
    def _():
        y_ref[...] = jnp.zeros_like(y_ref)


def _experts(tile_expert, tile_valid, tile_row, xs, w1, w3, w2):
    rows, d = xs.shape
    f = w1.shape[2]
    tm = EXPERT_TILE
    tf = 512
    grid_spec = pltpu.PrefetchScalarGridSpec(
        num_scalar_prefetch=3,
        grid=(rows // tm, f // tf),
        in_specs=[
            pl.BlockSpec((tm, d), lambda i, k, te, tv, tr: (tr[i], 0)),
            pl.BlockSpec((None, d, tf), lambda i, k, te, tv, tr: (te[i], 0, jnp.where(tv[i] > 0, k, f // tf - 1))),
            pl.BlockSpec((None, d, tf), lambda i, k, te, tv, tr: (te[i], 0, jnp.where(tv[i] > 0, k, f // tf - 1))),
            pl.BlockSpec((None, tf, d), lambda i, k, te, tv, tr: (te[i], jnp.where(tv[i] > 0, k, f // tf - 1), 0)),
        ],
        out_specs=pl.BlockSpec((tm, d), lambda i, k, te, tv, tr: (i, 0)),
        scratch_shapes=[pltpu.VMEM((tm, d), BF16), pltpu.VMEM((tm, d), F32)],
    )

    def body(te_ref, tv_ref, tr_ref, *rest):
        del tr_ref
        _expert_kernel(te_ref, tv_ref, *rest)

    return pl.pallas_call(
        body,
        grid_spec=grid_spec,
        out_shape=jax.ShapeDtypeStruct((rows, d), F32),
        compiler_params=_cparams("arbitrary", "arbitrary"),
        name="moe_experts",
    )(tile_expert, tile_valid, tile_row, xs, w1, w3, w2)


def _combine_kernel(pos0_ref, posn_ref, wts_ref, x_ref, mod_ref, ys_ref, o_ref, pos_smem, y_ref, sem_p, sem_g):
    i = pl.program_id(0)
    n = pl.num_programs(0)
    tm = x_ref.shape[0]
    cur = i % 2

    def gather(pos_ref, slot):
        cp = pltpu.make_async_copy(pos_ref.at[0, 0], pos_smem, sem_p)
        cp.start()
        cp.wait()

        def issue(t, carry):
            _row_copy(ys_ref, pos_smem[t], y_ref.at[slot, 0], t, sem_g.at[slot]).start()
            _row_copy(ys_ref, pos_smem[tm + t], y_ref.at[slot, 1], t, sem_g.at[slot]).start()
            return carry

        lax.fori_loop(0, tm, issue, 0, unroll=8)

    @pl.when(i == 0)
    def _():
        gather(pos0_ref, 0)

    @pl.when(i < n - 1)
    def _():
        gather(posn_ref, 1 - cur)

    def drain(t, carry):
        _row_copy(ys_ref, 0, y_ref.at[cur, 0], 0, sem_g.at[cur]).wait()
        _row_copy(ys_ref, 0, y_ref.at[cur, 1], 0, sem_g.at[cur]).wait()
        return carry

    lax.fori_loop(0, tm, drain, 0, unroll=8)
    wts = wts_ref[...]
    f = wts[:, 0:1] * y_ref[cur, 0] + wts[:, 1:2] * y_ref[cur, 1]
    o_ref[...] = x_ref[...] + mod_ref[0][5:6] * f


def _combine(pos, wts, xs, mod, ys, n_lat, seq, nb):
    t, d = xs.shape
    tm = 256
    row = functools.partial(_mod_row, tile=tm, n_lat=n_lat, seq=seq, nb=nb)
    return pl.pallas_call(
        _combine_kernel,
        grid=(t // tm,),
        in_specs=[
            pl.BlockSpec((1, 1, 2 * tm), lambda i: (0, 0, 0)),
            pl.BlockSpec((1, 1, 2 * tm), lambda i: (jnp.minimum(i + 1, t // tm - 1), 0, 0)),
            pl.BlockSpec((tm, LANES), lambda i: (i, 0)),
            pl.BlockSpec((tm, d), lambda i: (i, 0)),
            pl.BlockSpec((1, N_MOD, d), lambda i: (row(i), 0, 0)),
            pl.BlockSpec(memory_space=pl.ANY),
        ],
        out_specs=pl.BlockSpec((tm, d), lambda i: (i, 0)),
        out_shape=jax.ShapeDtypeStruct((t, d), F32),
        scratch_shapes=[pltpu.SMEM((2 * tm,), jnp.int32), pltpu.VMEM((2, 2, tm, d), F32),
                        pltpu.SemaphoreType.DMA(()), pltpu.SemaphoreType.DMA((2,))],
        compiler_params=_cparams("arbitrary"),
        name="moe_combine",
    )(pos, pos, wts, xs, mod, ys)


def _moe(h, logits, xs, mod, w1, w3, w2, n_lat, seq, nb):
    t, d = h.shape
    meta, wts, counts = _router(logits)
    te_rows = EXPERT_TILE
    cnt = counts[0, :N_EXPERTS].astype(jnp.int32)
    padded = ((cnt + te_rows - 1) // te_rows) * te_rows
    ends = jnp.cumsum(padded)
    starts = ends - padded
    pos1 = jnp.take(starts, meta[:, 0]) + meta[:, 2]
    pos2 = jnp.take(starts, meta[:, 1]) + meta[:, 3]
    tmd = 256
    pos = jnp.concatenate([pos1.reshape(t // tmd, 1, tmd), pos2.reshape(t // tmd, 1, tmd)], axis=-1)
    n_tiles = (2 * t) // te_rows + N_EXPERTS
    tile_start = jnp.arange(n_tiles, dtype=jnp.int32) * te_rows
    n_valid = ends[-1] // te_rows
    tile_valid = (tile_start < ends[-1]).astype(jnp.int32)
    tile_row = jnp.minimum(jnp.arange(n_tiles, dtype=jnp.int32), n_valid - 1)
    tile_expert = jnp.sum((tile_row[:, None] * te_rows >= ends[None, :]).astype(jnp.int32), axis=1)
    xs0 = jnp.zeros((n_tiles * te_rows, d), F32)
    x_sorted = _dispatch(pos, h, xs0)
    y_sorted = _experts(tile_expert, tile_valid, tile_row, x_sorted, w1, w3, w2)
    return _combine(pos, wts, xs, mod, y_sorted, n_lat, seq, nb)


def kernel(x, c, ctx, c_ctx, w_mod, b_mod, norm1, norm2, w_in, w_out, conv_w, conv_b, lru_wa, lru_ba, lru_wi, lru_bi, lru_lambda, hgrn_lb_logits, hgrn_gnorm, q_norm, k_norm, ffn_w1, ffn_w3, ffn_w2, router, moe_w1, moe_w3, moe_w2):
    nb, n, d = x.shape
    nc = ctx.shape[1]
    depth = w_mod.shape[0]
    assert n % SEQ_CHUNK == 0 and nc % SEQ_CHUNK == 0 and n % TOKEN_TILE == 0
    assert (nb * nc) % TOKEN_TILE == 0 and n % GRID_W == 0
    n_lat = nb * n
    nlc = n // SEQ_CHUNK
    ncc = nc // SEQ_CHUNK

    n_all = n_lat + nb * nc
    x_lat, x_ctx, ctx_base = x.reshape(n_lat, d), ctx.reshape(nb * nc, d), 0
    mod_rows = 2 * SUBLANES * ((nb + 1 + 2 * SUBLANES - 1) // (2 * SUBLANES))
    cc = jnp.zeros((mod_rows, d), F32).at[:nb].set(c).at[nb].set(c_ctx)
    mod_all = _mod_table(cc, w_mod, b_mod).reshape(depth, mod_rows, N_MOD, d)
    cos, sin = _rope_tables(n)
    router_f = jnp.pad(router, ((0, 0), (0, 0), (0, LANES - router.shape[-1])))
    router_hi = router_f.astype(BF16)
    router_lo = (router_f - router_hi.astype(F32)).astype(BF16)
    router_p = jnp.concatenate([router_hi, router_lo], axis=-1)

    for l in range(depth):
        need_ctx = l < depth - 1
        mod = mod_all[l]
        u = _inproj(x_lat, x_ctx, ctx_base, n_all, mod, norm1[l], _to_bf16(w_in, l), n_lat, n, nb)

        lru_args = lambda dd: (conv_w[l], conv_b[l], lru_wa[l, dd], lru_ba[l, dd], lru_wi[l, dd], lru_bi[l, dd],
                               lru_lambda[l, dd])
        hf = _lru_pass(False, u, *lru_args(0), None, nb, ncc, nlc)
        ra = _lru_pass(True, u, *lru_args(1), hf, nb, ncc, nlc)

        of = _hgrn_pass(False, l, u, hgrn_lb_logits[0], hgrn_gnorm[l], None, nb, ncc, nlc)
        hg = _hgrn_pass(True, l, u, hgrn_lb_logits[1], hgrn_gnorm[l], of, nb, ncc, nlc)

        q, k, vt = _qkv(u, cos, sin, q_norm[l], k_norm[l], n_lat, nlc)
        at = _attn(q, k, vt, nb, n, nc, need_ctx)

        n_tok = n_all if need_ctx else n_lat
        j = l // 2
        routed = l % 2 == 1
        outs = _outproj(ra, hg, at, x_lat, x_ctx, ctx_base, mod, _to_bf16(w_out, l), norm2[l],
                        router_p[j] if routed else None, n_tok, n_lat, n, nb)
        if routed:
            xn, h2, logits = outs
            xs = _moe(h2, logits, xn, mod, _to_bf16(moe_w1, j), _to_bf16(moe_w3, j), _to_bf16(moe_w2, j),
                      n_lat, n, nb)
        else:
            xn, h2 = outs
            xs = _ffn(h2, xn, mod, _to_bf16(ffn_w1, j), _to_bf16(ffn_w3, j), _to_bf16(ffn_w2, j), n_lat, n, nb)
        x_lat, x_ctx, ctx_base = xs, xs, n_lat
    return xs[:n_lat].reshape(nb, n, d)
```

```python
import functools

import jax
import jax.numpy as jnp
from jax import lax
from jax.experimental import pallas as pl
from jax.experimental.pallas import tpu as pltpu

F32 = jnp.float32
BF16 = jnp.bfloat16

EPS = 1e-6
N_MOD = 6
GRID_W = 64
LRU_WIDTH = 512
LRU_BLOCKS = 4
LRU_BLOCK = LRU_WIDTH // LRU_BLOCKS
LRU_C = 8.0
HG_HEADS = 4
HG_DK = 128
HG_WIDTH = HG_HEADS * HG_DK
HG_CHUNK = 64
HG_SUB = 8
HEAD_DIM = 128
N_Q_HEADS = 8
N_KV_HEADS = 2
Q_PER_KV = N_Q_HEADS // N_KV_HEADS
ATT_WIDTH = N_Q_HEADS * HEAD_DIM
KV_WIDTH = N_KV_HEADS * HEAD_DIM
ROPE_THETA = 10000.0
ROPE_FREQS = HEAD_DIM // 4
ATTN_SCALE = HEAD_DIM ** -0.5
LOG2E = 1.4426950408889634
Q_SCALE = ATTN_SCALE * LOG2E
A_IN = 2 * LRU_WIDTH
B_IN = 5 * HG_WIDTH
C_IN = ATT_WIDTH + 2 * KV_WIDTH
IN_WIDTH = A_IN + B_IN + C_IN
N_EXPERTS = 8

LANES = 128
SUBLANES = 8
SEQ_CHUNK = 256
TOKEN_TILE = 512
INPROJ_FLAT_TILE = 1024
EXPERT_TILE = 512
ATT_KEY_TILE = 1024
ATT_QUERY_TILE = 256
CAST_BLOCK_ELEMS = 1024 * 1024
VMEM_LIMIT = 56 * 1024 * 1024
NEG_BIG = -1e30


def _cparams(*sem):
    return pltpu.CompilerParams(dimension_semantics=sem, vmem_limit_bytes=VMEM_LIMIT)


def _sigmoid(v):
    return 1.0 / (1.0 + jnp.exp(-v))


def _silu(v):
    return v * _sigmoid(v)


def _dot(a, b):
    return jnp.dot(a, b, preferred_element_type=F32)


def _dot_nt(a, b):
    return lax.dot_general(a, b, (((1,), (1,)), ((), ())), preferred_element_type=F32)


def _dot_tn(a, b):
    return lax.dot_general(a, b, (((0,), (0,)), ((), ())), preferred_element_type=F32)


def _mod_row(i, tile, n_lat, seq, nb):
    return jnp.where(i < n_lat // tile, (i * tile) // seq, nb)


def _cast_kernel(w_ref, o_ref):
    o_ref[...] = w_ref[...].astype(BF16)


def _to_bf16(w, index):
    inner = w.shape[1:]
    cols = inner[-1]
    rows = 1
    for s in inner[:-1]:
        rows *= s
    tr = 2 * SUBLANES
    while tr * 2 * cols <= CAST_BLOCK_ELEMS and rows % (tr * 2) == 0:
        tr *= 2
    assert rows % tr == 0
    steps = rows // tr
    out = pl.pallas_call(
        _cast_kernel,
        grid=(steps,),
        in_specs=[pl.BlockSpec((tr, cols), lambda i: (index * steps + i, 0))],
        out_specs=pl.BlockSpec((tr, cols), lambda i: (i, 0)),
        out_shape=jax.ShapeDtypeStruct((rows, cols), BF16),
        compiler_params=_cparams("parallel"),
        name="to_bf16",
    )(w.reshape(w.shape[0] * rows, cols))
    return out.reshape(inner)


def _mod_kernel(c_ref, w_ref, b_ref, o_ref):
    s = _silu(c_ref[...]).astype(BF16)
    o_ref[0] = _dot(s, w_ref[0].astype(BF16)) + b_ref[0]


def _mod_table(cc, w_mod, b_mod):
    nl, d, md = w_mod.shape
    rows = cc.shape[0]
    tn = 1024 if md % 1024 == 0 else md
    return pl.pallas_call(
        _mod_kernel,
        grid=(nl, md // tn),
        in_specs=[
            pl.BlockSpec((rows, d), lambda l, j: (0, 0)),
            pl.BlockSpec((1, d, tn), lambda l, j: (l, 0, j)),
            pl.BlockSpec((1, 1, tn), lambda l, j: (l, 0, j)),
        ],
        out_specs=pl.BlockSpec((1, rows, tn), lambda l, j: (l, 0, j)),
        out_shape=jax.ShapeDtypeStruct((nl, rows, md), F32),
        compiler_params=_cparams("parallel", "parallel"),
        name="mod_table",
    )(cc, w_mod, b_mod.reshape(nl, 1, md))


def _stream_specs(tm, d, lat_tiles, ctx_base):
    lat = pl.BlockSpec((tm, d), lambda i, *_: (jnp.minimum(i, lat_tiles - 1), 0))
    ctx = pl.BlockSpec((tm, d), lambda i, *_: (ctx_base + jnp.maximum(i - lat_tiles, 0), 0))
    return lat, ctx


def _inproj_kernel(lat_tiles, *refs):
    if lat_tiles is None:
        x_ref, mod_ref, g_ref, w_ref, o_ref, h_ref = refs
    else:
        xl_ref, xc_ref, mod_ref, g_ref, w_ref, o_ref, h_ref = refs
    i = pl.program_id(0)
    first = pl.program_id(1) == 0

    def norm_mod(x_ref):
        x = x_ref[...]
        y = x * lax.rsqrt(jnp.mean(x * x, axis=-1, keepdims=True) + EPS) * g_ref[...]
        m = mod_ref[0]
        h_ref[...] = (y * (1.0 + m[1:2]) + m[0:1]).astype(BF16)

    if lat_tiles is None:
        @pl.when(first)
        def _():
            norm_mod(x_ref)
    else:
        @pl.when(first & (i < lat_tiles))
        def _():
            norm_mod(xl_ref)

        @pl.when(first & (i >= lat_tiles))
        def _():
            norm_mod(xc_ref)

    o_ref[...] = _dot(h_ref[...], w_ref[...])


def _inproj(x_lat, x_ctx, ctx_base, n_tok, mod, g, w, n_lat, seq, nb):
    d = x_lat.shape[1]
    width = w.shape[1]
    tn = 1024
    flat = x_lat is x_ctx
    if flat:
        tm = INPROJ_FLAT_TILE if (seq % INPROJ_FLAT_TILE == 0 and (n_tok - n_lat) % INPROJ_FLAT_TILE == 0) \
            else TOKEN_TILE
        x_specs = [pl.BlockSpec((tm, d), lambda i, j: (i, 0))]
        x_args = [x_lat]
    else:
        tm = TOKEN_TILE
        x_specs = list(_stream_specs(tm, d, n_lat // tm, ctx_base // tm))
        x_args = [x_lat, x_ctx]
    row = functools.partial(_mod_row, tile=tm, n_lat=n_lat, seq=seq, nb=nb)
    return pl.pallas_call(
        functools.partial(_inproj_kernel, None if flat else n_lat // tm),
        grid=(n_tok // tm, width // tn),
        in_specs=x_specs + [
            pl.BlockSpec((1, N_MOD, d), lambda i, j: (row(i), 0, 0)),
            pl.BlockSpec((1, d), lambda i, j: (0, 0)),
            pl.BlockSpec((d, tn), lambda i, j: (0, j)),
        ],
        out_specs=pl.BlockSpec((tm, tn), lambda i, j: (i, j)),
        out_shape=jax.ShapeDtypeStruct((n_tok, width), F32),
        scratch_shapes=[pltpu.VMEM((tm, d), BF16)],
        compiler_params=_cparams("parallel", "arbitrary"),
        name="inproj",
    )(*x_args, mod, g.reshape(1, d), w)


def _seq_pos(j, rev, ncc, nlc):
    is_ctx = j < ncc
    jl = j - ncc
    cc = (ncc - 1 - j) if rev else j
    cl = (nlc - 1 - jl) if rev else jl
    return is_ctx, cc, cl


def _seq_blk(b, j, rev, ncc, nlc, nb):
    is_ctx, cc, cl = _seq_pos(j, rev, ncc, nlc)
    return jnp.where(is_ctx, nb * nlc + b * ncc + cc, b * nlc + cl)


def _scan_rows(a_ref, b_ref, h_ref, hc_ref, rev):
    n_groups = a_ref.shape[0] // SUBLANES
    width = a_ref.shape[1]
    rows = lax.broadcasted_iota(jnp.int32, (SUBLANES, width), 0)

    def body(g, hc):
        gi = (n_groups - 1 - g) if rev else g
        r0 = pl.multiple_of(gi * SUBLANES, SUBLANES)
        a = a_ref[pl.ds(r0, SUBLANES), :]
        b = b_ref[pl.ds(r0, SUBLANES), :]
        for s in (1, 2, 4):
            if rev:
                a_s = pltpu.roll(a, SUBLANES - s, 0)
                b_s = pltpu.roll(b, SUBLANES - s, 0)
                keep = rows < SUBLANES - s
            else:
                a_s = pltpu.roll(a, s, 0)
                b_s = pltpu.roll(b, s, 0)
                keep = rows >= s
            b = a * jnp.where(keep, b_s, 0.0) + b
            a = a * jnp.where(keep, a_s, 1.0)
        h = b + a * hc
        h_ref[pl.ds(r0, SUBLANES), :] = h
        return h[0:1] if rev else h[SUBLANES - 1:SUBLANES]

    hc_ref[...] = lax.fori_loop(0, n_groups, body, hc_ref[...])


def _lru_kernel(rev, ncc, nlc, *refs):
    if rev:
        (u_ref, prev_ref, next_ref, cw_ref, cb_ref, wa_ref, ba_ref, wi_ref, bi_ref, lam_ref,
         hf_ref, o_ref, xe_ref, a_ref, b_ref, h_ref, hc_ref) = refs
    else:
        (u_ref, prev_ref, next_ref, cw_ref, cb_ref, wa_ref, ba_ref, wi_ref, bi_ref, lam_ref,
         o_ref, xe_ref, a_ref, b_ref, hc_ref) = refs
        h_ref = o_ref
    j = pl.program_id(1)
    is_ctx, cc, cl = _seq_pos(j, rev, ncc, nlc)
    c = jnp.where(is_ctx, cc, cl)
    nseq = jnp.where(is_ctx, ncc, nlc)

    @pl.when(j == 0)
    def _():
        hc_ref[...] = jnp.zeros_like(hc_ref)

    ch = SEQ_CHUNK
    w = LRU_WIDTH
    xe_ref[SUBLANES:SUBLANES + ch, :] = u_ref[:, w:2 * w]
    xe_ref[0:SUBLANES, :] = jnp.where(c > 0, prev_ref[...], 0.0)
    xe_ref[SUBLANES + ch:2 * SUBLANES + ch, :] = jnp.where(c < nseq - 1, next_ref[...], 0.0)
    cw = cw_ref[...]
    xc = cb_ref[...]
    for tap in range(4):
        off = SUBLANES - 2 + tap
        xc = xc + cw[tap:tap + 1] * xe_ref[off:off + ch, :]
    xb = xc.astype(BF16)
    ra = jnp.concatenate(
        [_dot(xb[:, n * LRU_BLOCK:(n + 1) * LRU_BLOCK], wa_ref[n]) for n in range(LRU_BLOCKS)], axis=-1)
    ri = jnp.concatenate(
        [_dot(xb[:, n * LRU_BLOCK:(n + 1) * LRU_BLOCK], wi_ref[n]) for n in range(LRU_BLOCKS)], axis=-1)
    r = _sigmoid(ra + ba_ref[...])
    gate_i = _sigmoid(ri + bi_ref[...])
    neg_lam = -lam_ref[...]
    softplus = jnp.maximum(neg_lam, 0.0) + jnp.log1p(jnp.exp(-jnp.abs(neg_lam)))
    log_a = -LRU_C * r * softplus
    a = jnp.exp(log_a)
    a_ref[...] = a
    b_ref[...] = jnp.sqrt(-jnp.tanh(log_a) * (a * a + 1.0)) * (gate_i * xc)
    _scan_rows(a_ref, b_ref, h_ref, hc_ref, rev)
    if rev:
        y = u_ref[:, 0:w]
        gelu = 0.5 * y * (1.0 + jnp.tanh(0.7978845608028654 * (y + 0.044715 * (y * y * y))))
        o_ref[...] = (gelu * (hf_ref[...] + h_ref[...])).astype(BF16)


def _lru_pass(rev, u, cw, cb, wa, ba, wi, bi, lam, hf, nb, ncc, nlc):
    t = u.shape[0]
    ch = SEQ_CHUNK
    w = LRU_WIDTH
    r8 = ch // SUBLANES
    blk = functools.partial(_seq_blk, rev=rev, ncc=ncc, nlc=nlc, nb=nb)
    full2 = lambda b, j: (0, 0)
    in_specs = [
        pl.BlockSpec((ch, 2 * w), lambda b, j: (blk(b, j), 0)),
        pl.BlockSpec((SUBLANES, w), lambda b, j: (jnp.maximum(blk(b, j) * r8 - 1, 0), 1)),
        pl.BlockSpec((SUBLANES, w), lambda b, j: (jnp.minimum(blk(b, j) * r8 + r8, t // SUBLANES - 1), 1)),
        pl.BlockSpec((4, w), full2),
        pl.BlockSpec((1, w), full2),
        pl.BlockSpec((LRU_BLOCKS, LRU_BLOCK, LRU_BLOCK), lambda b, j: (0, 0, 0)),
        pl.BlockSpec((1, w), full2),
        pl.BlockSpec((LRU_BLOCKS, LRU_BLOCK, LRU_BLOCK), lambda b, j: (0, 0, 0)),
        pl.BlockSpec((1, w), full2),
        pl.BlockSpec((1, w), full2),
    ]
    args = [u, u, u, cw, cb.reshape(1, w), wa.astype(BF16), ba.reshape(1, w), wi.astype(BF16),
            bi.reshape(1, w), lam.reshape(1, w)]
    scratch = [pltpu.VMEM((ch + 2 * SUBLANES, w), F32), pltpu.VMEM((ch, w), F32), pltpu.VMEM((ch, w), F32)]
    if rev:
        in_specs.append(pl.BlockSpec((ch, w), lambda b, j: (blk(b, j), 0)))
        args.append(hf)
        scratch.append(pltpu.VMEM((ch, w), F32))
        out_dtype = BF16
    else:
        out_dtype = F32
    scratch.append(pltpu.VMEM((1, w), F32))
    return pl.pallas_call(
        functools.partial(_lru_kernel, rev, ncc, nlc),
        grid=(nb, ncc + nlc),
        in_specs=in_specs,
        out_specs=pl.BlockSpec((ch, w), lambda b, j: (blk(b, j), 0)),
        out_shape=jax.ShapeDtypeStruct((t, w), out_dtype),
        scratch_shapes=scratch,
        compiler_params=_cparams("parallel", "arbitrary"),
        name="lru_bwd" if rev else "lru_fwd",
    )(*args)


def _cumsum_rows(v, rev):
    n = v.shape[0]
    rows = lax.broadcasted_iota(jnp.int32, v.shape, 0)
    s = 1
    while s < n:
        if rev:
            v = v + jnp.where(rows < n - s, pltpu.roll(v, n - s, 0), 0.0)
        else:
            v = v + jnp.where(rows >= s, pltpu.roll(v, s, 0), 0.0)
        s *= 2
    return v


def _hgrn_diag_att(q, k, bc, lo, width, rev):
    n = q.shape[0]
    lane = lax.broadcasted_iota(jnp.int32, (n, width), 1)
    row = lax.broadcasted_iota(jnp.int32, (n, width), 0)
    att = jnp.zeros((n, width), F32)
    for s in range(n):
        w = q * (k[s:s + 1] * jnp.exp2(bc - bc[s:s + 1]))
        att = jnp.where(lane == lo + s, jnp.sum(w, axis=-1, keepdims=True), att)
    keep = (row + lo <= lane) if rev else (row + lo >= lane)
    return jnp.where(keep, att, 0.0)


def _hgrn_chunk(qraw, fl, v, lbh, st, rev):
    cs = HG_CHUNK
    sb = HG_SUB
    q = _silu(qraw)
    f = lbh + (1.0 - lbh) * _sigmoid(fl)
    k = 1.0 - f
    bc = _cumsum_rows(jnp.log(f) * LOG2E, rev)
    blast = bc[0:1] if rev else bc[cs - 1:cs]
    o = _dot_nt((q * jnp.exp2(bc)).astype(BF16), st.astype(BF16))
    kdec = k * jnp.exp2(blast - bc)
    st_new = st * jnp.exp2(blast) + _dot_tn(v.astype(BF16), kdec.astype(BF16))
    att_rows = []
    for i in range(cs // sb):
        lo, hi = i * sb, (i + 1) * sb
        att_i = _hgrn_diag_att(q[lo:hi], k[lo:hi], bc[lo:hi], lo, cs, rev)
        if (hi < cs) if rev else (lo > 0):
            mref = bc[hi:hi + 1] if rev else bc[lo - 1:lo]
            plo, phi = (hi, cs) if rev else (0, lo)
            qhat = q[lo:hi] * jnp.exp2(bc[lo:hi] - mref)
            kp = k[plo:phi] * jnp.exp2(mref - bc[plo:phi])
            pad = jnp.zeros((cs - (phi - plo), HG_DK), F32)
            kp = jnp.concatenate([pad, kp] if rev else [kp, pad], axis=0)
            att_i = att_i + _dot_nt(qhat.astype(BF16), kp.astype(BF16))
        att_rows.append(att_i)
    att = jnp.concatenate(att_rows, axis=0)
    return o + _dot(att.astype(BF16), v.astype(BF16)), st_new


def _hgrn_kernel(rev, layer, *refs):
    if rev:
        q_ref, f_ref, v_ref, lbl_ref, og_ref, of_ref, gn_ref, o_ref, st_ref = refs
    else:
        q_ref, f_ref, v_ref, lbl_ref, o_ref, st_ref = refs

    @pl.when(pl.program_id(1) == 0)
    def _():
        st_ref[...] = jnp.zeros_like(st_ref)

    lg = lbl_ref[...]
    e = jnp.exp(lg - jnp.max(lg, axis=0, keepdims=True))
    p = e / jnp.sum(e, axis=0, keepdims=True)
    lb = jnp.zeros_like(p[0:1])
    for jl in range(1, layer + 1):
        lb = lb + p[jl:jl + 1]

    cs = HG_CHUNK
    n_chunks = q_ref.shape[0] // cs

    def body(ci, carry):
        cidx = (n_chunks - 1 - ci) if rev else ci
        r0 = pl.multiple_of(cidx * cs, cs)
        for h in range(HG_HEADS):
            cols = slice(h * HG_DK, (h + 1) * HG_DK)
            o, st_new = _hgrn_chunk(q_ref[pl.ds(r0, cs), cols], f_ref[pl.ds(r0, cs), cols],
                                    v_ref[pl.ds(r0, cs), cols], lb[:, cols], st_ref[h], rev)
            st_ref[h] = st_new
            if rev:
                o = o + of_ref[pl.ds(r0, cs), cols]
                y = o * lax.rsqrt(jnp.mean(o * o, axis=-1, keepdims=True) + EPS) * gn_ref[...]
                o_ref[pl.ds(r0, cs), cols] = (y * _silu(og_ref[pl.ds(r0, cs), cols])).astype(BF16)
            else:
                o_ref[pl.ds(r0, cs), cols] = o
        return carry

    lax.fori_loop(0, n_chunks, body, 0, unroll=4)


def _hgrn_pass(rev, layer, u, lb_logits, gnorm, of, nb, ncc, nlc):
    t = u.shape[0]
    ch = SEQ_CHUNK
    w = HG_WIDTH
    nl = lb_logits.shape[0]
    blk = functools.partial(_seq_blk, rev=rev, ncc=ncc, nlc=nlc, nb=nb)
    base = A_IN // w
    col = lambda cb: pl.BlockSpec((ch, w), lambda b, j: (blk(b, j), cb))
    in_specs = [col(base), col(base + 1 + (1 if rev else 0)), col(base + 3),
                pl.BlockSpec((nl, w), lambda b, j: (0, 0))]
    args = [u, u, u, lb_logits]
    if rev:
        in_specs += [col(base + 4), pl.BlockSpec((ch, w), lambda b, j: (blk(b, j), 0)),
                     pl.BlockSpec((1, HG_DK), lambda b, j: (0, 0))]
        args += [u, of, gnorm.reshape(1, HG_DK)]
    return pl.pallas_call(
        functools.partial(_hgrn_kernel, rev, layer),
        grid=(nb, ncc + nlc),
        in_specs=in_specs,
        out_specs=pl.BlockSpec((ch, w), lambda b, j: (blk(b, j), 0)),
        out_shape=jax.ShapeDtypeStruct((t, w), BF16 if rev else F32),
        scratch_shapes=[pltpu.VMEM((HG_HEADS, HG_DK, HG_DK), F32)],
        compiler_params=_cparams("parallel", "arbitrary"),
        name="hgrn_bwd" if rev else "hgrn_fwd",
    )(*args)


def _rope_tables(n):
    rows = n // GRID_W
    row = jnp.repeat(jnp.arange(rows, dtype=F32), GRID_W)
    colp = jnp.tile(jnp.arange(GRID_W, dtype=F32), rows)
    inv = ROPE_THETA ** (-jnp.arange(ROPE_FREQS, dtype=F32) / ROPE_FREQS)
    ar = row[:, None] * inv
    ac = colp[:, None] * inv
    cos = jnp.concatenate([jnp.cos(ar), jnp.cos(ar), jnp.cos(ac), jnp.cos(ac)], axis=-1)
    sin = jnp.concatenate([-jnp.sin(ar), jnp.sin(ar), -jnp.sin(ac), jnp.sin(ac)], axis=-1)
    cos = jnp.concatenate([cos, jnp.ones((SEQ_CHUNK, HEAD_DIM), F32)], axis=0)
    sin = jnp.concatenate([sin, jnp.zeros((SEQ_CHUNK, HEAD_DIM), F32)], axis=0)
    return cos, sin


def _qkv_kernel(qa_ref, qb_ref, kv_ref, cos_ref, sin_ref, qn_ref, kn_ref, q_out, k_out, vt_out):
    cos = cos_ref[...]
    sin = sin_ref[...]
    lane = lax.broadcasted_iota(jnp.int32, cos.shape, 1)
    first = (lane % (2 * ROPE_FREQS)) < ROPE_FREQS

    def norm_rope(v, g, scale):
        y = v * lax.rsqrt(jnp.mean(v * v, axis=-1, keepdims=True) + EPS) * g
        partner = jnp.where(first, pltpu.roll(y, HEAD_DIM - ROPE_FREQS, 1), pltpu.roll(y, ROPE_FREQS, 1))
        out = y * cos + partner * sin
        if scale is not None:
            out = out * scale
        return out.astype(BF16)

    half = N_Q_HEADS // 2
    for h in range(half):
        cols = slice(h * HEAD_DIM, (h + 1) * HEAD_DIM)
        q_out[:, cols] = norm_rope(qa_ref[:, cols], qn_ref[...], Q_SCALE)
        cols_b = slice((half + h) * HEAD_DIM, (half + h + 1) * HEAD_DIM)
        q_out[:, cols_b] = norm_rope(qb_ref[:, cols], qn_ref[...], Q_SCALE)
    for h in range(N_KV_HEADS):
        cols = slice(h * HEAD_DIM, (h + 1) * HEAD_DIM)
        k_out[:, cols] = norm_rope(kv_ref[:, cols], kn_ref[...], None)
        vt_out[cols, :] = kv_ref[:, KV_WIDTH + h * HEAD_DIM:KV_WIDTH + (h + 1) * HEAD_DIM].T.astype(BF16)


def _qkv(u, cos, sin, qn, kn, n_lat, nlc):
    t = u.shape[0]
    ch = SEQ_CHUNK
    cw = 512
    base = (A_IN + B_IN) // cw
    tab = lambda i: (jnp.where(i < n_lat // ch, i % nlc, nlc), 0)
    return pl.pallas_call(
        _qkv_kernel,
        grid=(t // ch,),
        in_specs=[
            pl.BlockSpec((ch, cw), lambda i: (i, base)),
            pl.BlockSpec((ch, cw), lambda i: (i, base + 1)),
            pl.BlockSpec((ch, cw), lambda i: (i, base + 2)),
            pl.BlockSpec((ch, HEAD_DIM), tab),
            pl.BlockSpec((ch, HEAD_DIM), tab),
            pl.BlockSpec((1, HEAD_DIM), lambda i: (0, 0)),
            pl.BlockSpec((1, HEAD_DIM), lambda i: (0, 0)),
        ],
        out_specs=[
            pl.BlockSpec((ch, ATT_WIDTH), lambda i: (i, 0)),
            pl.BlockSpec((ch, KV_WIDTH), lambda i: (i, 0)),
            pl.BlockSpec((KV_WIDTH, ch), lambda i: (0, i)),
        ],
        out_shape=[
            jax.ShapeDtypeStruct((t, ATT_WIDTH), BF16),
            jax.ShapeDtypeStruct((t, KV_WIDTH), BF16),
            jax.ShapeDtypeStruct((KV_WIDTH, t), BF16),
        ],
        compiler_params=_cparams("parallel"),
        name="qkv_prep",
    )(u, u, u, cos, sin, qn.reshape(1, HEAD_DIM), kn.reshape(1, HEAD_DIM))


def _row_groups(v, op):
    return op(v.reshape(v.shape[0] // SUBLANES, SUBLANES, v.shape[1]), axis=0)


def _attn_body(with_lat, q_ref, kc_ref, vtc_ref, kl_ref, vtl_ref, o_ref, acc_ref, s_ref):
    tq = q_ref.shape[0]
    tk = s_ref.shape[1]
    n_tiles = kl_ref.shape[0] // tk if with_lat else 0

    for g in range(N_KV_HEADS):
        gcols = slice(g * HEAD_DIM, (g + 1) * HEAD_DIM)
        qs = jnp.concatenate(
            [q_ref[:, (g * Q_PER_KV + h) * HEAD_DIM:(g * Q_PER_KV + h + 1) * HEAD_DIM] for h in range(Q_PER_KV)],
            axis=0)

        def softmax_pv(s, vtt, carry):
            m_new = jnp.max(_row_groups(s, jnp.max), axis=0, keepdims=True)
            if carry is not None:
                m_new = jnp.maximum(carry[0], m_new)
            p = jnp.exp2(s - m_new)
            pv = _dot(vtt, p.astype(BF16))
            if carry is None:
                acc_ref[...] = pv
                return m_new, _row_groups(p, jnp.sum)
            alpha = jnp.exp2(carry[0] - m_new)
            acc_ref[...] = alpha * acc_ref[...] + pv
            return m_new, alpha * carry[1] + _row_groups(p, jnp.sum)

        def scores(t):
            r0 = t * tk if isinstance(t, int) else pl.multiple_of(t * tk, tk)
            return _dot_nt(kl_ref[pl.ds(r0, tk), gcols], qs)

        def consume(slot, t, carry):
            r0 = t * tk if isinstance(t, int) else pl.multiple_of(t * tk, tk)
            return softmax_pv(s_ref[slot], vtl_ref[gcols, pl.ds(r0, tk)], carry)

        if n_tiles:
            s_ref[0] = scores(0)
        carry = softmax_pv(_dot_nt(kc_ref[:, gcols], qs), vtc_ref[gcols, :], None)
        n_pairs = max(n_tiles // 2 - 1, 0)
        if n_pairs:
            def pair(j, carry):
                t0 = 2 * j
                s_ref[1] = scores(t0 + 1)
                carry = consume(0, t0, carry)
                s_ref[0] = scores(t0 + 2)
                return consume(1, t0 + 1, carry)
            carry = lax.fori_loop(0, n_pairs, pair, carry)
        for t in range(2 * n_pairs, n_tiles):
            if t + 1 < n_tiles:
                s_ref[(t + 1) % 2] = scores(t + 1)
            carry = consume(t % 2, t, carry)
        ot = acc_ref[...] * (1.0 / jnp.sum(carry[1], axis=0, keepdims=True))
        for h in range(Q_PER_KV):
            cols = slice((g * Q_PER_KV + h) * HEAD_DIM, (g * Q_PER_KV + h + 1) * HEAD_DIM)
            o_ref[:, cols] = ot[:, h * tq:(h + 1) * tq].T.astype(BF16)


def _attn_kernel(lat_steps, ctx_steps, *refs):
    if ctx_steps == 0:
        _attn_body(True, *refs)
        return
    i = pl.program_id(1)

    @pl.when(i < lat_steps)
    def _():
        _attn_body(True, *refs)

    @pl.when(i >= lat_steps)
    def _():
        _attn_body(False, *refs)


def _attn(q, k, vt, nb, n, nc, with_ctx_queries):
    n_lat = nb * n
    tq = ATT_QUERY_TILE
    lat_steps = n // tq
    ctx_steps = nc // tq if with_ctx_queries else 0
    rows = n_lat + (nb * nc if with_ctx_queries else 0)

    def q_map(b, i):
        return (jnp.where(i < lat_steps, b * lat_steps + i, n_lat // tq + b * ctx_steps + (i - lat_steps)), 0)

    m = Q_PER_KV * tq
    tk = min(ATT_KEY_TILE, n)
    assert n % tk == 0 and n % tq == 0 and nc % tq == 0
    return pl.pallas_call(
        functools.partial(_attn_kernel, lat_steps, ctx_steps),
        grid=(nb, lat_steps + ctx_steps),
        in_specs=[pl.BlockSpec((tq, ATT_WIDTH), q_map),
                  pl.BlockSpec((nc, KV_WIDTH), lambda b, i: (n_lat // nc + b, 0)),
                  pl.BlockSpec((KV_WIDTH, nc), lambda b, i: (0, n_lat // nc + b)),
                  pl.BlockSpec((n, KV_WIDTH), lambda b, i: (b, 0)),
                  pl.BlockSpec((KV_WIDTH, n), lambda b, i: (0, b))],
        out_specs=pl.BlockSpec((tq, ATT_WIDTH), q_map),
        out_shape=jax.ShapeDtypeStruct((rows, ATT_WIDTH), BF16),
        scratch_shapes=[pltpu.VMEM((HEAD_DIM, m), F32), pltpu.VMEM((2, tk, m), F32)],
        compiler_params=_cparams("parallel", "arbitrary"),
        name="attention",
    )(q, k, vt, k, vt)


def _outproj_kernel(routed, lat_tiles, *refs):
    if routed:
        ra_ref, hg_ref, at_ref, xl_ref, xc_ref, mod_ref, w_ref, g_ref, rt_ref, xo_ref, h_ref, lg_ref = refs
    else:
        ra_ref, hg_ref, at_ref, xl_ref, xc_ref, mod_ref, w_ref, g_ref, xo_ref, h_ref = refs
    w1 = ra_ref.shape[1]
    w2 = w1 + hg_ref.shape[1]
    mix = (_dot(ra_ref[...], w_ref[0:w1, :]) + _dot(hg_ref[...], w_ref[w1:w2, :])
           + _dot(at_ref[...], w_ref[w2:, :]))
    m = mod_ref[0]
    x_in = jnp.where(pl.program_id(0) < lat_tiles, xl_ref[...], xc_ref[...])
    x = x_in + m[2:3] * mix
    xo_ref[...] = x
    y = x * lax.rsqrt(jnp.mean(x * x, axis=-1, keepdims=True) + EPS) * g_ref[...]
    h = y * (1.0 + m[4:5]) + m[3:4]
    if routed:
        h_ref[...] = h
        h_hi = h.astype(BF16)
        h_lo = (h - h_hi.astype(F32)).astype(BF16)
        lg = _dot(h_hi, rt_ref[...]) + _dot(h_lo, rt_ref[...])
        lg_ref[...] = lg[:, 0:LANES] + lg[:, LANES:2 * LANES]
    else:
        h_ref[...] = h.astype(BF16)


def _outproj(ra, hg, at, x_lat, x_ctx, ctx_base, mod, w, g, router, n_tok, n_lat, seq, nb):
    d = x_lat.shape[1]
    tm = 256
    routed = router is not None
    row = functools.partial(_mod_row, tile=tm, n_lat=n_lat, seq=seq, nb=nb)
    tok = lambda width: pl.BlockSpec((tm, width), lambda i: (i, 0))
    lat_spec, ctx_spec = _stream_specs(tm, d, n_lat // tm, ctx_base // tm)
    in_specs = [tok(ra.shape[1]), tok(hg.shape[1]), tok(at.shape[1]), lat_spec, ctx_spec,
                pl.BlockSpec((1, N_MOD, d), lambda i: (row(i), 0, 0)),
                pl.BlockSpec(w.shape, lambda i: (0, 0)),
                pl.BlockSpec((1, d), lambda i: (0, 0))]
    args = [ra, hg, at, x_lat, x_ctx, mod, w, g.reshape(1, d)]
    out_specs = [tok(d), tok(d)]
    out_shape = [jax.ShapeDtypeStruct((n_tok, d), F32), jax.ShapeDtypeStruct((n_tok, d), F32 if routed else BF16)]
    if routed:
        in_specs.append(pl.BlockSpec((d, 2 * LANES), lambda i: (0, 0)))
        args.append(router)
        out_specs.append(tok(LANES))
        out_shape.append(jax.ShapeDtypeStruct((n_tok, LANES), F32))
    return pl.pallas_call(
        functools.partial(_outproj_kernel, routed, n_lat // tm),
        grid=(n_tok // tm,),
        in_specs=in_specs,
        out_specs=out_specs,
        out_shape=out_shape,
        compiler_params=_cparams("parallel"),
        name="outproj",
    )(*args)


def _ffn_kernel(h_ref, x_ref, mod_ref, w1_ref, w3_ref, w2_ref, o_ref, acc_ref):
    k = pl.program_id(1)

    @pl.when(k == 0)
    def _():
        acc_ref[...] = jnp.zeros_like(acc_ref)

    h = h_ref[...]
    a = _dot(h, w1_ref[...])
    z = _silu(a) * _dot(h, w3_ref[...])
    acc_ref[...] += _dot(z.astype(BF16), w2_ref[...])

    @pl.when(k == pl.num_programs(1) - 1)
    def _():
        o_ref[...] = x_ref[...] + mod_ref[0][5:6] * acc_ref[...]


def _ffn(h, xs, mod, w1, w3, w2, n_lat, seq, nb):
    t, d = h.shape
    f = w1.shape[1]
    tm = TOKEN_TILE
    tf = 512
    row = functools.partial(_mod_row, tile=tm, n_lat=n_lat, seq=seq, nb=nb)
    return pl.pallas_call(
        _ffn_kernel,
        grid=(t // tm, f // tf),
        in_specs=[
            pl.BlockSpec((tm, d), lambda i, k: (i, 0)),
            pl.BlockSpec((tm, d), lambda i, k: (i, 0)),
            pl.BlockSpec((1, N_MOD, d), lambda i, k: (row(i), 0, 0)),
            pl.BlockSpec((d, tf), lambda i, k: (0, k)),
            pl.BlockSpec((d, tf), lambda i, k: (0, k)),
            pl.BlockSpec((tf, d), lambda i, k: (k, 0)),
        ],
        out_specs=pl.BlockSpec((tm, d), lambda i, k: (i, 0)),
        out_shape=jax.ShapeDtypeStruct((t, d), F32),
        scratch_shapes=[pltpu.VMEM((tm, d), F32)],
        compiler_params=_cparams("parallel", "arbitrary"),
        name="ffn_dense",
    )(h, xs, mod, w1, w3, w2)


def _router_kernel(lg_ref, meta_ref, wts_ref, cnt_ref, run_ref):
    @pl.when(pl.program_id(0) == 0)
    def _():
        run_ref[...] = jnp.zeros_like(run_ref)

    lg = lg_ref[...]
    tm = lg.shape[0]
    lane = lax.broadcasted_iota(jnp.int32, lg.shape, 1)
    lane_f = lane.astype(F32)
    v = jnp.where(lane < N_EXPERTS, lg, -jnp.inf)
    m1 = jnp.max(v, axis=-1, keepdims=True)
    i1 = jnp.min(jnp.where(v == m1, lane_f, float(LANES)), axis=-1, keepdims=True)
    v2 = jnp.where(lane_f == i1, -jnp.inf, v)
    m2 = jnp.max(v2, axis=-1, keepdims=True)
    i2 = jnp.min(jnp.where(v2 == m2, lane_f, float(LANES)), axis=-1, keepdims=True)
    e = jnp.exp(m2 - m1)
    wt1 = 1.0 / (1.0 + e)
    wt2 = e / (1.0 + e)
    hit1 = lane_f == i1
    hit2 = lane_f == i2
    assign = jnp.where(hit1 | hit2, 1.0, 0.0)
    r = lax.broadcasted_iota(jnp.int32, (tm, tm), 0)
    c = lax.broadcasted_iota(jnp.int32, (tm, tm), 1)
    tri = jnp.where(r > c, 1.0, 0.0).astype(BF16)
    rank = _dot(tri, assign.astype(BF16)) + run_ref[0:1, :]
    r1 = jnp.sum(jnp.where(hit1, rank, 0.0), axis=-1, keepdims=True)
    r2 = jnp.sum(jnp.where(hit2, rank, 0.0), axis=-1, keepdims=True)
    run_ref[...] = run_ref[...] + jnp.sum(assign, axis=0, keepdims=True)
    meta = jnp.where(lane == 0, i1, jnp.where(lane == 1, i2, jnp.where(lane == 2, r1, jnp.where(lane == 3, r2, 0.0))))
    meta_ref[...] = meta.astype(jnp.int32)
    wts_ref[...] = jnp.where(lane == 0, wt1, jnp.where(lane == 1, wt2, 0.0))
    cnt_ref[...] = run_ref[...]


def _router(logits):
    t = logits.shape[0]
    tm = TOKEN_TILE
    tok = pl.BlockSpec((tm, LANES), lambda i: (i, 0))
    return pl.pallas_call(
        _router_kernel,
        grid=(t // tm,),
        in_specs=[tok],
        out_specs=[tok, tok, pl.BlockSpec((SUBLANES, LANES), lambda i: (0, 0))],
        out_shape=[jax.ShapeDtypeStruct((t, LANES), jnp.int32), jax.ShapeDtypeStruct((t, LANES), F32),
                   jax.ShapeDtypeStruct((SUBLANES, LANES), F32)],
        scratch_shapes=[pltpu.VMEM((SUBLANES, LANES), F32)],
        compiler_params=_cparams("arbitrary"),
        name="router",
    )(logits)


def _row_copy(src_ref, src_row, dst_ref, dst_row, sem):
    return pltpu.make_async_copy(src_ref.at[pl.ds(src_row, 1)], dst_ref.at[pl.ds(dst_row, 1)], sem)


def _dispatch_kernel(pos_ref, h_ref, xs_in_ref, xs_ref, pos_smem, hbuf, sem_p, sem_h, sem_d):
    del xs_in_ref
    i = pl.program_id(0)
    n = pl.num_programs(0)
    tm = hbuf.shape[1]
    cur = i % 3

    def tile_copy(tile, buf):
        return pltpu.make_async_copy(h_ref.at[pl.ds(pl.multiple_of(tile * tm, tm), tm)], hbuf.at[buf], sem_h.at[buf])

    @pl.when(i == 0)
    def _():
        tile_copy(0, 0).start()

    @pl.when(i < n - 1)
    def _():
        tile_copy(i + 1, (i + 1) % 3).start()

    cp = pltpu.make_async_copy(pos_ref.at[0, 0], pos_smem, sem_p)
    cp.start()
    cp.wait()
    tile_copy(i, cur).wait()

    def issue(t, carry):
        _row_copy(hbuf.at[cur], t, xs_ref, pos_smem[t], sem_d.at[cur]).start()
        _row_copy(hbuf.at[cur], t, xs_ref, pos_smem[tm + t], sem_d.at[cur]).start()
        return carry

    lax.fori_loop(0, tm, issue, 0, unroll=8)

    def drain(buf):
        def body(t, carry):
            _row_copy(hbuf.at[buf], 0, xs_ref, 0, sem_d.at[buf]).wait()
            _row_copy(hbuf.at[buf], 0, xs_ref, 0, sem_d.at[buf]).wait()
            return carry
        lax.fori_loop(0, tm, body, 0, unroll=8)

    @pl.when(i > 0)
    def _():
        drain((i + 2) % 3)

    @pl.when(i == n - 1)
    def _():
        drain(cur)


def _dispatch(pos, h, xs0):
    t, d = h.shape
    tm = 256
    nt = t // tm
    return pl.pallas_call(
        _dispatch_kernel,
        grid=(nt,),
        in_specs=[
            pl.BlockSpec((1, 1, 2 * tm), lambda i: (i, 0, 0)),
            pl.BlockSpec(memory_space=pl.ANY),
            pl.BlockSpec(memory_space=pl.ANY),
        ],
        out_specs=pl.BlockSpec(memory_space=pl.ANY),
        out_shape=jax.ShapeDtypeStruct(xs0.shape, xs0.dtype),
        scratch_shapes=[pltpu.SMEM((2 * tm,), jnp.int32), pltpu.VMEM((3, tm, d), F32),
                        pltpu.SemaphoreType.DMA(()), pltpu.SemaphoreType.DMA((3,)), pltpu.SemaphoreType.DMA((3,))],
        input_output_aliases={2: 0},
        compiler_params=_cparams("arbitrary"),
        name="moe_dispatch",
    )(pos, h, xs0)


def _expert_kernel(te_ref, tv_ref, x_ref, w1_ref, w3_ref, w2_ref, y_ref, xb_ref, acc_ref):
    del te_ref
    i = pl.program_id(0)
    k = pl.program_id(1)

    @pl.when(tv_ref[i] > 0)
    def _():
        @pl.when(k == 0)
        def _():
            xb_ref[...] = x_ref[...].astype(BF16)
            acc_ref[...] = jnp.zeros_like(acc_ref)

        xb = xb_ref[...]
        a = _dot(xb, w1_ref[...])
        z = _silu(a) * _dot(xb, w3_ref[...])
        acc_ref[...] += _dot(z.astype(BF16), w2_ref[...])

        @pl.when(k == pl.num_programs(1) - 1)
        def _():
            y_ref[...] = acc_ref[...]

    @pl.when((tv_ref[i] == 0) & (k == 0))
    def _():
        y_ref[...] = jnp.zeros_like(y_ref)


def _experts(tile_expert, tile_valid, tile_row, xs, w1, w3, w2):
    rows, d = xs.shape
    f = w1.shape[2]
    tm = EXPERT_TILE
    tf = 512
    grid_spec = pltpu.PrefetchScalarGridSpec(
        num_scalar_prefetch=3,
        grid=(rows // tm, f // tf),
        in_specs=[
            pl.BlockSpec((tm, d), lambda i, k, te, tv, tr: (tr[i], 0)),
            pl.BlockSpec((None, d, tf), lambda i, k, te, tv, tr: (te[i], 0, jnp.where(tv[i] > 0, k, f // tf - 1))),
            pl.BlockSpec((None, d, tf), lambda i, k, te, tv, tr: (te[i], 0, jnp.where(tv[i] > 0, k, f // tf - 1))),
            pl.BlockSpec((None, tf, d), lambda i, k, te, tv, tr: (te[i], jnp.where(tv[i] > 0, k, f // tf - 1), 0)),
        ],
        out_specs=pl.BlockSpec((tm, d), lambda i, k, te, tv, tr: (i, 0)),
        scratch_shapes=[pltpu.VMEM((tm, d), BF16), pltpu.VMEM((tm, d), F32)],
    )

    def body(te_ref, tv_ref, tr_ref, *rest):
        del tr_ref
        _expert_kernel(te_ref, tv_ref, *rest)

    return pl.pallas_call(
        body,
        grid_spec=grid_spec,
        out_shape=jax.ShapeDtypeStruct((rows, d), F32),
        compiler_params=_cparams("arbitrary", "arbitrary"),
        name="moe_experts",
    )(tile_expert, tile_valid, tile_row, xs, w1, w3, w2)


def _combine_kernel(pos0_ref, posn_ref, wts_ref, x_ref, mod_ref, ys_ref, o_ref, pos_smem, y_ref, sem_p, sem_g):
    i = pl.program_id(0)
    n = pl.num_programs(0)
    tm = x_ref.shape[0]
    cur = i % 2

    def gather(pos_ref, slot):
        cp = pltpu.make_async_copy(pos_ref.at[0, 0], pos_smem, sem_p)
        cp.start()
        cp.wait()

        def issue(t, carry):
            _row_copy(ys_ref, pos_smem[t], y_ref.at[slot, 0], t, sem_g.at[slot]).start()
            _row_copy(ys_ref, pos_smem[tm + t], y_ref.at[slot, 1], t, sem_g.at[slot]).start()
            return carry

        lax.fori_loop(0, tm, issue, 0, unroll=8)

    @pl.when(i == 0)
    def _():
        gather(pos0_ref, 0)

    @pl.when(i < n - 1)
    def _():
        gather(posn_ref, 1 - cur)

    def drain(t, carry):
        _row_copy(ys_ref, 0, y_ref.at[cur, 0], 0, sem_g.at[cur]).wait()
        _row_copy(ys_ref, 0, y_ref.at[cur, 1], 0, sem_g.at[cur]).wait()
        return carry

    lax.fori_loop(0, tm, drain, 0, unroll=8)
    wts = wts_ref[...]
    f = wts[:, 0:1] * y_ref[cur, 0] + wts[:, 1:2] * y_ref[cur, 1]
    o_ref[...] = x_ref[...] + mod_ref[0][5:6] * f


def _combine(pos, wts, xs, mod, ys, n_lat, seq, nb):
    t, d = xs.shape
    tm = 256
    row = functools.partial(_mod_row, tile=tm, n_lat=n_lat, seq=seq, nb=nb)
    return pl.pallas_call(
        _combine_kernel,
        grid=(t // tm,),
        in_specs=[
            pl.BlockSpec((1, 1, 2 * tm), lambda i: (0, 0, 0)),
            pl.BlockSpec((1, 1, 2 * tm), lambda i: (jnp.minimum(i + 1, t // tm - 1), 0, 0)),
            pl.BlockSpec((tm, LANES), lambda i: (i, 0)),
            pl.BlockSpec((tm, d), lambda i: (i, 0)),
            pl.BlockSpec((1, N_MOD, d), lambda i: (row(i), 0, 0)),
            pl.BlockSpec(memory_space=pl.ANY),
        ],
        out_specs=pl.BlockSpec((tm, d), lambda i: (i, 0)),
        out_shape=jax.ShapeDtypeStruct((t, d), F32),
        scratch_shapes=[pltpu.SMEM((2 * tm,), jnp.int32), pltpu.VMEM((2, 2, tm, d), F32),
                        pltpu.SemaphoreType.DMA(()), pltpu.SemaphoreType.DMA((2,))],
        compiler_params=_cparams("arbitrary"),
        name="moe_combine",
    )(pos, pos, wts, xs, mod, ys)


def _moe(h, logits, xs, mod, w1, w3, w2, n_lat, seq, nb):
    t, d = h.shape
    meta, wts, counts = _router(logits)
    te_rows = EXPERT_TILE
    cnt = counts[0, :N_EXPERTS].astype(jnp.int32)
    padded = ((cnt + te_rows - 1) // te_rows) * te_rows
    ends = jnp.cumsum(padded)
    starts = ends - padded
    pos1 = jnp.take(starts, meta[:, 0]) + meta[:, 2]
    pos2 = jnp.take(starts, meta[:, 1]) + meta[:, 3]
    tmd = 256
    pos = jnp.concatenate([pos1.reshape(t // tmd, 1, tmd), pos2.reshape(t // tmd, 1, tmd)], axis=-1)
    n_tiles = (2 * t) // te_rows + N_EXPERTS
    tile_start = jnp.arange(n_tiles, dtype=jnp.int32) * te_rows
    n_valid = ends[-1] // te_rows
    tile_valid = (tile_start < ends[-1]).astype(jnp.int32)
    tile_row = jnp.minimum(jnp.arange(n_tiles, dtype=jnp.int32), n_valid - 1)
    tile_expert = jnp.sum((tile_row[:, None] * te_rows >= ends[None, :]).astype(jnp.int32), axis=1)
    xs0 = jnp.zeros((n_tiles * te_rows, d), F32)
    x_sorted = _dispatch(pos, h, xs0)
    y_sorted = _experts(tile_expert, tile_valid, tile_row, x_sorted, w1, w3, w2)
    return _combine(pos, wts, xs, mod, y_sorted, n_lat, seq, nb)


def kernel(x, c, ctx, c_ctx, w_mod, b_mod, norm1, norm2, w_in, w_out, conv_w, conv_b, lru_wa, lru_ba, lru_wi, lru_bi, lru_lambda, hgrn_lb_logits, hgrn_gnorm, q_norm, k_norm, ffn_w1, ffn_w3, ffn_w2, router, moe_w1, moe_w3, moe_w2):
    nb, n, d = x.shape
    nc = ctx.shape[1]
    depth = w_mod.shape[0]
    assert n % SEQ_CHUNK == 0 and nc % SEQ_CHUNK == 0 and n % TOKEN_TILE == 0
    assert (nb * nc) % TOKEN_TILE == 0 and n % GRID_W == 0
    n_lat = nb * n
    nlc = n // SEQ_CHUNK
    ncc = nc // SEQ_CHUNK

    n_all = n_lat + nb * nc
    x_lat, x_ctx, ctx_base = x.reshape(n_lat, d), ctx.reshape(nb * nc, d), 0
    mod_rows = 2 * SUBLANES * ((nb + 1 + 2 * SUBLANES - 1) // (2 * SUBLANES))
    cc = jnp.zeros((mod_rows, d), F32).at[:nb].set(c).at[nb].set(c_ctx)
    mod_all = _mod_table(cc, w_mod, b_mod).reshape(depth, mod_rows, N_MOD, d)
    cos, sin = _rope_tables(n)
    router_f = jnp.pad(router, ((0, 0), (0, 0), (0, LANES - router.shape[-1])))
    router_hi = router_f.astype(BF16)
    router_lo = (router_f - router_hi.astype(F32)).astype(BF16)
    router_p = jnp.concatenate([router_hi, router_lo], axis=-1)

    for l in range(depth):
        need_ctx = l < depth - 1
        mod = mod_all[l]
        u = _inproj(x_lat, x_ctx, ctx_base, n_all, mod, norm1[l], _to_bf16(w_in, l), n_lat, n, nb)

        lru_args = lambda dd: (conv_w[l], conv_b[l], lru_wa[l, dd], lru_ba[l, dd], lru_wi[l, dd], lru_bi[l, dd],
                               lru_lambda[l, dd])
        hf = _lru_pass(False, u, *lru_args(0), None, nb, ncc, nlc)
        ra = _lru_pass(True, u, *lru_args(1), hf, nb, ncc, nlc)

        of = _hgrn_pass(False, l, u, hgrn_lb_logits[0], hgrn_gnorm[l], None, nb, ncc, nlc)
        hg = _hgrn_pass(True, l, u, hgrn_lb_logits[1], hgrn_gnorm[l], of, nb, ncc, nlc)

        q, k, vt = _qkv(u, cos, sin, q_norm[l], k_norm[l], n_lat, nlc)
        at = _attn(q, k, vt, nb, n, nc, need_ctx)

        n_tok = n_all if need_ctx else n_lat
        j = l // 2
        routed = l % 2 == 1
        outs = _outproj(ra, hg, at, x_lat, x_ctx, ctx_base, mod, _to_bf16(w_out, l), norm2[l],
                        router_p[j] if routed else None, n_tok, n_lat, n, nb)
        if routed:
            xn, h2, logits = outs
            xs = _moe(h2, logits, xn, mod, _to_bf16(moe_w1, j), _to_bf16(moe_w3, j), _to_bf16(moe_w2, j),
                      n_lat, n, nb)
        else:
            xn, h2 = outs
            xs = _ffn(h2, xn, mod, _to_bf16(ffn_w1, j), _to_bf16(ffn_w3, j), _to_bf16(ffn_w2, j), n_lat, n, nb)
        x_lat, x_ctx, ctx_base = xs, xs, n_lat
    return xs[:n_lat].reshape(nb, n, d)
```

```python
import functools

import jax
import jax.numpy as jnp
from jax import lax
from jax.experimental import pallas as pl
from jax.experimental.pallas import tpu as pltpu

F32 = jnp.float32
BF16 = jnp.bfloat16

EPS = 1e-6
N_MOD = 6
GRID_W = 64
LRU_WIDTH = 512
LRU_BLOCKS = 4
LRU_BLOCK = LRU_WIDTH // LRU_BLOCKS
LRU_C = 8.0
HG_HEADS = 4
HG_DK = 128
HG_WIDTH = HG_HEADS * HG_DK
HG_CHUNK = 64
HG_SUB = 8
HEAD_DIM = 128
N_Q_HEADS = 8
N_KV_HEADS = 2
Q_PER_KV = N_Q_HEADS // N_KV_HEADS
ATT_WIDTH = N_Q_HEADS * HEAD_DIM
KV_WIDTH = N_KV_HEADS * HEAD_DIM
ROPE_THETA = 10000.0
ROPE_FREQS = HEAD_DIM // 4
ATTN_SCALE = HEAD_DIM ** -0.5
LOG2E = 1.4426950408889634
Q_SCALE = ATTN_SCALE * LOG2E
A_IN = 2 * LRU_WIDTH
B_IN = 5 * HG_WIDTH
C_IN = ATT_WIDTH + 2 * KV_WIDTH
IN_WIDTH = A_IN + B_IN + C_IN
N_EXPERTS = 8

LANES = 128
SUBLANES = 8
SEQ_CHUNK = 256
TOKEN_TILE = 512
INPROJ_FLAT_TILE = 1024
EXPERT_TILE = 512
ATT_KEY_TILE = 1024
ATT_QUERY_TILE = 256
CAST_BLOCK_ELEMS = 2 * 1024 * 1024
VMEM_LIMIT = 56 * 1024 * 1024
NEG_BIG = -1e30


def _cparams(*sem):
    return pltpu.CompilerParams(dimension_semantics=sem, vmem_limit_bytes=VMEM_LIMIT)


def _sigmoid(v):
    return 1.0 / (1.0 + jnp.exp(-v))


def _sigmoid_tanh(v):
    return 0.5 * jnp.tanh(0.5 * v) + 0.5


def _silu(v):
    return v * _sigmoid_tanh(v)


def _dot(a, b):
    return jnp.dot(a, b, preferred_element_type=F32)


def _dot_nt(a, b):
    return lax.dot_general(a, b, (((1,), (1,)), ((), ())), preferred_element_type=F32)


def _dot_tn(a, b):
    return lax.dot_general(a, b, (((0,), (0,)), ((), ())), preferred_element_type=F32)


def _mod_row(i, tile, n_lat, seq, nb):
    return jnp.where(i < n_lat // tile, (i * tile) // seq, nb)


def _cast_kernel(w_ref, o_ref):
    o_ref[...] = w_ref[...].astype(BF16)


def _to_bf16(w, index):
    inner = w.shape[1:]
    cols = inner[-1]
    rows = 1
    for s in inner[:-1]:
        rows *= s
    tr = 2 * SUBLANES
    while tr * 2 * cols <= CAST_BLOCK_ELEMS and rows % (tr * 2) == 0:
        tr *= 2
    assert rows % tr == 0
    steps = rows // tr
    out = pl.pallas_call(
        _cast_kernel,
        grid=(steps,),
        in_specs=[pl.BlockSpec((tr, cols), lambda i: (index * steps + i, 0))],
        out_specs=pl.BlockSpec((tr, cols), lambda i: (i, 0)),
        out_shape=jax.ShapeDtypeStruct((rows, cols), BF16),
        compiler_params=_cparams("parallel"),
        name="to_bf16",
    )(w.reshape(w.shape[0] * rows, cols))
    return out.reshape(inner)


def _mod_kernel(c_ref, w_ref, b_ref, o_ref):
    s = _silu(c_ref[...]).astype(BF16)
    o_ref[0] = _dot(s, w_ref[0].astype(BF16)) + b_ref[0]


def _mod_table(cc, w_mod, b_mod):
    nl, d, md = w_mod.shape
    rows = cc.shape[0]
    tn = 1024 if md % 1024 == 0 else md
    return pl.pallas_call(
        _mod_kernel,
        grid=(nl, md // tn),
        in_specs=[
            pl.BlockSpec((rows, d), lambda l, j: (0, 0)),
            pl.BlockSpec((1, d, tn), lambda l, j: (l, 0, j)),
            pl.BlockSpec((1, 1, tn), lambda l, j: (l, 0, j)),
        ],
        out_specs=pl.BlockSpec((1, rows, tn), lambda l, j: (l, 0, j)),
        out_shape=jax.ShapeDtypeStruct((nl, rows, md), F32),
        compiler_params=_cparams("parallel", "parallel"),
        name="mod_table",
    )(cc, w_mod, b_mod.reshape(nl, 1, md))


def _stream_specs(tm, d, lat_tiles, ctx_base):
    lat = pl.BlockSpec((tm, d), lambda i, *_: (jnp.minimum(i, lat_tiles - 1), 0))
    ctx = pl.BlockSpec((tm, d), lambda i, *_: (ctx_base + jnp.maximum(i - lat_tiles, 0), 0))
    return lat, ctx


def _inproj_kernel(lat_tiles, *refs):
    if lat_tiles is None:
        x_ref, mod_ref, g_ref, w_ref, o_ref, h_ref = refs
    else:
        xl_ref, xc_ref, mod_ref, g_ref, w_ref, o_ref, h_ref = refs
    i = pl.program_id(0)
    first = pl.program_id(1) == 0

    def norm_mod(x_ref):
        x = x_ref[...]
        y = x * lax.rsqrt(jnp.mean(x * x, axis=-1, keepdims=True) + EPS) * g_ref[...]
        m = mod_ref[0]
        h_ref[...] = (y * (1.0 + m[1:2]) + m[0:1]).astype(BF16)

    if lat_tiles is None:
        @pl.when(first)
        def _():
            norm_mod(x_ref)
    else:
        @pl.when(first & (i < lat_tiles))
        def _():
            norm_mod(xl_ref)

        @pl.when(first & (i >= lat_tiles))
        def _():
            norm_mod(xc_ref)

    o_ref[...] = _dot(h_ref[...], w_ref[...])


def _inproj(x_lat, x_ctx, ctx_base, n_tok, mod, g, w, n_lat, seq, nb):
    d = x_lat.shape[1]
    width = w.shape[1]
    tn = 1024
    flat = x_lat is x_ctx
    if flat:
        tm = INPROJ_FLAT_TILE if (seq % INPROJ_FLAT_TILE == 0 and (n_tok - n_lat) % INPROJ_FLAT_TILE == 0) \
            else TOKEN_TILE
        x_specs = [pl.BlockSpec((tm, d), lambda i, j: (i, 0))]
        x_args = [x_lat]
    else:
        tm = TOKEN_TILE
        x_specs = list(_stream_specs(tm, d, n_lat // tm, ctx_base // tm))
        x_args = [x_lat, x_ctx]
    row = functools.partial(_mod_row, tile=tm, n_lat=n_lat, seq=seq, nb=nb)
    return pl.pallas_call(
        functools.partial(_inproj_kernel, None if flat else n_lat // tm),
        grid=(n_tok // tm, width // tn),
        in_specs=x_specs + [
            pl.BlockSpec((1, N_MOD, d), lambda i, j: (row(i), 0, 0)),
            pl.BlockSpec((1, d), lambda i, j: (0, 0)),
            pl.BlockSpec((d, tn), lambda i, j: (0, j)),
        ],
        out_specs=pl.BlockSpec((tm, tn), lambda i, j: (i, j)),
        out_shape=jax.ShapeDtypeStruct((n_tok, width), F32),
        scratch_shapes=[pltpu.VMEM((tm, d), BF16)],
        compiler_params=_cparams("parallel", "arbitrary"),
        name="inproj",
    )(*x_args, mod, g.reshape(1, d), w)


def _seq_pos(j, rev, ncc, nlc):
    is_ctx = j < ncc
    jl = j - ncc
    cc = (ncc - 1 - j) if rev else j
    cl = (nlc - 1 - jl) if rev else jl
    return is_ctx, cc, cl


def _seq_blk(b, j, rev, ncc, nlc, nb):
    is_ctx, cc, cl = _seq_pos(j, rev, ncc, nlc)
    return jnp.where(is_ctx, nb * nlc + b * ncc + cc, b * nlc + cl)


def _scan_rows(a_ref, b_ref, h_ref, hc_ref, rev):
    n_groups = a_ref.shape[0] // SUBLANES
    width = a_ref.shape[1]
    rows = lax.broadcasted_iota(jnp.int32, (SUBLANES, width), 0)

    def body(g, hc):
        gi = (n_groups - 1 - g) if rev else g
        r0 = pl.multiple_of(gi * SUBLANES, SUBLANES)
        a = a_ref[pl.ds(r0, SUBLANES), :]
        b = b_ref[pl.ds(r0, SUBLANES), :]
        for s in (1, 2, 4):
            if rev:
                a_s = pltpu.roll(a, SUBLANES - s, 0)
                b_s = pltpu.roll(b, SUBLANES - s, 0)
                keep = rows < SUBLANES - s
            else:
                a_s = pltpu.roll(a, s, 0)
                b_s = pltpu.roll(b, s, 0)
                keep = rows >= s
            b = a * jnp.where(keep, b_s, 0.0) + b
            a = a * jnp.where(keep, a_s, 1.0)
        h = b + a * hc
        h_ref[pl.ds(r0, SUBLANES), :] = h
        return h[0:1] if rev else h[SUBLANES - 1:SUBLANES]

    hc_ref[...] = lax.fori_loop(0, n_groups, body, hc_ref[...], unroll=4)


def _lru_kernel(rev, ncc, nlc, *refs):
    if rev:
        (u_ref, prev_ref, next_ref, cw_ref, cb_ref, wa_ref, ba_ref, wi_ref, bi_ref, lam_ref,
         hf_ref, o_ref, xe_ref, a_ref, b_ref, h_ref, hc_ref) = refs
    else:
        (u_ref, prev_ref, next_ref, cw_ref, cb_ref, wa_ref, ba_ref, wi_ref, bi_ref, lam_ref,
         o_ref, xe_ref, a_ref, b_ref, hc_ref) = refs
        h_ref = o_ref
    j = pl.program_id(1)
    is_ctx, cc, cl = _seq_pos(j, rev, ncc, nlc)
    c = jnp.where(is_ctx, cc, cl)
    nseq = jnp.where(is_ctx, ncc, nlc)

    @pl.when(j == 0)
    def _():
        hc_ref[...] = jnp.zeros_like(hc_ref)

    ch = SEQ_CHUNK
    w = LRU_WIDTH
    xe_ref[SUBLANES:SUBLANES + ch, :] = u_ref[:, w:2 * w]
    xe_ref[0:SUBLANES, :] = jnp.where(c > 0, prev_ref[...], 0.0)
    xe_ref[SUBLANES + ch:2 * SUBLANES + ch, :] = jnp.where(c < nseq - 1, next_ref[...], 0.0)
    cw = cw_ref[...]
    xc = cb_ref[...]
    for tap in range(4):
        off = SUBLANES - 2 + tap
        xc = xc + cw[tap:tap + 1] * xe_ref[off:off + ch, :]
    xb = xc.astype(BF16)
    ra = jnp.concatenate(
        [_dot(xb[:, n * LRU_BLOCK:(n + 1) * LRU_BLOCK], wa_ref[n]) for n in range(LRU_BLOCKS)], axis=-1)
    ri = jnp.concatenate(
        [_dot(xb[:, n * LRU_BLOCK:(n + 1) * LRU_BLOCK], wi_ref[n]) for n in range(LRU_BLOCKS)], axis=-1)
    r = _sigmoid_tanh(ra + ba_ref[...])
    gate_i = _sigmoid_tanh(ri + bi_ref[...])
    neg_lam = -lam_ref[...]
    softplus = jnp.maximum(neg_lam, 0.0) + jnp.log1p(jnp.exp(-jnp.abs(neg_lam)))
    log_a = -LRU_C * r * softplus
    a = jnp.exp(log_a)
    a_ref[...] = a
    b_ref[...] = jnp.sqrt(-jnp.tanh(log_a) * (a * a + 1.0)) * (gate_i * xc)
    _scan_rows(a_ref, b_ref, h_ref, hc_ref, rev)
    if rev:
        y = u_ref[:, 0:w]
        gelu = 0.5 * y * (1.0 + jnp.tanh(0.7978845608028654 * (y + 0.044715 * (y * y * y))))
        o_ref[...] = (gelu * (hf_ref[...] + h_ref[...])).astype(BF16)


def _lru_pass(rev, u, cw, cb, wa, ba, wi, bi, lam, hf, nb, ncc, nlc):
    t = u.shape[0]
    ch = SEQ_CHUNK
    w = LRU_WIDTH
    r8 = ch // SUBLANES
    blk = functools.partial(_seq_blk, rev=rev, ncc=ncc, nlc=nlc, nb=nb)
    full2 = lambda b, j: (0, 0)
    in_specs = [
        pl.BlockSpec((ch, 2 * w), lambda b, j: (blk(b, j), 0)),
        pl.BlockSpec((SUBLANES, w), lambda b, j: (jnp.maximum(blk(b, j) * r8 - 1, 0), 1)),
        pl.BlockSpec((SUBLANES, w), lambda b, j: (jnp.minimum(blk(b, j) * r8 + r8, t // SUBLANES - 1), 1)),
        pl.BlockSpec((4, w), full2),
        pl.BlockSpec((1, w), full2),
        pl.BlockSpec((LRU_BLOCKS, LRU_BLOCK, LRU_BLOCK), lambda b, j: (0, 0, 0)),
        pl.BlockSpec((1, w), full2),
        pl.BlockSpec((LRU_BLOCKS, LRU_BLOCK, LRU_BLOCK), lambda b, j: (0, 0, 0)),
        pl.BlockSpec((1, w), full2),
        pl.BlockSpec((1, w), full2),
    ]
    args = [u, u, u, cw, cb.reshape(1, w), wa.astype(BF16), ba.reshape(1, w), wi.astype(BF16),
            bi.reshape(1, w), lam.reshape(1, w)]
    scratch = [pltpu.VMEM((ch + 2 * SUBLANES, w), F32), pltpu.VMEM((ch, w), F32), pltpu.VMEM((ch, w), F32)]
    if rev:
        in_specs.append(pl.BlockSpec((ch, w), lambda b, j: (blk(b, j), 0)))
        args.append(hf)
        scratch.append(pltpu.VMEM((ch, w), F32))
        out_dtype = BF16
    else:
        out_dtype = F32
    scratch.append(pltpu.VMEM((1, w), F32))
    return pl.pallas_call(
        functools.partial(_lru_kernel, rev, ncc, nlc),
        grid=(nb, ncc + nlc),
        in_specs=in_specs,
        out_specs=pl.BlockSpec((ch, w), lambda b, j: (blk(b, j), 0)),
        out_shape=jax.ShapeDtypeStruct((t, w), out_dtype),
        scratch_shapes=scratch,
        compiler_params=_cparams("parallel", "arbitrary"),
        name="lru_bwd" if rev else "lru_fwd",
    )(*args)


def _cumsum_rows(v, rev):
    n = v.shape[0]
    rows = lax.broadcasted_iota(jnp.int32, v.shape, 0)
    s = 1
    while s < n:
        if rev:
            v = v + jnp.where(rows < n - s, pltpu.roll(v, n - s, 0), 0.0)
        else:
            v = v + jnp.where(rows >= s, pltpu.roll(v, s, 0), 0.0)
        s *= 2
    return v


def _hgrn_diag_att(q, k, bc, lo, width, rev):
    n = q.shape[0]
    lane = lax.broadcasted_iota(jnp.int32, (n, width), 1)
    row = lax.broadcasted_iota(jnp.int32, (n, width), 0)
    att = jnp.zeros((n, width), F32)
    for s in range(n):
        w = q * (k[s:s + 1] * jnp.exp2(bc - bc[s:s + 1]))
        att = jnp.where(lane == lo + s, jnp.sum(w, axis=-1, keepdims=True), att)
    keep = (row + lo <= lane) if rev else (row + lo >= lane)
    return jnp.where(keep, att, 0.0)


def _hgrn_chunk(qraw, fl, v, lbh, st, rev):
    cs = HG_CHUNK
    sb = HG_SUB
    q = _silu(qraw)
    f = lbh + (1.0 - lbh) * _sigmoid(fl)
    k = 1.0 - f
    bc = _cumsum_rows(jnp.log(f) * LOG2E, rev)
    blast = bc[0:1] if rev else bc[cs - 1:cs]
    o = _dot_nt((q * jnp.exp2(bc)).astype(BF16), st.astype(BF16))
    kdec = k * jnp.exp2(blast - bc)
    st_new = st * jnp.exp2(blast) + _dot_tn(v.astype(BF16), kdec.astype(BF16))
    att_rows = []
    for i in range(cs // sb):
        lo, hi = i * sb, (i + 1) * sb
        att_i = _hgrn_diag_att(q[lo:hi], k[lo:hi], bc[lo:hi], lo, cs, rev)
        if (hi < cs) if rev else (lo > 0):
            mref = bc[hi:hi + 1] if rev else bc[lo - 1:lo]
            plo, phi = (hi, cs) if rev else (0, lo)
            qhat = q[lo:hi] * jnp.exp2(bc[lo:hi] - mref)
            kp = k[plo:phi] * jnp.exp2(mref - bc[plo:phi])
            pad = jnp.zeros((cs - (phi - plo), HG_DK), F32)
            kp = jnp.concatenate([pad, kp] if rev else [kp, pad], axis=0)
            att_i = att_i + _dot_nt(qhat.astype(BF16), kp.astype(BF16))
        att_rows.append(att_i)
    att = jnp.concatenate(att_rows, axis=0)
    return o + _dot(att.astype(BF16), v.astype(BF16)), st_new


def _hgrn_kernel(rev, layer, *refs):
    if rev:
        q_ref, f_ref, v_ref, lbl_ref, og_ref, of_ref, gn_ref, o_ref, st_ref = refs
    else:
        q_ref, f_ref, v_ref, lbl_ref, o_ref, st_ref = refs

    @pl.when(pl.program_id(1) == 0)
    def _():
        st_ref[...] = jnp.zeros_like(st_ref)

    lg = lbl_ref[...]
    e = jnp.exp(lg - jnp.max(lg, axis=0, keepdims=True))
    p = e / jnp.sum(e, axis=0, keepdims=True)
    lb = jnp.zeros_like(p[0:1])
    for jl in range(1, layer + 1):
        lb = lb + p[jl:jl + 1]

    cs = HG_CHUNK
    n_chunks = q_ref.shape[0] // cs

    def body(ci, carry):
        cidx = (n_chunks - 1 - ci) if rev else ci
        r0 = pl.multiple_of(cidx * cs, cs)
        for h in range(HG_HEADS):
            cols = slice(h * HG_DK, (h + 1) * HG_DK)
            o, st_new = _hgrn_chunk(q_ref[pl.ds(r0, cs), cols], f_ref[pl.ds(r0, cs), cols],
                                    v_ref[pl.ds(r0, cs), cols], lb[:, cols], st_ref[h], rev)
            st_ref[h] = st_new
            if rev:
                o = o + of_ref[pl.ds(r0, cs), cols]
                y = o * lax.rsqrt(jnp.mean(o * o, axis=-1, keepdims=True) + EPS) * gn_ref[...]
                o_ref[pl.ds(r0, cs), cols] = (y * _silu(og_ref[pl.ds(r0, cs), cols])).astype(BF16)
            else:
                o_ref[pl.ds(r0, cs), cols] = o
        return carry

    lax.fori_loop(0, n_chunks, body, 0, unroll=4)


def _hgrn_pass(rev, layer, u, lb_logits, gnorm, of, nb, ncc, nlc):
    t = u.shape[0]
    ch = SEQ_CHUNK
    w = HG_WIDTH
    nl = lb_logits.shape[0]
    blk = functools.partial(_seq_blk, rev=rev, ncc=ncc, nlc=nlc, nb=nb)
    base = A_IN // w
    col = lambda cb: pl.BlockSpec((ch, w), lambda b, j: (blk(b, j), cb))
    in_specs = [col(base), col(base + 1 + (1 if rev else 0)), col(base + 3),
                pl.BlockSpec((nl, w), lambda b, j: (0, 0))]
    args = [u, u, u, lb_logits]
    if rev:
        in_specs += [col(base + 4), pl.BlockSpec((ch, w), lambda b, j: (blk(b, j), 0)),
                     pl.BlockSpec((1, HG_DK), lambda b, j: (0, 0))]
        args += [u, of, gnorm.reshape(1, HG_DK)]
    return pl.pallas_call(
        functools.partial(_hgrn_kernel, rev, layer),
        grid=(nb, ncc + nlc),
        in_specs=in_specs,
        out_specs=pl.BlockSpec((ch, w), lambda b, j: (blk(b, j), 0)),
        out_shape=jax.ShapeDtypeStruct((t, w), BF16 if rev else F32),
        scratch_shapes=[pltpu.VMEM((HG_HEADS, HG_DK, HG_DK), F32)],
        compiler_params=_cparams("parallel", "arbitrary"),
        name="hgrn_bwd" if rev else "hgrn_fwd",
    )(*args)


def _rope_tables(n):
    rows = n // GRID_W
    row = jnp.repeat(jnp.arange(rows, dtype=F32), GRID_W)
    colp = jnp.tile(jnp.arange(GRID_W, dtype=F32), rows)
    inv = ROPE_THETA ** (-jnp.arange(ROPE_FREQS, dtype=F32) / ROPE_FREQS)
    ar = row[:, None] * inv
    ac = colp[:, None] * inv
    cos = jnp.concatenate([jnp.cos(ar), jnp.cos(ar), jnp.cos(ac), jnp.cos(ac)], axis=-1)
    sin = jnp.concatenate([-jnp.sin(ar), jnp.sin(ar), -jnp.sin(ac), jnp.sin(ac)], axis=-1)
    cos = jnp.concatenate([cos, jnp.ones((SEQ_CHUNK, HEAD_DIM), F32)], axis=0)
    sin = jnp.concatenate([sin, jnp.zeros((SEQ_CHUNK, HEAD_DIM), F32)], axis=0)
    return cos, sin


def _qkv_kernel(qa_ref, qb_ref, kv_ref, cos_ref, sin_ref, qn_ref, kn_ref, q_out, k_out, vt_out):
    cos = cos_ref[...]
    sin = sin_ref[...]
    lane = lax.broadcasted_iota(jnp.int32, cos.shape, 1)
    first = (lane % (2 * ROPE_FREQS)) < ROPE_FREQS

    def norm_rope(v, g, scale):
        y = v * lax.rsqrt(jnp.mean(v * v, axis=-1, keepdims=True) + EPS) * g
        partner = jnp.where(first, pltpu.roll(y, HEAD_DIM - ROPE_FREQS, 1), pltpu.roll(y, ROPE_FREQS, 1))
        out = y * cos + partner * sin
        if scale is not None:
            out = out * scale
        return out.astype(BF16)

    half = N_Q_HEADS // 2
    for h in range(half):
        cols = slice(h * HEAD_DIM, (h + 1) * HEAD_DIM)
        q_out[:, cols] = norm_rope(qa_ref[:, cols], qn_ref[...], Q_SCALE)
        cols_b = slice((half + h) * HEAD_DIM, (half + h + 1) * HEAD_DIM)
        q_out[:, cols_b] = norm_rope(qb_ref[:, cols], qn_ref[...], Q_SCALE)
    for h in range(N_KV_HEADS):
        cols = slice(h * HEAD_DIM, (h + 1) * HEAD_DIM)
        k_out[:, cols] = norm_rope(kv_ref[:, cols], kn_ref[...], None)
        vt_out[cols, :] = kv_ref[:, KV_WIDTH + h * HEAD_DIM:KV_WIDTH + (h + 1) * HEAD_DIM].T.astype(BF16)


def _qkv(u, cos, sin, qn, kn, n_lat, nlc):
    t = u.shape[0]
    ch = SEQ_CHUNK
    cw = 512
    base = (A_IN + B_IN) // cw
    tab = lambda i: (jnp.where(i < n_lat // ch, i % nlc, nlc), 0)
    return pl.pallas_call(
        _qkv_kernel,
        grid=(t // ch,),
        in_specs=[
            pl.BlockSpec((ch, cw), lambda i: (i, base)),
            pl.BlockSpec((ch, cw), lambda i: (i, base + 1)),
            pl.BlockSpec((ch, cw), lambda i: (i, base + 2)),
            pl.BlockSpec((ch, HEAD_DIM), tab),
            pl.BlockSpec((ch, HEAD_DIM), tab),
            pl.BlockSpec((1, HEAD_DIM), lambda i: (0, 0)),
            pl.BlockSpec((1, HEAD_DIM), lambda i: (0, 0)),
        ],
        out_specs=[
            pl.BlockSpec((ch, ATT_WIDTH), lambda i: (i, 0)),
            pl.BlockSpec((ch, KV_WIDTH), lambda i: (i, 0)),
            pl.BlockSpec((KV_WIDTH, ch), lambda i: (0, i)),
        ],
        out_shape=[
            jax.ShapeDtypeStruct((t, ATT_WIDTH), BF16),
            jax.ShapeDtypeStruct((t, KV_WIDTH), BF16),
            jax.ShapeDtypeStruct((KV_WIDTH, t), BF16),
        ],
        compiler_params=_cparams("parallel"),
        name="qkv_prep",
    )(u, u, u, cos, sin, qn.reshape(1, HEAD_DIM), kn.reshape(1, HEAD_DIM))


def _row_groups(v, op):
    return op(v.reshape(v.shape[0] // SUBLANES, SUBLANES, v.shape[1]), axis=0)


def _attn_body(with_lat, q_ref, kc_ref, vtc_ref, kl_ref, vtl_ref, o_ref, acc_ref, s_ref):
    tq = q_ref.shape[0]
    tk = s_ref.shape[1]
    n_tiles = kl_ref.shape[0] // tk if with_lat else 0

    for g in range(N_KV_HEADS):
        gcols = slice(g * HEAD_DIM, (g + 1) * HEAD_DIM)
        qs = jnp.concatenate(
            [q_ref[:, (g * Q_PER_KV + h) * HEAD_DIM:(g * Q_PER_KV + h + 1) * HEAD_DIM] for h in range(Q_PER_KV)],
            axis=0)

        def softmax_pv(s, vtt, carry):
            m_new = jnp.max(_row_groups(s, jnp.max), axis=0, keepdims=True)
            if carry is not None:
                m_new = jnp.maximum(carry[0], m_new)
            p = jnp.exp2(s - m_new)
            pv = _dot(vtt, p.astype(BF16))
            if carry is None:
                acc_ref[...] = pv
                return m_new, _row_groups(p, jnp.sum)
            alpha = jnp.exp2(carry[0] - m_new)
            acc_ref[...] = alpha * acc_ref[...] + pv
            return m_new, alpha * carry[1] + _row_groups(p, jnp.sum)

        def scores(t):
            r0 = t * tk if isinstance(t, int) else pl.multiple_of(t * tk, tk)
            return _dot_nt(kl_ref[pl.ds(r0, tk), gcols], qs)

        def consume(slot, t, carry):
            r0 = t * tk if isinstance(t, int) else pl.multiple_of(t * tk, tk)
            return softmax_pv(s_ref[slot], vtl_ref[gcols, pl.ds(r0, tk)], carry)

        if n_tiles:
            s_ref[0] = scores(0)
        carry = softmax_pv(_dot_nt(kc_ref[:, gcols], qs), vtc_ref[gcols, :], None)
        n_pairs = max(n_tiles // 2 - 1, 0)
        if n_pairs:
            def pair(j, carry):
                t0 = 2 * j
                s_ref[1] = scores(t0 + 1)
                carry = consume(0, t0, carry)
                s_ref[0] = scores(t0 + 2)
                return consume(1, t0 + 1, carry)
            carry = lax.fori_loop(0, n_pairs, pair, carry)
        for t in range(2 * n_pairs, n_tiles):
            if t + 1 < n_tiles:
                s_ref[(t + 1) % 2] = scores(t + 1)
            carry = consume(t % 2, t, carry)
        ot = acc_ref[...] * (1.0 / jnp.sum(carry[1], axis=0, keepdims=True))
        for h in range(Q_PER_KV):
            cols = slice((g * Q_PER_KV + h) * HEAD_DIM, (g * Q_PER_KV + h + 1) * HEAD_DIM)
            o_ref[:, cols] = ot[:, h * tq:(h + 1) * tq].T.astype(BF16)


def _attn_kernel(lat_steps, ctx_steps, *refs):
    if ctx_steps == 0:
        _attn_body(True, *refs)
        return
    i = pl.program_id(1)

    @pl.when(i < lat_steps)
    def _():
        _attn_body(True, *refs)

    @pl.when(i >= lat_steps)
    def _():
        _attn_body(False, *refs)


def _attn(q, k, vt, nb, n, nc, with_ctx_queries):
    n_lat = nb * n
    tq = ATT_QUERY_TILE
    lat_steps = n // tq
    ctx_steps = nc // tq if with_ctx_queries else 0
    rows = n_lat + (nb * nc if with_ctx_queries else 0)

    def q_map(b, i):
        return (jnp.where(i < lat_steps, b * lat_steps + i, n_lat // tq + b * ctx_steps + (i - lat_steps)), 0)

    m = Q_PER_KV * tq
    tk = min(ATT_KEY_TILE, n)
    assert n % tk == 0 and n % tq == 0 and nc % tq == 0
    return pl.pallas_call(
        functools.partial(_attn_kernel, lat_steps, ctx_steps),
        grid=(nb, lat_steps + ctx_steps),
        in_specs=[pl.BlockSpec((tq, ATT_WIDTH), q_map),
                  pl.BlockSpec((nc, KV_WIDTH), lambda b, i: (n_lat // nc + b, 0)),
                  pl.BlockSpec((KV_WIDTH, nc), lambda b, i: (0, n_lat // nc + b)),
                  pl.BlockSpec((n, KV_WIDTH), lambda b, i: (b, 0)),
                  pl.BlockSpec((KV_WIDTH, n), lambda b, i: (0, b))],
        out_specs=pl.BlockSpec((tq, ATT_WIDTH), q_map),
        out_shape=jax.ShapeDtypeStruct((rows, ATT_WIDTH), BF16),
        scratch_shapes=[pltpu.VMEM((HEAD_DIM, m), F32), pltpu.VMEM((2, tk, m), F32)],
        compiler_params=_cparams("parallel", "arbitrary"),
        name="attention",
    )(q, k, vt, k, vt)


def _outproj_kernel(routed, lat_tiles, *refs):
    if routed:
        ra_ref, hg_ref, at_ref, xl_ref, xc_ref, mod_ref, w_ref, g_ref, rt_ref, xo_ref, h_ref, lg_ref = refs
    else:
        ra_ref, hg_ref, at_ref, xl_ref, xc_ref, mod_ref, w_ref, g_ref, xo_ref, h_ref = refs
    w1 = ra_ref.shape[1]
    w2 = w1 + hg_ref.shape[1]
    mix = (_dot(ra_ref[...], w_ref[0:w1, :]) + _dot(hg_ref[...], w_ref[w1:w2, :])
           + _dot(at_ref[...], w_ref[w2:, :]))
    m = mod_ref[0]
    x_in = jnp.where(pl.program_id(0) < lat_tiles, xl_ref[...], xc_ref[...])
    x = x_in + m[2:3] * mix
    xo_ref[...] = x
    y = x * lax.rsqrt(jnp.mean(x * x, axis=-1, keepdims=True) + EPS) * g_ref[...]
    h = y * (1.0 + m[4:5]) + m[3:4]
    if routed:
        h_ref[...] = h
        h_hi = h.astype(BF16)
        h_lo = (h - h_hi.astype(F32)).astype(BF16)
        lg = _dot(h_hi, rt_ref[...]) + _dot(h_lo, rt_ref[...])
        lg_ref[...] = lg[:, 0:LANES] + lg[:, LANES:2 * LANES]
    else:
        h_ref[...] = h.astype(BF16)


def _outproj(ra, hg, at, x_lat, x_ctx, ctx_base, mod, w, g, router, n_tok, n_lat, seq, nb):
    d = x_lat.shape[1]
    tm = 256
    routed = router is not None
    row = functools.partial(_mod_row, tile=tm, n_lat=n_lat, seq=seq, nb=nb)
    tok = lambda width: pl.BlockSpec((tm, width), lambda i: (i, 0))
    lat_spec, ctx_spec = _stream_specs(tm, d, n_lat // tm, ctx_base // tm)
    in_specs = [tok(ra.shape[1]), tok(hg.shape[1]), tok(at.shape[1]), lat_spec, ctx_spec,
                pl.BlockSpec((1, N_MOD, d), lambda i: (row(i), 0, 0)),
                pl.BlockSpec(w.shape, lambda i: (0, 0)),
                pl.BlockSpec((1, d), lambda i: (0, 0))]
    args = [ra, hg, at, x_lat, x_ctx, mod, w, g.reshape(1, d)]
    out_specs = [tok(d), tok(d)]
    out_shape = [jax.ShapeDtypeStruct((n_tok, d), F32), jax.ShapeDtypeStruct((n_tok, d), F32 if routed else BF16)]
    if routed:
        in_specs.append(pl.BlockSpec((d, 2 * LANES), lambda i: (0, 0)))
        args.append(router)
        out_specs.append(tok(LANES))
        out_shape.append(jax.ShapeDtypeStruct((n_tok, LANES), F32))
    return pl.pallas_call(
        functools.partial(_outproj_kernel, routed, n_lat // tm),
        grid=(n_tok // tm,),
        in_specs=in_specs,
        out_specs=out_specs,
        out_shape=out_shape,
        compiler_params=_cparams("parallel"),
        name="outproj",
    )(*args)


def _ffn_kernel(h_ref, x_ref, mod_ref, w1_ref, w3_ref, w2_ref, o_ref, acc_ref):
    k = pl.program_id(1)

    @pl.when(k == 0)
    def _():
        acc_ref[...] = jnp.zeros_like(acc_ref)

    h = h_ref[...]
    a = _dot(h, w1_ref[...])
    z = _silu(a) * _dot(h, w3_ref[...])
    acc_ref[...] += _dot(z.astype(BF16), w2_ref[...])

    @pl.when(k == pl.num_programs(1) - 1)
    def _():
        o_ref[...] = x_ref[...] + mod_ref[0][5:6] * acc_ref[...]


def _ffn(h, xs, mod, w1, w3, w2, n_lat, seq, nb):
    t, d = h.shape
    f = w1.shape[1]
    tm = TOKEN_TILE
    tf = 512
    row = functools.partial(_mod_row, tile=tm, n_lat=n_lat, seq=seq, nb=nb)
    return pl.pallas_call(
        _ffn_kernel,
        grid=(t // tm, f // tf),
        in_specs=[
            pl.BlockSpec((tm, d), lambda i, k: (i, 0)),
            pl.BlockSpec((tm, d), lambda i, k: (i, 0)),
            pl.BlockSpec((1, N_MOD, d), lambda i, k: (row(i), 0, 0)),
            pl.BlockSpec((d, tf), lambda i, k: (0, k)),
            pl.BlockSpec((d, tf), lambda i, k: (0, k)),
            pl.BlockSpec((tf, d), lambda i, k: (k, 0)),
        ],
        out_specs=pl.BlockSpec((tm, d), lambda i, k: (i, 0)),
        out_shape=jax.ShapeDtypeStruct((t, d), F32),
        scratch_shapes=[pltpu.VMEM((tm, d), F32)],
        compiler_params=_cparams("parallel", "arbitrary"),
        name="ffn_dense",
    )(h, xs, mod, w1, w3, w2)


def _router_kernel(lg_ref, meta_ref, wts_ref, cnt_ref, run_ref):
    @pl.when(pl.program_id(0) == 0)
    def _():
        run_ref[...] = jnp.zeros_like(run_ref)

    lg = lg_ref[...]
    tm = lg.shape[0]
    lane = lax.broadcasted_iota(jnp.int32, lg.shape, 1)
    lane_f = lane.astype(F32)
    v = jnp.where(lane < N_EXPERTS, lg, -jnp.inf)
    m1 = jnp.max(v, axis=-1, keepdims=True)
    i1 = jnp.min(jnp.where(v == m1, lane_f, float(LANES)), axis=-1, keepdims=True)
    v2 = jnp.where(lane_f == i1, -jnp.inf, v)
    m2 = jnp.max(v2, axis=-1, keepdims=True)
    i2 = jnp.min(jnp.where(v2 == m2, lane_f, float(LANES)), axis=-1, keepdims=True)
    e = jnp.exp(m2 - m1)
    wt1 = 1.0 / (1.0 + e)
    wt2 = e / (1.0 + e)
    hit1 = lane_f == i1
    hit2 = lane_f == i2
    assign = jnp.where(hit1 | hit2, 1.0, 0.0)
    r = lax.broadcasted_iota(jnp.int32, (tm, tm), 0)
    c = lax.broadcasted_iota(jnp.int32, (tm, tm), 1)
    tri = jnp.where(r > c, 1.0, 0.0).astype(BF16)
    rank = _dot(tri, assign.astype(BF16)) + run_ref[0:1, :]
    r1 = jnp.sum(jnp.where(hit1, rank, 0.0), axis=-1, keepdims=True)
    r2 = jnp.sum(jnp.where(hit2, rank, 0.0), axis=-1, keepdims=True)
    run_ref[...] = run_ref[...] + jnp.sum(assign, axis=0, keepdims=True)
    meta = jnp.where(lane == 0, i1, jnp.where(lane == 1, i2, jnp.where(lane == 2, r1, jnp.where(lane == 3, r2, 0.0))))
    meta_ref[...] = meta.astype(jnp.int32)
    wts_ref[...] = jnp.where(lane == 0, wt1, jnp.where(lane == 1, wt2, 0.0))
    cnt_ref[...] = run_ref[...]


def _router(logits):
    t = logits.shape[0]
    tm = TOKEN_TILE
    tok = pl.BlockSpec((tm, LANES), lambda i: (i, 0))
    return pl.pallas_call(
        _router_kernel,
        grid=(t // tm,),
        in_specs=[tok],
        out_specs=[tok, tok, pl.BlockSpec((SUBLANES, LANES), lambda i: (0, 0))],
        out_shape=[jax.ShapeDtypeStruct((t, LANES), jnp.int32), jax.ShapeDtypeStruct((t, LANES), F32),
                   jax.ShapeDtypeStruct((SUBLANES, LANES), F32)],
        scratch_shapes=[pltpu.VMEM((SUBLANES, LANES), F32)],
        compiler_params=_cparams("arbitrary"),
        name="router",
    )(logits)


def _row_copy(src_ref, src_row, dst_ref, dst_row, sem):
    return pltpu.make_async_copy(src_ref.at[pl.ds(src_row, 1)], dst_ref.at[pl.ds(dst_row, 1)], sem)


def _dispatch_kernel(pos_ref, h_ref, xs_in_ref, xs_ref, pos_smem, hbuf, sem_p, sem_h, sem_d):
    del xs_in_ref
    i = pl.program_id(0)
    n = pl.num_programs(0)
    tm = hbuf.shape[1]
    cur = i % 3

    def tile_copy(tile, buf):
        return pltpu.make_async_copy(h_ref.at[pl.ds(pl.multiple_of(tile * tm, tm), tm)], hbuf.at[buf], sem_h.at[buf])

    @pl.when(i == 0)
    def _():
        tile_copy(0, 0).start()

    @pl.when(i < n - 1)
    def _():
        tile_copy(i + 1, (i + 1) % 3).start()

    cp = pltpu.make_async_copy(pos_ref.at[0, 0], pos_smem, sem_p)
    cp.start()
    cp.wait()
    tile_copy(i, cur).wait()

    def issue(t, carry):
        _row_copy(hbuf.at[cur], t, xs_ref, pos_smem[t], sem_d.at[cur]).start()
        _row_copy(hbuf.at[cur], t, xs_ref, pos_smem[tm + t], sem_d.at[cur]).start()
        return carry

    lax.fori_loop(0, tm, issue, 0, unroll=8)

    def drain(buf):
        def body(t, carry):
            _row_copy(hbuf.at[buf], 0, xs_ref, 0, sem_d.at[buf]).wait()
            _row_copy(hbuf.at[buf], 0, xs_ref, 0, sem_d.at[buf]).wait()
            return carry
        lax.fori_loop(0, tm, body, 0, unroll=8)

    @pl.when(i > 0)
    def _():
        drain((i + 2) % 3)

    @pl.when(i == n - 1)
    def _():
        drain(cur)


def _dispatch(pos, h, xs0):
    t, d = h.shape
    tm = 256
    nt = t // tm
    return pl.pallas_call(
        _dispatch_kernel,
        grid=(nt,),
        in_specs=[
            pl.BlockSpec((1, 1, 2 * tm), lambda i: (i, 0, 0)),
            pl.BlockSpec(memory_space=pl.ANY),
            pl.BlockSpec(memory_space=pl.ANY),
        ],
        out_specs=pl.BlockSpec(memory_space=pl.ANY),
        out_shape=jax.ShapeDtypeStruct(xs0.shape, xs0.dtype),
        scratch_shapes=[pltpu.SMEM((2 * tm,), jnp.int32), pltpu.VMEM((3, tm, d), F32),
                        pltpu.SemaphoreType.DMA(()), pltpu.SemaphoreType.DMA((3,)), pltpu.SemaphoreType.DMA((3,))],
        input_output_aliases={2: 0},
        compiler_params=_cparams("arbitrary"),
        name="moe_dispatch",
    )(pos, h, xs0)


def _expert_kernel(te_ref, tv_ref, x_ref, w1_ref, w3_ref, w2_ref, y_ref, xb_ref, acc_ref):
    del te_ref
    i = pl.program_id(0)
    k = pl.program_id(1)

    @pl.when(tv_ref[i] > 0)
    def _():
        @pl.when(k == 0)
        def _():
            xb_ref[...] = x_ref[...].astype(BF16)
            acc_ref[...] = jnp.zeros_like(acc_ref)

        xb = xb_ref[...]
        a = _dot(xb, w1_ref[...])
        z = _silu(a) * _dot(xb, w3_ref[...])
        acc_ref[...] += _dot(z.astype(BF16), w2_ref[...])

        @pl.when(k == pl.num_programs(1) - 1)
        def _():
            y_ref[...] = acc_ref[...]

    @pl.when((tv_ref[i] == 0) & (k == 0))
    def _():
        y_ref[...] = jnp.zeros_like(y_ref)


def _experts(tile_expert, tile_valid, tile_row, xs, w1, w3, w2):
    rows, d = xs.shape
    f = w1.shape[2]
    tm = EXPERT_TILE
    tf = 512
    grid_spec = pltpu.PrefetchScalarGridSpec(
        num_scalar_prefetch=3,
        grid=(rows // tm, f // tf),
        in_specs=[
            pl.BlockSpec((tm, d), lambda i, k, te, tv, tr: (tr[i], 0)),
            pl.BlockSpec((None, d, tf), lambda i, k, te, tv, tr: (te[i], 0, jnp.where(tv[i] > 0, k, f // tf - 1))),
            pl.BlockSpec((None, d, tf), lambda i, k, te, tv, tr: (te[i], 0, jnp.where(tv[i] > 0, k, f // tf - 1))),
            pl.BlockSpec((None, tf, d), lambda i, k, te, tv, tr: (te[i], jnp.where(tv[i] > 0, k, f // tf - 1), 0)),
        ],
        out_specs=pl.BlockSpec((tm, d), lambda i, k, te, tv, tr: (i, 0)),
        scratch_shapes=[pltpu.VMEM((tm, d), BF16), pltpu.VMEM((tm, d), F32)],
    )

    def body(te_ref, tv_ref, tr_ref, *rest):
        del tr_ref
        _expert_kernel(te_ref, tv_ref, *rest)

    return pl.pallas_call(
        body,
        grid_spec=grid_spec,
        out_shape=jax.ShapeDtypeStruct((rows, d), F32),
        compiler_params=_cparams("arbitrary", "arbitrary"),
        name="moe_experts",
    )(tile_expert, tile_valid, tile_row, xs, w1, w3, w2)


def _combine_kernel(pos0_ref, posn_ref, wts_ref, x_ref, mod_ref, ys_ref, o_ref, pos_smem, y_ref, sem_p, sem_g):
    i = pl.program_id(0)
    n = pl.num_programs(0)
    tm = x_ref.shape[0]
    cur = i % 2

    def gather(pos_ref, slot):
        cp = pltpu.make_async_copy(pos_ref.at[0, 0], pos_smem, sem_p)
        cp.start()
        cp.wait()

        def issue(t, carry):
            _row_copy(ys_ref, pos_smem[t], y_ref.at[slot, 0], t, sem_g.at[slot]).start()
            _row_copy(ys_ref, pos_smem[tm + t], y_ref.at[slot, 1], t, sem_g.at[slot]).start()
            return carry

        lax.fori_loop(0, tm, issue, 0, unroll=8)

    @pl.when(i == 0)
    def _():
        gather(pos0_ref, 0)

    @pl.when(i < n - 1)
    def _():
        gather(posn_ref, 1 - cur)

    def drain(t, carry):
        _row_copy(ys_ref, 0, y_ref.at[cur, 0], 0, sem_g.at[cur]).wait()
        _row_copy(ys_ref, 0, y_ref.at[cur, 1], 0, sem_g.at[cur]).wait()
        return carry

    lax.fori_loop(0, tm, drain, 0, unroll=8)
    wts = wts_ref[...]
    f = wts[:, 0:1] * y_ref[cur, 0] + wts[:, 1:2] * y_ref[cur, 1]
    o_ref[...] = x_ref[...] + mod_ref[0][5:6] * f


def _combine(pos, wts, xs, mod, ys, n_lat, seq, nb):
    t, d = xs.shape
    tm = 256
    row = functools.partial(_mod_row, tile=tm, n_lat=n_lat, seq=seq, nb=nb)
    return pl.pallas_call(
        _combine_kernel,
        grid=(t // tm,),
        in_specs=[
            pl.BlockSpec((1, 1, 2 * tm), lambda i: (0, 0, 0)),
            pl.BlockSpec((1, 1, 2 * tm), lambda i: (jnp.minimum(i + 1, t // tm - 1), 0, 0)),
            pl.BlockSpec((tm, LANES), lambda i: (i, 0)),
            pl.BlockSpec((tm, d), lambda i: (i, 0)),
            pl.BlockSpec((1, N_MOD, d), lambda i: (row(i), 0, 0)),
            pl.BlockSpec(memory_space=pl.ANY),
        ],
        out_specs=pl.BlockSpec((tm, d), lambda i: (i, 0)),
        out_shape=jax.ShapeDtypeStruct((t, d), F32),
        scratch_shapes=[pltpu.SMEM((2 * tm,), jnp.int32), pltpu.VMEM((2, 2, tm, d), F32),
                        pltpu.SemaphoreType.DMA(()), pltpu.SemaphoreType.DMA((2,))],
        compiler_params=_cparams("arbitrary"),
        name="moe_combine",
    )(pos, pos, wts, xs, mod, ys)


def _moe(h, logits, xs, mod, w1, w3, w2, n_lat, seq, nb):
    t, d = h.shape
    meta, wts, counts = _router(logits)
    te_rows = EXPERT_TILE
    cnt = counts[0, :N_EXPERTS].astype(jnp.int32)
    padded = ((cnt + te_rows - 1) // te_rows) * te_rows
    ends = jnp.cumsum(padded)
    starts = ends - padded
    pos1 = jnp.take(starts, meta[:, 0]) + meta[:, 2]
    pos2 = jnp.take(starts, meta[:, 1]) + meta[:, 3]
    tmd = 256
    pos = jnp.concatenate([pos1.reshape(t // tmd, 1, tmd), pos2.reshape(t // tmd, 1, tmd)], axis=-1)
    n_tiles = (2 * t) // te_rows + N_EXPERTS
    tile_start = jnp.arange(n_tiles, dtype=jnp.int32) * te_rows
    n_valid = ends[-1] // te_rows
    tile_valid = (tile_start < ends[-1]).astype(jnp.int32)
    tile_row = jnp.minimum(jnp.arange(n_tiles, dtype=jnp.int32), n_valid - 1)
    tile_expert = jnp.sum((tile_row[:, None] * te_rows >= ends[None, :]).astype(jnp.int32), axis=1)
    xs0 = jnp.zeros((n_tiles * te_rows, d), F32)
    x_sorted = _dispatch(pos, h, xs0)
    y_sorted = _experts(tile_expert, tile_valid, tile_row, x_sorted, w1, w3, w2)
    return _combine(pos, wts, xs, mod, y_sorted, n_lat, seq, nb)


def kernel(x, c, ctx, c_ctx, w_mod, b_mod, norm1, norm2, w_in, w_out, conv_w, conv_b, lru_wa, lru_ba, lru_wi, lru_bi, lru_lambda, hgrn_lb_logits, hgrn_gnorm, q_norm, k_norm, ffn_w1, ffn_w3, ffn_w2, router, moe_w1, moe_w3, moe_w2):
    nb, n, d = x.shape
    nc = ctx.shape[1]
    depth = w_mod.shape[0]
    assert n % SEQ_CHUNK == 0 and nc % SEQ_CHUNK == 0 and n % TOKEN_TILE == 0
    assert (nb * nc) % TOKEN_TILE == 0 and n % GRID_W == 0
    n_lat = nb * n
    nlc = n // SEQ_CHUNK
    ncc = nc // SEQ_CHUNK

    n_all = n_lat + nb * nc
    x_lat, x_ctx, ctx_base = x.reshape(n_lat, d), ctx.reshape(nb * nc, d), 0
    mod_rows = 2 * SUBLANES * ((nb + 1 + 2 * SUBLANES - 1) // (2 * SUBLANES))
    cc = jnp.zeros((mod_rows, d), F32).at[:nb].set(c).at[nb].set(c_ctx)
    mod_all = _mod_table(cc, w_mod, b_mod).reshape(depth, mod_rows, N_MOD, d)
    cos, sin = _rope_tables(n)
    router_f = jnp.pad(router, ((0, 0), (0, 0), (0, LANES - router.shape[-1])))
    router_hi = router_f.astype(BF16)
    router_lo = (router_f - router_hi.astype(F32)).astype(BF16)
    router_p = jnp.concatenate([router_hi, router_lo], axis=-1)

    for l in range(depth):
        need_ctx = l < depth - 1
        mod = mod_all[l]
        u = _inproj(x_lat, x_ctx, ctx_base, n_all, mod, norm1[l], _to_bf16(w_in, l), n_lat, n, nb)

        lru_args = lambda dd: (conv_w[l], conv_b[l], lru_wa[l, dd], lru_ba[l, dd], lru_wi[l, dd], lru_bi[l, dd],
                               lru_lambda[l, dd])
        hf = _lru_pass(False, u, *lru_args(0), None, nb, ncc, nlc)
        ra = _lru_pass(True, u, *lru_args(1), hf, nb, ncc, nlc)

        of = _hgrn_pass(False, l, u, hgrn_lb_logits[0], hgrn_gnorm[l], None, nb, ncc, nlc)
        hg = _hgrn_pass(True, l, u, hgrn_lb_logits[1], hgrn_gnorm[l], of, nb, ncc, nlc)

        q, k, vt = _qkv(u, cos, sin, q_norm[l], k_norm[l], n_lat, nlc)
        at = _attn(q, k, vt, nb, n, nc, need_ctx)

        n_tok = n_all if need_ctx else n_lat
        j = l // 2
        routed = l % 2 == 1
        outs = _outproj(ra, hg, at, x_lat, x_ctx, ctx_base, mod, _to_bf16(w_out, l), norm2[l],
                        router_p[j] if routed else None, n_tok, n_lat, n, nb)
        if routed:
            xn, h2, logits = outs
            xs = _moe(h2, logits, xn, mod, _to_bf16(moe_w1, j), _to_bf16(moe_w3, j), _to_bf16(moe_w2, j),
                      n_lat, n, nb)
        else:
            xn, h2 = outs
            xs = _ffn(h2, xn, mod, _to_bf16(ffn_w1, j), _to_bf16(ffn_w3, j), _to_bf16(ffn_w2, j), n_lat, n, nb)
        x_lat, x_ctx, ctx_base = xs, xs, n_lat
    return xs[:n_lat].reshape(nb, n, d)
```

```python
import functools

import jax
import jax.numpy as jnp
from jax import lax
from jax.experimental import pallas as pl
from jax.experimental.pallas import tpu as pltpu

F32 = jnp.float32
BF16 = jnp.bfloat16

EPS = 1e-6
N_MOD = 6
GRID_W = 64
LRU_WIDTH = 512
LRU_BLOCKS = 4
LRU_BLOCK = LRU_WIDTH // LRU_BLOCKS
LRU_C = 8.0
HG_HEADS = 4
HG_DK = 128
HG_WIDTH = HG_HEADS * HG_DK
HG_CHUNK = 64
HG_SUB = 8
HEAD_DIM = 128
N_Q_HEADS = 8
N_KV_HEADS = 2
Q_PER_KV = N_Q_HEADS // N_KV_HEADS
ATT_WIDTH = N_Q_HEADS * HEAD_DIM
KV_WIDTH = N_KV_HEADS * HEAD_DIM
ROPE_THETA = 10000.0
ROPE_FREQS = HEAD_DIM // 4
ATTN_SCALE = HEAD_DIM ** -0.5
LOG2E = 1.4426950408889634
Q_SCALE = ATTN_SCALE * LOG2E
A_IN = 2 * LRU_WIDTH
B_IN = 5 * HG_WIDTH
N_EXPERTS = 8

LANES = 128
SUBLANES = 8
SEQ_CHUNK = 256
TOKEN_TILE = 512
INPROJ_FLAT_TILE = 1024
EXPERT_TILE = 512
ATT_KEY_TILE = 1024
ATT_QUERY_TILE = 256
CAST_BLOCK_ELEMS = 2 * 1024 * 1024
VMEM_LIMIT = 56 * 1024 * 1024


def _cparams(*sem):
    return pltpu.CompilerParams(dimension_semantics=sem, vmem_limit_bytes=VMEM_LIMIT)


def _sigmoid(v):
    return 1.0 / (1.0 + jnp.exp(-v))


def _sigmoid_tanh(v):
    return 0.5 * jnp.tanh(0.5 * v) + 0.5


def _silu(v):
    return v * _sigmoid_tanh(v)


def _dot(a, b):
    return jnp.dot(a, b, preferred_element_type=F32)


def _dot_nt(a, b):
    return lax.dot_general(a, b, (((1,), (1,)), ((), ())), preferred_element_type=F32)


def _dot_tn(a, b):
    return lax.dot_general(a, b, (((0,), (0,)), ((), ())), preferred_element_type=F32)


def _mod_row(i, tile, n_lat, seq, nb):
    return jnp.where(i < n_lat // tile, (i * tile) // seq, nb)


def _cast_kernel(w_ref, o_ref):
    o_ref[...] = w_ref[...].astype(BF16)


def _to_bf16(w, index):
    inner = w.shape[1:]
    cols = inner[-1]
    rows = 1
    for s in inner[:-1]:
        rows *= s
    tr = 2 * SUBLANES
    while tr * 2 * cols <= CAST_BLOCK_ELEMS and rows % (tr * 2) == 0:
        tr *= 2
    assert rows % tr == 0
    steps = rows // tr
    out = pl.pallas_call(
        _cast_kernel,
        grid=(steps,),
        in_specs=[pl.BlockSpec((tr, cols), lambda i: (index * steps + i, 0))],
        out_specs=pl.BlockSpec((tr, cols), lambda i: (i, 0)),
        out_shape=jax.ShapeDtypeStruct((rows, cols), BF16),
        compiler_params=_cparams("parallel"),
        name="to_bf16",
    )(w.reshape(w.shape[0] * rows, cols))
    return out.reshape(inner)


def _mod_kernel(c_ref, w_ref, b_ref, o_ref):
    s = _silu(c_ref[...]).astype(BF16)
    o_ref[0] = _dot(s, w_ref[0].astype(BF16)) + b_ref[0]


def _mod_table(cc, w_mod, b_mod):
    nl, d, md = w_mod.shape
    rows = cc.shape[0]
    tn = 1024 if md % 1024 == 0 else md
    return pl.pallas_call(
        _mod_kernel,
        grid=(nl, md // tn),
        in_specs=[
            pl.BlockSpec((rows, d), lambda l, j: (0, 0)),
            pl.BlockSpec((1, d, tn), lambda l, j: (l, 0, j)),
            pl.BlockSpec((1, 1, tn), lambda l, j: (l, 0, j)),
        ],
        out_specs=pl.BlockSpec((1, rows, tn), lambda l, j: (l, 0, j)),
        out_shape=jax.ShapeDtypeStruct((nl, rows, md), F32),
        compiler_params=_cparams("parallel", "parallel"),
        name="mod_table",
    )(cc, w_mod, b_mod.reshape(nl, 1, md))


def _stream_specs(tm, d, lat_tiles, ctx_base):
    lat = pl.BlockSpec((tm, d), lambda i, *_: (jnp.minimum(i, lat_tiles - 1), 0))
    ctx = pl.BlockSpec((tm, d), lambda i, *_: (ctx_base + jnp.maximum(i - lat_tiles, 0), 0))
    return lat, ctx


def _inproj_kernel(lat_tiles, *refs):
    if lat_tiles is None:
        x_ref, mod_ref, g_ref, w_ref, o_ref, h_ref = refs
    else:
        xl_ref, xc_ref, mod_ref, g_ref, w_ref, o_ref, h_ref = refs
    i = pl.program_id(0)
    first = pl.program_id(1) == 0

    def norm_mod(x_ref):
        x = x_ref[...]
        y = x * lax.rsqrt(jnp.mean(x * x, axis=-1, keepdims=True) + EPS) * g_ref[...]
        m = mod_ref[0]
        h_ref[...] = (y * (1.0 + m[1:2]) + m[0:1]).astype(BF16)

    if lat_tiles is None:
        @pl.when(first)
        def _():
            norm_mod(x_ref)
    else:
        @pl.when(first & (i < lat_tiles))
        def _():
            norm_mod(xl_ref)

        @pl.when(first & (i >= lat_tiles))
        def _():
            norm_mod(xc_ref)

    o_ref[...] = _dot(h_ref[...], w_ref[...])


def _inproj(x_lat, x_ctx, ctx_base, n_tok, mod, g, w, n_lat, seq, nb):
    d = x_lat.shape[1]
    width = w.shape[1]
    tn = 1024
    flat = x_lat is x_ctx
    if flat:
        tm = INPROJ_FLAT_TILE if (seq % INPROJ_FLAT_TILE == 0 and (n_tok - n_lat) % INPROJ_FLAT_TILE == 0) \
            else TOKEN_TILE
        x_specs = [pl.BlockSpec((tm, d), lambda i, j: (i, 0))]
        x_args = [x_lat]
    else:
        tm = TOKEN_TILE
        x_specs = list(_stream_specs(tm, d, n_lat // tm, ctx_base // tm))
        x_args = [x_lat, x_ctx]
    row = functools.partial(_mod_row, tile=tm, n_lat=n_lat, seq=seq, nb=nb)
    return pl.pallas_call(
        functools.partial(_inproj_kernel, None if flat else n_lat // tm),
        grid=(n_tok // tm, width // tn),
        in_specs=x_specs + [
            pl.BlockSpec((1, N_MOD, d), lambda i, j: (row(i), 0, 0)),
            pl.BlockSpec((1, d), lambda i, j: (0, 0)),
            pl.BlockSpec((d, tn), lambda i, j: (0, j)),
        ],
        out_specs=pl.BlockSpec((tm, tn), lambda i, j: (i, j)),
        out_shape=jax.ShapeDtypeStruct((n_tok, width), F32),
        scratch_shapes=[pltpu.VMEM((tm, d), BF16)],
        compiler_params=_cparams("parallel", "arbitrary"),
        name="inproj",
    )(*x_args, mod, g.reshape(1, d), w)


def _seq_pos(j, rev, ncc, nlc):
    is_ctx = j < ncc
    jl = j - ncc
    cc = (ncc - 1 - j) if rev else j
    cl = (nlc - 1 - jl) if rev else jl
    return is_ctx, cc, cl


def _seq_blk(b, j, rev, ncc, nlc, nb):
    is_ctx, cc, cl = _seq_pos(j, rev, ncc, nlc)
    return jnp.where(is_ctx, nb * nlc + b * ncc + cc, b * nlc + cl)


def _scan_rows(a_ref, b_ref, h_ref, hc_ref, rev):
    n_groups = a_ref.shape[0] // SUBLANES
    width = a_ref.shape[1]
    rows = lax.broadcasted_iota(jnp.int32, (SUBLANES, width), 0)

    def body(g, hc):
        gi = (n_groups - 1 - g) if rev else g
        r0 = pl.multiple_of(gi * SUBLANES, SUBLANES)
        a = a_ref[pl.ds(r0, SUBLANES), :]
        b = b_ref[pl.ds(r0, SUBLANES), :]
        for s in (1, 2, 4):
            if rev:
                a_s = pltpu.roll(a, SUBLANES - s, 0)
                b_s = pltpu.roll(b, SUBLANES - s, 0)
                keep = rows < SUBLANES - s
            else:
                a_s = pltpu.roll(a, s, 0)
                b_s = pltpu.roll(b, s, 0)
                keep = rows >= s
            b = a * jnp.where(keep, b_s, 0.0) + b
            a = a * jnp.where(keep, a_s, 1.0)
        h = b + a * hc
        h_ref[pl.ds(r0, SUBLANES), :] = h
        return h[0:1] if rev else h[SUBLANES - 1:SUBLANES]

    hc_ref[...] = lax.fori_loop(0, n_groups, body, hc_ref[...], unroll=4)


def _lru_kernel(rev, ncc, nlc, *refs):
    if rev:
        (u_ref, prev_ref, next_ref, cw_ref, cb_ref, wa_ref, ba_ref, wi_ref, bi_ref, lam_ref,
         hf_ref, o_ref, xe_ref, a_ref, b_ref, h_ref, hc_ref) = refs
    else:
        (u_ref, prev_ref, next_ref, cw_ref, cb_ref, wa_ref, ba_ref, wi_ref, bi_ref, lam_ref,
         o_ref, xe_ref, a_ref, b_ref, hc_ref) = refs
        h_ref = o_ref
    j = pl.program_id(1)
    is_ctx, cc, cl = _seq_pos(j, rev, ncc, nlc)
    c = jnp.where(is_ctx, cc, cl)
    nseq = jnp.where(is_ctx, ncc, nlc)

    @pl.when(j == 0)
    def _():
        hc_ref[...] = jnp.zeros_like(hc_ref)

    ch = SEQ_CHUNK
    w = LRU_WIDTH
    xe_ref[SUBLANES:SUBLANES + ch, :] = u_ref[:, w:2 * w]
    xe_ref[0:SUBLANES, :] = jnp.where(c > 0, prev_ref[...], 0.0)
    xe_ref[SUBLANES + ch:2 * SUBLANES + ch, :] = jnp.where(c < nseq - 1, next_ref[...], 0.0)
    cw = cw_ref[...]
    xc = cb_ref[...]
    for tap in range(4):
        off = SUBLANES - 2 + tap
        xc = xc + cw[tap:tap + 1] * xe_ref[off:off + ch, :]
    xb = xc.astype(BF16)
    ra = jnp.concatenate(
        [_dot(xb[:, n * LRU_BLOCK:(n + 1) * LRU_BLOCK], wa_ref[n]) for n in range(LRU_BLOCKS)], axis=-1)
    ri = jnp.concatenate(
        [_dot(xb[:, n * LRU_BLOCK:(n + 1) * LRU_BLOCK], wi_ref[n]) for n in range(LRU_BLOCKS)], axis=-1)
    r = _sigmoid_tanh(ra + ba_ref[...])
    gate_i = _sigmoid_tanh(ri + bi_ref[...])
    neg_lam = -lam_ref[...]
    softplus = jnp.maximum(neg_lam, 0.0) + jnp.log1p(jnp.exp(-jnp.abs(neg_lam)))
    log_a = -LRU_C * r * softplus
    a = jnp.exp(log_a)
    a_ref[...] = a
    b_ref[...] = jnp.sqrt(-jnp.tanh(log_a) * (a * a + 1.0)) * (gate_i * xc)
    _scan_rows(a_ref, b_ref, h_ref, hc_ref, rev)
    if rev:
        y = u_ref[:, 0:w]
        gelu = 0.5 * y * (1.0 + jnp.tanh(0.7978845608028654 * (y + 0.044715 * (y * y * y))))
        o_ref[...] = (gelu * (hf_ref[...] + h_ref[...])).astype(BF16)


def _lru_pass(rev, u, cw, cb, wa, ba, wi, bi, lam, hf, nb, ncc, nlc):
    t = u.shape[0]
    ch = SEQ_CHUNK
    w = LRU_WIDTH
    r8 = ch // SUBLANES
    blk = functools.partial(_seq_blk, rev=rev, ncc=ncc, nlc=nlc, nb=nb)
    full2 = lambda b, j: (0, 0)
    in_specs = [
        pl.BlockSpec((ch, 2 * w), lambda b, j: (blk(b, j), 0)),
        pl.BlockSpec((SUBLANES, w), lambda b, j: (jnp.maximum(blk(b, j) * r8 - 1, 0), 1)),
        pl.BlockSpec((SUBLANES, w), lambda b, j: (jnp.minimum(blk(b, j) * r8 + r8, t // SUBLANES - 1), 1)),
        pl.BlockSpec((4, w), full2),
        pl.BlockSpec((1, w), full2),
        pl.BlockSpec((LRU_BLOCKS, LRU_BLOCK, LRU_BLOCK), lambda b, j: (0, 0, 0)),
        pl.BlockSpec((1, w), full2),
        pl.BlockSpec((LRU_BLOCKS, LRU_BLOCK, LRU_BLOCK), lambda b, j: (0, 0, 0)),
        pl.BlockSpec((1, w), full2),
        pl.BlockSpec((1, w), full2),
    ]
    args = [u, u, u, cw, cb.reshape(1, w), wa.astype(BF16), ba.reshape(1, w), wi.astype(BF16),
            bi.reshape(1, w), lam.reshape(1, w)]
    scratch = [pltpu.VMEM((ch + 2 * SUBLANES, w), F32), pltpu.VMEM((ch, w), F32), pltpu.VMEM((ch, w), F32)]
    if rev:
        in_specs.append(pl.BlockSpec((ch, w), lambda b, j: (blk(b, j), 0)))
        args.append(hf)
        scratch.append(pltpu.VMEM((ch, w), F32))
        out_dtype = BF16
    else:
        out_dtype = F32
    scratch.append(pltpu.VMEM((1, w), F32))
    return pl.pallas_call(
        functools.partial(_lru_kernel, rev, ncc, nlc),
        grid=(nb, ncc + nlc),
        in_specs=in_specs,
        out_specs=pl.BlockSpec((ch, w), lambda b, j: (blk(b, j), 0)),
        out_shape=jax.ShapeDtypeStruct((t, w), out_dtype),
        scratch_shapes=scratch,
        compiler_params=_cparams("parallel", "arbitrary"),
        name="lru_bwd" if rev else "lru_fwd",
    )(*args)


def _cumsum_rows(v, rev):
    n = v.shape[0]
    rows = lax.broadcasted_iota(jnp.int32, v.shape, 0)
    s = 1
    while s < n:
        if rev:
            v = v + jnp.where(rows < n - s, pltpu.roll(v, n - s, 0), 0.0)
        else:
            v = v + jnp.where(rows >= s, pltpu.roll(v, s, 0), 0.0)
        s *= 2
    return v


def _hgrn_diag_att(q, k, bc, lo, width, rev):
    n = q.shape[0]
    lane = lax.broadcasted_iota(jnp.int32, (n, width), 1)
    row = lax.broadcasted_iota(jnp.int32, (n, width), 0)
    att = jnp.zeros((n, width), F32)
    for s in range(n):
        w = q * (k[s:s + 1] * jnp.exp2(bc - bc[s:s + 1]))
        att = jnp.where(lane == lo + s, jnp.sum(w, axis=-1, keepdims=True), att)
    keep = (row + lo <= lane) if rev else (row + lo >= lane)
    return jnp.where(keep, att, 0.0)


def _hgrn_chunk(qraw, fl, v, lbh, st, rev):
    cs = HG_CHUNK
    sb = HG_SUB
    q = _silu(qraw)
    f = lbh + (1.0 - lbh) * _sigmoid(fl)
    k = 1.0 - f
    bc = _cumsum_rows(jnp.log(f) * LOG2E, rev)
    blast = bc[0:1] if rev else bc[cs - 1:cs]
    o = _dot_nt((q * jnp.exp2(bc)).astype(BF16), st.astype(BF16))
    kdec = k * jnp.exp2(blast - bc)
    st_new = st * jnp.exp2(blast) + _dot_tn(v.astype(BF16), kdec.astype(BF16))
    att_rows = []
    for i in range(cs // sb):
        lo, hi = i * sb, (i + 1) * sb
        att_i = _hgrn_diag_att(q[lo:hi], k[lo:hi], bc[lo:hi], lo, cs, rev)
        if (hi < cs) if rev else (lo > 0):
            mref = bc[hi:hi + 1] if rev else bc[lo - 1:lo]
            plo, phi = (hi, cs) if rev else (0, lo)
            qhat = q[lo:hi] * jnp.exp2(bc[lo:hi] - mref)
            kp = k[plo:phi] * jnp.exp2(mref - bc[plo:phi])
            pad = jnp.zeros((cs - (phi - plo), HG_DK), F32)
            kp = jnp.concatenate([pad, kp] if rev else [kp, pad], axis=0)
            att_i = att_i + _dot_nt(qhat.astype(BF16), kp.astype(BF16))
        att_rows.append(att_i)
    att = jnp.concatenate(att_rows, axis=0)
    return o + _dot(att.astype(BF16), v.astype(BF16)), st_new


def _hgrn_kernel(rev, layer, *refs):
    if rev:
        q_ref, f_ref, v_ref, lbl_ref, og_ref, of_ref, gn_ref, o_ref, st_ref = refs
    else:
        q_ref, f_ref, v_ref, lbl_ref, o_ref, st_ref = refs

    @pl.when(pl.program_id(1) == 0)
    def _():
        st_ref[...] = jnp.zeros_like(st_ref)

    lg = lbl_ref[...]
    e = jnp.exp(lg - jnp.max(lg, axis=0, keepdims=True))
    p = e / jnp.sum(e, axis=0, keepdims=True)
    lb = jnp.zeros_like(p[0:1])
    for jl in range(1, layer + 1):
        lb = lb + p[jl:jl + 1]

    cs = HG_CHUNK
    n_chunks = q_ref.shape[0] // cs

    def body(ci, carry):
        cidx = (n_chunks - 1 - ci) if rev else ci
        r0 = pl.multiple_of(cidx * cs, cs)
        for h in range(HG_HEADS):
            cols = slice(h * HG_DK, (h + 1) * HG_DK)
            o, st_new = _hgrn_chunk(q_ref[pl.ds(r0, cs), cols], f_ref[pl.ds(r0, cs), cols],
                                    v_ref[pl.ds(r0, cs), cols], lb[:, cols], st_ref[h], rev)
            st_ref[h] = st_new
            if rev:
                o = o + of_ref[pl.ds(r0, cs), cols]
                y = o * lax.rsqrt(jnp.mean(o * o, axis=-1, keepdims=True) + EPS) * gn_ref[...]
                o_ref[pl.ds(r0, cs), cols] = (y * _silu(og_ref[pl.ds(r0, cs), cols])).astype(BF16)
            else:
                o_ref[pl.ds(r0, cs), cols] = o
        return carry

    lax.fori_loop(0, n_chunks, body, 0, unroll=4)


def _hgrn_pass(rev, layer, u, lb_logits, gnorm, of, nb, ncc, nlc):
    t = u.shape[0]
    ch = SEQ_CHUNK
    w = HG_WIDTH
    nl = lb_logits.shape[0]
    blk = functools.partial(_seq_blk, rev=rev, ncc=ncc, nlc=nlc, nb=nb)
    base = A_IN // w
    col = lambda cb: pl.BlockSpec((ch, w), lambda b, j: (blk(b, j), cb))
    in_specs = [col(base), col(base + 1 + (1 if rev else 0)), col(base + 3),
                pl.BlockSpec((nl, w), lambda b, j: (0, 0))]
    args = [u, u, u, lb_logits]
    if rev:
        in_specs += [col(base + 4), pl.BlockSpec((ch, w), lambda b, j: (blk(b, j), 0)),
                     pl.BlockSpec((1, HG_DK), lambda b, j: (0, 0))]
        args += [u, of, gnorm.reshape(1, HG_DK)]
    return pl.pallas_call(
        functools.partial(_hgrn_kernel, rev, layer),
        grid=(nb, ncc + nlc),
        in_specs=in_specs,
        out_specs=pl.BlockSpec((ch, w), lambda b, j: (blk(b, j), 0)),
        out_shape=jax.ShapeDtypeStruct((t, w), BF16 if rev else F32),
        scratch_shapes=[pltpu.VMEM((HG_HEADS, HG_DK, HG_DK), F32)],
        compiler_params=_cparams("parallel", "arbitrary"),
        name="hgrn_bwd" if rev else "hgrn_fwd",
    )(*args)


def _rope_tables(n):
    rows = n // GRID_W
    row = jnp.repeat(jnp.arange(rows, dtype=F32), GRID_W)
    colp = jnp.tile(jnp.arange(GRID_W, dtype=F32), rows)
    inv = ROPE_THETA ** (-jnp.arange(ROPE_FREQS, dtype=F32) / ROPE_FREQS)
    ar = row[:, None] * inv
    ac = colp[:, None] * inv
    cos = jnp.concatenate([jnp.cos(ar), jnp.cos(ar), jnp.cos(ac), jnp.cos(ac)], axis=-1)
    sin = jnp.concatenate([-jnp.sin(ar), jnp.sin(ar), -jnp.sin(ac), jnp.sin(ac)], axis=-1)
    cos = jnp.concatenate([cos, jnp.ones((SEQ_CHUNK, HEAD_DIM), F32)], axis=0)
    sin = jnp.concatenate([sin, jnp.zeros((SEQ_CHUNK, HEAD_DIM), F32)], axis=0)
    return cos, sin


def _qkv_kernel(qa_ref, qb_ref, kv_ref, cos_ref, sin_ref, qn_ref, kn_ref, q_out, k_out, vt_out):
    cos = cos_ref[...]
    sin = sin_ref[...]
    lane = lax.broadcasted_iota(jnp.int32, cos.shape, 1)
    first = (lane % (2 * ROPE_FREQS)) < ROPE_FREQS

    def norm_rope(v, g, scale):
        y = v * lax.rsqrt(jnp.mean(v * v, axis=-1, keepdims=True) + EPS) * g
        partner = jnp.where(first, pltpu.roll(y, HEAD_DIM - ROPE_FREQS, 1), pltpu.roll(y, ROPE_FREQS, 1))
        out = y * cos + partner * sin
        if scale is not None:
            out = out * scale
        return out.astype(BF16)

    half = N_Q_HEADS // 2
    for h in range(half):
        cols = slice(h * HEAD_DIM, (h + 1) * HEAD_DIM)
        q_out[:, cols] = norm_rope(qa_ref[:, cols], qn_ref[...], Q_SCALE)
        cols_b = slice((half + h) * HEAD_DIM, (half + h + 1) * HEAD_DIM)
        q_out[:, cols_b] = norm_rope(qb_ref[:, cols], qn_ref[...], Q_SCALE)
    for h in range(N_KV_HEADS):
        cols = slice(h * HEAD_DIM, (h + 1) * HEAD_DIM)
        k_out[:, cols] = norm_rope(kv_ref[:, cols], kn_ref[...], None)
        vt_out[cols, :] = kv_ref[:, KV_WIDTH + h * HEAD_DIM:KV_WIDTH + (h + 1) * HEAD_DIM].T.astype(BF16)


def _qkv(u, cos, sin, qn, kn, n_lat, nlc):
    t = u.shape[0]
    ch = SEQ_CHUNK
    cw = 512
    base = (A_IN + B_IN) // cw
    tab = lambda i: (jnp.where(i < n_lat // ch, i % nlc, nlc), 0)
    return pl.pallas_call(
        _qkv_kernel,
        grid=(t // ch,),
        in_specs=[
            pl.BlockSpec((ch, cw), lambda i: (i, base)),
            pl.BlockSpec((ch, cw), lambda i: (i, base + 1)),
            pl.BlockSpec((ch, cw), lambda i: (i, base + 2)),
            pl.BlockSpec((ch, HEAD_DIM), tab),
            pl.BlockSpec((ch, HEAD_DIM), tab),
            pl.BlockSpec((1, HEAD_DIM), lambda i: (0, 0)),
            pl.BlockSpec((1, HEAD_DIM), lambda i: (0, 0)),
        ],
        out_specs=[
            pl.BlockSpec((ch, ATT_WIDTH), lambda i: (i, 0)),
            pl.BlockSpec((ch, KV_WIDTH), lambda i: (i, 0)),
            pl.BlockSpec((KV_WIDTH, ch), lambda i: (0, i)),
        ],
        out_shape=[
            jax.ShapeDtypeStruct((t, ATT_WIDTH), BF16),
            jax.ShapeDtypeStruct((t, KV_WIDTH), BF16),
            jax.ShapeDtypeStruct((KV_WIDTH, t), BF16),
        ],
        compiler_params=_cparams("parallel"),
        name="qkv_prep",
    )(u, u, u, cos, sin, qn.reshape(1, HEAD_DIM), kn.reshape(1, HEAD_DIM))


def _row_groups(v, op):
    return op(v.reshape(v.shape[0] // SUBLANES, SUBLANES, v.shape[1]), axis=0)


def _attn_body(with_lat, q_ref, kc_ref, vtc_ref, kl_ref, vtl_ref, o_ref, acc_ref, s_ref):
    tq = q_ref.shape[0]
    tk = s_ref.shape[1]
    n_tiles = kl_ref.shape[0] // tk if with_lat else 0

    for g in range(N_KV_HEADS):
        gcols = slice(g * HEAD_DIM, (g + 1) * HEAD_DIM)
        qs = jnp.concatenate(
            [q_ref[:, (g * Q_PER_KV + h) * HEAD_DIM:(g * Q_PER_KV + h + 1) * HEAD_DIM] for h in range(Q_PER_KV)],
            axis=0)

        def softmax_pv(s, vtt, carry):
            m_new = jnp.max(_row_groups(s, jnp.max), axis=0, keepdims=True)
            if carry is not None:
                m_new = jnp.maximum(carry[0], m_new)
            p = jnp.exp2(s - m_new)
            pv = _dot(vtt, p.astype(BF16))
            if carry is None:
                acc_ref[...] = pv
                return m_new, _row_groups(p, jnp.sum)
            alpha = jnp.exp2(carry[0] - m_new)
            acc_ref[...] = alpha * acc_ref[...] + pv
            return m_new, alpha * carry[1] + _row_groups(p, jnp.sum)

        def scores(t):
            r0 = t * tk if isinstance(t, int) else pl.multiple_of(t * tk, tk)
            return _dot_nt(kl_ref[pl.ds(r0, tk), gcols], qs)

        def consume(slot, t, carry):
            r0 = t * tk if isinstance(t, int) else pl.multiple_of(t * tk, tk)
            return softmax_pv(s_ref[slot], vtl_ref[gcols, pl.ds(r0, tk)], carry)

        if n_tiles:
            s_ref[0] = scores(0)
        carry = softmax_pv(_dot_nt(kc_ref[:, gcols], qs), vtc_ref[gcols, :], None)
        n_pairs = max(n_tiles // 2 - 1, 0)
        if n_pairs:
            def pair(j, carry):
                t0 = 2 * j
                s_ref[1] = scores(t0 + 1)
                carry = consume(0, t0, carry)
                s_ref[0] = scores(t0 + 2)
                return consume(1, t0 + 1, carry)
            carry = lax.fori_loop(0, n_pairs, pair, carry)
        for t in range(2 * n_pairs, n_tiles):
            if t + 1 < n_tiles:
                s_ref[(t + 1) % 2] = scores(t + 1)
            carry = consume(t % 2, t, carry)
        ot = acc_ref[...] * (1.0 / jnp.sum(carry[1], axis=0, keepdims=True))
        for h in range(Q_PER_KV):
            cols = slice((g * Q_PER_KV + h) * HEAD_DIM, (g * Q_PER_KV + h + 1) * HEAD_DIM)
            o_ref[:, cols] = ot[:, h * tq:(h + 1) * tq].T.astype(BF16)


def _attn_kernel(lat_steps, ctx_steps, *refs):
    if ctx_steps == 0:
        _attn_body(True, *refs)
        return
    i = pl.program_id(1)

    @pl.when(i < lat_steps)
    def _():
        _attn_body(True, *refs)

    @pl.when(i >= lat_steps)
    def _():
        _attn_body(False, *refs)


def _attn(q, k, vt, nb, n, nc, with_ctx_queries):
    n_lat = nb * n
    tq = ATT_QUERY_TILE
    lat_steps = n // tq
    ctx_steps = nc // tq if with_ctx_queries else 0
    rows = n_lat + (nb * nc if with_ctx_queries else 0)

    def q_map(b, i):
        return (jnp.where(i < lat_steps, b * lat_steps + i, n_lat // tq + b * ctx_steps + (i - lat_steps)), 0)

    m = Q_PER_KV * tq
    tk = min(ATT_KEY_TILE, n)
    assert n % tk == 0 and n % tq == 0 and nc % tq == 0
    return pl.pallas_call(
        functools.partial(_attn_kernel, lat_steps, ctx_steps),
        grid=(nb, lat_steps + ctx_steps),
        in_specs=[pl.BlockSpec((tq, ATT_WIDTH), q_map),
                  pl.BlockSpec((nc, KV_WIDTH), lambda b, i: (n_lat // nc + b, 0)),
                  pl.BlockSpec((KV_WIDTH, nc), lambda b, i: (0, n_lat // nc + b)),
                  pl.BlockSpec((n, KV_WIDTH), lambda b, i: (b, 0)),
                  pl.BlockSpec((KV_WIDTH, n), lambda b, i: (0, b))],
        out_specs=pl.BlockSpec((tq, ATT_WIDTH), q_map),
        out_shape=jax.ShapeDtypeStruct((rows, ATT_WIDTH), BF16),
        scratch_shapes=[pltpu.VMEM((HEAD_DIM, m), F32), pltpu.VMEM((2, tk, m), F32)],
        compiler_params=_cparams("parallel", "arbitrary"),
        name="attention",
    )(q, k, vt, k, vt)


def _outproj_kernel(routed, lat_tiles, *refs):
    if routed:
        ra_ref, hg_ref, at_ref, xl_ref, xc_ref, mod_ref, w_ref, g_ref, rt_ref, xo_ref, h_ref, lg_ref = refs
    else:
        ra_ref, hg_ref, at_ref, xl_ref, xc_ref, mod_ref, w_ref, g_ref, xo_ref, h_ref = refs
    w1 = ra_ref.shape[1]
    w2 = w1 + hg_ref.shape[1]
    mix = (_dot(ra_ref[...], w_ref[0:w1, :]) + _dot(hg_ref[...], w_ref[w1:w2, :])
           + _dot(at_ref[...], w_ref[w2:, :]))
    m = mod_ref[0]
    x_in = jnp.where(pl.program_id(0) < lat_tiles, xl_ref[...], xc_ref[...])
    x = x_in + m[2:3] * mix
    xo_ref[...] = x
    y = x * lax.rsqrt(jnp.mean(x * x, axis=-1, keepdims=True) + EPS) * g_ref[...]
    h = y * (1.0 + m[4:5]) + m[3:4]
    if routed:
        h_ref[...] = h
        h_hi = h.astype(BF16)
        h_lo = (h - h_hi.astype(F32)).astype(BF16)
        lg = _dot(h_hi, rt_ref[...]) + _dot(h_lo, rt_ref[...])
        lg_ref[...] = lg[:, 0:LANES] + lg[:, LANES:2 * LANES]
    else:
        h_ref[...] = h.astype(BF16)


def _outproj(ra, hg, at, x_lat, x_ctx, ctx_base, mod, w, g, router, n_tok, n_lat, seq, nb):
    d = x_lat.shape[1]
    tm = 256
    routed = router is not None
    row = functools.partial(_mod_row, tile=tm, n_lat=n_lat, seq=seq, nb=nb)
    tok = lambda width: pl.BlockSpec((tm, width), lambda i: (i, 0))
    lat_spec, ctx_spec = _stream_specs(tm, d, n_lat // tm, ctx_base // tm)
    in_specs = [tok(ra.shape[1]), tok(hg.shape[1]), tok(at.shape[1]), lat_spec, ctx_spec,
                pl.BlockSpec((1, N_MOD, d), lambda i: (row(i), 0, 0)),
                pl.BlockSpec(w.shape, lambda i: (0, 0)),
                pl.BlockSpec((1, d), lambda i: (0, 0))]
    args = [ra, hg, at, x_lat, x_ctx, mod, w, g.reshape(1, d)]
    out_specs = [tok(d), tok(d)]
    out_shape = [jax.ShapeDtypeStruct((n_tok, d), F32), jax.ShapeDtypeStruct((n_tok, d), F32 if routed else BF16)]
    if routed:
        in_specs.append(pl.BlockSpec((d, 2 * LANES), lambda i: (0, 0)))
        args.append(router)
        out_specs.append(tok(LANES))
        out_shape.append(jax.ShapeDtypeStruct((n_tok, LANES), F32))
    return pl.pallas_call(
        functools.partial(_outproj_kernel, routed, n_lat // tm),
        grid=(n_tok // tm,),
        in_specs=in_specs,
        out_specs=out_specs,
        out_shape=out_shape,
        compiler_params=_cparams("parallel"),
        name="outproj",
    )(*args)


def _ffn_kernel(h_ref, x_ref, mod_ref, w1_ref, w3_ref, w2_ref, o_ref, acc_ref):
    k = pl.program_id(1)

    @pl.when(k == 0)
    def _():
        acc_ref[...] = jnp.zeros_like(acc_ref)

    h = h_ref[...]
    a = _dot(h, w1_ref[...])
    z = _silu(a) * _dot(h, w3_ref[...])
    acc_ref[...] += _dot(z.astype(BF16), w2_ref[...])

    @pl.when(k == pl.num_programs(1) - 1)
    def _():
        o_ref[...] = x_ref[...] + mod_ref[0][5:6] * acc_ref[...]


def _ffn(h, xs, mod, w1, w3, w2, n_lat, seq, nb):
    t, d = h.shape
    f = w1.shape[1]
    tm = TOKEN_TILE
    tf = 512
    row = functools.partial(_mod_row, tile=tm, n_lat=n_lat, seq=seq, nb=nb)
    return pl.pallas_call(
        _ffn_kernel,
        grid=(t // tm, f // tf),
        in_specs=[
            pl.BlockSpec((tm, d), lambda i, k: (i, 0)),
            pl.BlockSpec((tm, d), lambda i, k: (i, 0)),
            pl.BlockSpec((1, N_MOD, d), lambda i, k: (row(i), 0, 0)),
            pl.BlockSpec((d, tf), lambda i, k: (0, k)),
            pl.BlockSpec((d, tf), lambda i, k: (0, k)),
            pl.BlockSpec((tf, d), lambda i, k: (k, 0)),
        ],
        out_specs=pl.BlockSpec((tm, d), lambda i, k: (i, 0)),
        out_shape=jax.ShapeDtypeStruct((t, d), F32),
        scratch_shapes=[pltpu.VMEM((tm, d), F32)],
        compiler_params=_cparams("parallel", "arbitrary"),
        name="ffn_dense",
    )(h, xs, mod, w1, w3, w2)


def _router_kernel(lg_ref, meta_ref, wts_ref, cnt_ref, run_ref):
    @pl.when(pl.program_id(0) == 0)
    def _():
        run_ref[...] = jnp.zeros_like(run_ref)

    lg = lg_ref[...]
    tm = lg.shape[0]
    lane = lax.broadcasted_iota(jnp.int32, lg.shape, 1)
    lane_f = lane.astype(F32)
    v = jnp.where(lane < N_EXPERTS, lg, -jnp.inf)
    m1 = jnp.max(v, axis=-1, keepdims=True)
    i1 = jnp.min(jnp.where(v == m1, lane_f, float(LANES)), axis=-1, keepdims=True)
    v2 = jnp.where(lane_f == i1, -jnp.inf, v)
    m2 = jnp.max(v2, axis=-1, keepdims=True)
    i2 = jnp.min(jnp.where(v2 == m2, lane_f, float(LANES)), axis=-1, keepdims=True)
    e = jnp.exp(m2 - m1)
    wt1 = 1.0 / (1.0 + e)
    wt2 = e / (1.0 + e)
    hit1 = lane_f == i1
    hit2 = lane_f == i2
    assign = jnp.where(hit1 | hit2, 1.0, 0.0)
    r = lax.broadcasted_iota(jnp.int32, (tm, tm), 0)
    c = lax.broadcasted_iota(jnp.int32, (tm, tm), 1)
    tri = jnp.where(r > c, 1.0, 0.0).astype(BF16)
    rank = _dot(tri, assign.astype(BF16)) + run_ref[0:1, :]
    r1 = jnp.sum(jnp.where(hit1, rank, 0.0), axis=-1, keepdims=True)
    r2 = jnp.sum(jnp.where(hit2, rank, 0.0), axis=-1, keepdims=True)
    run_ref[...] = run_ref[...] + jnp.sum(assign, axis=0, keepdims=True)
    meta = jnp.where(lane == 0, i1, jnp.where(lane == 1, i2, jnp.where(lane == 2, r1, jnp.where(lane == 3, r2, 0.0))))
    meta_ref[...] = meta.astype(jnp.int32)
    wts_ref[...] = jnp.where(lane == 0, wt1, jnp.where(lane == 1, wt2, 0.0))
    cnt_ref[...] = run_ref[...]


def _router(logits):
    t = logits.shape[0]
    tm = TOKEN_TILE
    tok = pl.BlockSpec((tm, LANES), lambda i: (i, 0))
    return pl.pallas_call(
        _router_kernel,
        grid=(t // tm,),
        in_specs=[tok],
        out_specs=[tok, tok, pl.BlockSpec((SUBLANES, LANES), lambda i: (0, 0))],
        out_shape=[jax.ShapeDtypeStruct((t, LANES), jnp.int32), jax.ShapeDtypeStruct((t, LANES), F32),
                   jax.ShapeDtypeStruct((SUBLANES, LANES), F32)],
        scratch_shapes=[pltpu.VMEM((SUBLANES, LANES), F32)],
        compiler_params=_cparams("arbitrary"),
        name="router",
    )(logits)


def _row_copy(src_ref, src_row, dst_ref, dst_row, sem):
    return pltpu.make_async_copy(src_ref.at[pl.ds(src_row, 1)], dst_ref.at[pl.ds(dst_row, 1)], sem)


def _dispatch_kernel(pos_ref, h_ref, xs_in_ref, xs_ref, pos_smem, hbuf, sem_p, sem_h, sem_d):
    del xs_in_ref
    i = pl.program_id(0)
    n = pl.num_programs(0)
    tm = hbuf.shape[1]
    cur = i % 3

    def tile_copy(tile, buf):
        return pltpu.make_async_copy(h_ref.at[pl.ds(pl.multiple_of(tile * tm, tm), tm)], hbuf.at[buf], sem_h.at[buf])

    @pl.when(i == 0)
    def _():
        tile_copy(0, 0).start()

    @pl.when(i < n - 1)
    def _():
        tile_copy(i + 1, (i + 1) % 3).start()

    cp = pltpu.make_async_copy(pos_ref.at[0, 0], pos_smem, sem_p)
    cp.start()
    cp.wait()
    tile_copy(i, cur).wait()

    def issue(t, carry):
        _row_copy(hbuf.at[cur], t, xs_ref, pos_smem[t], sem_d.at[cur]).start()
        _row_copy(hbuf.at[cur], t, xs_ref, pos_smem[tm + t], sem_d.at[cur]).start()
        return carry

    lax.fori_loop(0, tm, issue, 0, unroll=8)

    def drain(buf):
        def body(t, carry):
            _row_copy(hbuf.at[buf], 0, xs_ref, 0, sem_d.at[buf]).wait()
            _row_copy(hbuf.at[buf], 0, xs_ref, 0, sem_d.at[buf]).wait()
            return carry
        lax.fori_loop(0, tm, body, 0, unroll=8)

    @pl.when(i > 0)
    def _():
        drain((i + 2) % 3)

    @pl.when(i == n - 1)
    def _():
        drain(cur)


def _dispatch(pos, h, xs0):
    t, d = h.shape
    tm = 256
    nt = t // tm
    return pl.pallas_call(
        _dispatch_kernel,
        grid=(nt,),
        in_specs=[
            pl.BlockSpec((1, 1, 2 * tm), lambda i: (i, 0, 0)),
            pl.BlockSpec(memory_space=pl.ANY),
            pl.BlockSpec(memory_space=pl.ANY),
        ],
        out_specs=pl.BlockSpec(memory_space=pl.ANY),
        out_shape=jax.ShapeDtypeStruct(xs0.shape, xs0.dtype),
        scratch_shapes=[pltpu.SMEM((2 * tm,), jnp.int32), pltpu.VMEM((3, tm, d), F32),
                        pltpu.SemaphoreType.DMA(()), pltpu.SemaphoreType.DMA((3,)), pltpu.SemaphoreType.DMA((3,))],
        input_output_aliases={2: 0},
        compiler_params=_cparams("arbitrary"),
        name="moe_dispatch",
    )(pos, h, xs0)


def _expert_kernel(te_ref, tv_ref, x_ref, w1_ref, w3_ref, w2_ref, y_ref, xb_ref):
    del te_ref
    i = pl.program_id(0)
    k = pl.program_id(1)

    @pl.when(k == 0)
    def _():
        y_ref[...] = jnp.zeros_like(y_ref)

    @pl.when(tv_ref[i] > 0)
    def _():
        @pl.when(k == 0)
        def _():
            xb_ref[...] = x_ref[...].astype(BF16)

        xb = xb_ref[...]
        a = _dot(xb, w1_ref[...])
        z = _silu(a) * _dot(xb, w3_ref[...])
        y_ref[...] += _dot(z.astype(BF16), w2_ref[...])


def _experts(tile_expert, tile_valid, tile_row, xs, w1, w3, w2):
    rows, d = xs.shape
    f = w1.shape[2]
    tm = EXPERT_TILE
    tf = 512
    grid_spec = pltpu.PrefetchScalarGridSpec(
        num_scalar_prefetch=3,
        grid=(rows // tm, f // tf),
        in_specs=[
            pl.BlockSpec((tm, d), lambda i, k, te, tv, tr: (tr[i], 0)),
            pl.BlockSpec((None, d, tf), lambda i, k, te, tv, tr: (te[i], 0, jnp.where(tv[i] > 0, k, f // tf - 1))),
            pl.BlockSpec((None, d, tf), lambda i, k, te, tv, tr: (te[i], 0, jnp.where(tv[i] > 0, k, f // tf - 1))),
            pl.BlockSpec((None, tf, d), lambda i, k, te, tv, tr: (te[i], jnp.where(tv[i] > 0, k, f // tf - 1), 0)),
        ],
        out_specs=pl.BlockSpec((tm, d), lambda i, k, te, tv, tr: (i, 0)),
        scratch_shapes=[pltpu.VMEM((tm, d), BF16)],
    )

    def body(te_ref, tv_ref, tr_ref, *rest):
        del tr_ref
        _expert_kernel(te_ref, tv_ref, *rest)

    return pl.pallas_call(
        body,
        grid_spec=grid_spec,
        out_shape=jax.ShapeDtypeStruct((rows, d), F32),
        compiler_params=_cparams("arbitrary", "arbitrary"),
        name="moe_experts",
    )(tile_expert, tile_valid, tile_row, xs, w1, w3, w2)


def _combine_kernel(pos0_ref, posn_ref, wts_ref, x_ref, mod_ref, ys_ref, o_ref, pos_smem, y_ref, sem_p, sem_g):
    i = pl.program_id(0)
    n = pl.num_programs(0)
    tm = x_ref.shape[0]
    cur = i % 2

    def gather(pos_ref, slot):
        cp = pltpu.make_async_copy(pos_ref.at[0, 0], pos_smem, sem_p)
        cp.start()
        cp.wait()

        def issue(t, carry):
            _row_copy(ys_ref, pos_smem[t], y_ref.at[slot, 0], t, sem_g.at[slot]).start()
            _row_copy(ys_ref, pos_smem[tm + t], y_ref.at[slot, 1], t, sem_g.at[slot]).start()
            return carry

        lax.fori_loop(0, tm, issue, 0, unroll=8)

    @pl.when(i == 0)
    def _():
        gather(pos0_ref, 0)

    @pl.when(i < n - 1)
    def _():
        gather(posn_ref, 1 - cur)

    def drain(t, carry):
        _row_copy(ys_ref, 0, y_ref.at[cur, 0], 0, sem_g.at[cur]).wait()
        _row_copy(ys_ref, 0, y_ref.at[cur, 1], 0, sem_g.at[cur]).wait()
        return carry

    lax.fori_loop(0, tm, drain, 0, unroll=8)
    wts = wts_ref[...]
    f = wts[:, 0:1] * y_ref[cur, 0] + wts[:, 1:2] * y_ref[cur, 1]
    o_ref[...] = x_ref[...] + mod_ref[0][5:6] * f


def _combine(pos, wts, xs, mod, ys, n_lat, seq, nb):
    t, d = xs.shape
    tm = 256
    row = functools.partial(_mod_row, tile=tm, n_lat=n_lat, seq=seq, nb=nb)
    return pl.pallas_call(
        _combine_kernel,
        grid=(t // tm,),
        in_specs=[
            pl.BlockSpec((1, 1, 2 * tm), lambda i: (0, 0, 0)),
            pl.BlockSpec((1, 1, 2 * tm), lambda i: (jnp.minimum(i + 1, t // tm - 1), 0, 0)),
            pl.BlockSpec((tm, LANES), lambda i: (i, 0)),
            pl.BlockSpec((tm, d), lambda i: (i, 0)),
            pl.BlockSpec((1, N_MOD, d), lambda i: (row(i), 0, 0)),
            pl.BlockSpec(memory_space=pl.ANY),
        ],
        out_specs=pl.BlockSpec((tm, d), lambda i: (i, 0)),
        out_shape=jax.ShapeDtypeStruct((t, d), F32),
        scratch_shapes=[pltpu.SMEM((2 * tm,), jnp.int32), pltpu.VMEM((2, 2, tm, d), F32),
                        pltpu.SemaphoreType.DMA(()), pltpu.SemaphoreType.DMA((2,))],
        compiler_params=_cparams("arbitrary"),
        name="moe_combine",
    )(pos, pos, wts, xs, mod, ys)


def _moe(h, logits, xs, mod, w1, w3, w2, n_lat, seq, nb):
    t, d = h.shape
    meta, wts, counts = _router(logits)
    te_rows = EXPERT_TILE
    cnt = counts[0, :N_EXPERTS].astype(jnp.int32)
    padded = ((cnt + te_rows - 1) // te_rows) * te_rows
    ends = jnp.cumsum(padded)
    starts = ends - padded
    pos1 = jnp.take(starts, meta[:, 0]) + meta[:, 2]
    pos2 = jnp.take(starts, meta[:, 1]) + meta[:, 3]
    tmd = 256
    pos = jnp.concatenate([pos1.reshape(t // tmd, 1, tmd), pos2.reshape(t // tmd, 1, tmd)], axis=-1)
    n_tiles = (2 * t) // te_rows + N_EXPERTS
    tile_start = jnp.arange(n_tiles, dtype=jnp.int32) * te_rows
    n_valid = ends[-1] // te_rows
    tile_valid = (tile_start < ends[-1]).astype(jnp.int32)
    tile_row = jnp.minimum(jnp.arange(n_tiles, dtype=jnp.int32), n_valid - 1)
    tile_expert = jnp.sum((tile_row[:, None] * te_rows >= ends[None, :]).astype(jnp.int32), axis=1)
    xs0 = jnp.zeros((n_tiles * te_rows, d), F32)
    x_sorted = _dispatch(pos, h, xs0)
    y_sorted = _experts(tile_expert, tile_valid, tile_row, x_sorted, w1, w3, w2)
    return _combine(pos, wts, xs, mod, y_sorted, n_lat, seq, nb)


def kernel(x, c, ctx, c_ctx, w_mod, b_mod, norm1, norm2, w_in, w_out, conv_w, conv_b, lru_wa, lru_ba, lru_wi, lru_bi, lru_lambda, hgrn_lb_logits, hgrn_gnorm, q_norm, k_norm, ffn_w1, ffn_w3, ffn_w2, router, moe_w1, moe_w3, moe_w2):
    nb, n, d = x.shape
    nc = ctx.shape[1]
    depth = w_mod.shape[0]
    assert n % SEQ_CHUNK == 0 and nc % SEQ_CHUNK == 0 and n % TOKEN_TILE == 0
    assert (nb * nc) % TOKEN_TILE == 0 and n % GRID_W == 0
    n_lat = nb * n
    nlc = n // SEQ_CHUNK
    ncc = nc // SEQ_CHUNK

    n_all = n_lat + nb * nc
    x_lat, x_ctx, ctx_base = x.reshape(n_lat, d), ctx.reshape(nb * nc, d), 0
    mod_rows = 2 * SUBLANES * ((nb + 1 + 2 * SUBLANES - 1) // (2 * SUBLANES))
    cc = jnp.zeros((mod_rows, d), F32).at[:nb].set(c).at[nb].set(c_ctx)
    mod_all = _mod_table(cc, w_mod, b_mod).reshape(depth, mod_rows, N_MOD, d)
    cos, sin = _rope_tables(n)
    router_f = jnp.pad(router, ((0, 0), (0, 0), (0, LANES - router.shape[-1])))
    router_hi = router_f.astype(BF16)
    router_lo = (router_f - router_hi.astype(F32)).astype(BF16)
    router_p = jnp.concatenate([router_hi, router_lo], axis=-1)

    for l in range(depth):
        need_ctx = l < depth - 1
        mod = mod_all[l]
        u = _inproj(x_lat, x_ctx, ctx_base, n_all, mod, norm1[l], _to_bf16(w_in, l), n_lat, n, nb)

        lru_args = lambda dd: (conv_w[l], conv_b[l], lru_wa[l, dd], lru_ba[l, dd], lru_wi[l, dd], lru_bi[l, dd],
                               lru_lambda[l, dd])
        hf = _lru_pass(False, u, *lru_args(0), None, nb, ncc, nlc)
        ra = _lru_pass(True, u, *lru_args(1), hf, nb, ncc, nlc)

        of = _hgrn_pass(False, l, u, hgrn_lb_logits[0], hgrn_gnorm[l], None, nb, ncc, nlc)
        hg = _hgrn_pass(True, l, u, hgrn_lb_logits[1], hgrn_gnorm[l], of, nb, ncc, nlc)

        q, k, vt = _qkv(u, cos, sin, q_norm[l], k_norm[l], n_lat, nlc)
        at = _attn(q, k, vt, nb, n, nc, need_ctx)

        n_tok = n_all if need_ctx else n_lat
        j = l // 2
        routed = l % 2 == 1
        outs = _outproj(ra, hg, at, x_lat, x_ctx, ctx_base, mod, _to_bf16(w_out, l), norm2[l],
                        router_p[j] if routed else None, n_tok, n_lat, n, nb)
        if routed:
            xn, h2, logits = outs
            xs = _moe(h2, logits, xn, mod, _to_bf16(moe_w1, j), _to_bf16(moe_w3, j), _to_bf16(moe_w2, j),
                      n_lat, n, nb)
        else:
            xn, h2 = outs
            xs = _ffn(h2, xn, mod, _to_bf16(ffn_w1, j), _to_bf16(ffn_w3, j), _to_bf16(ffn_w2, j), n_lat, n, nb)
        x_lat, x_ctx, ctx_base = xs, xs, n_lat
    return xs[:n_lat].reshape(nb, n, d)
```

```python
import functools

import jax
import jax.numpy as jnp
from jax import lax
from jax.experimental import pallas as pl
from jax.experimental.pallas import tpu as pltpu

F32 = jnp.float32
BF16 = jnp.bfloat16

EPS = 1e-6
N_MOD = 6
GRID_W = 64
LRU_WIDTH = 512
LRU_BLOCKS = 4
LRU_BLOCK = LRU_WIDTH // LRU_BLOCKS
LRU_C = 8.0
HG_HEADS = 4
HG_DK = 128
HG_WIDTH = HG_HEADS * HG_DK
HG_CHUNK = 64
HG_SUB = 8
HEAD_DIM = 128
N_Q_HEADS = 8
N_KV_HEADS = 2
Q_PER_KV = N_Q_HEADS // N_KV_HEADS
ATT_WIDTH = N_Q_HEADS * HEAD_DIM
KV_WIDTH = N_KV_HEADS * HEAD_DIM
ROPE_THETA = 10000.0
ROPE_FREQS = HEAD_DIM // 4
ATTN_SCALE = HEAD_DIM ** -0.5
LOG2E = 1.4426950408889634
Q_SCALE = ATTN_SCALE * LOG2E
A_IN = 2 * LRU_WIDTH
B_IN = 5 * HG_WIDTH
N_EXPERTS = 8

LANES = 128
SUBLANES = 8
SEQ_CHUNK = 256
TOKEN_TILE = 512
INPROJ_FLAT_TILE = 1024
EXPERT_TILE = 512
ATT_KEY_TILE = 1024
ATT_QUERY_TILE = 256
CAST_BLOCK_ELEMS = 2 * 1024 * 1024
VMEM_LIMIT = 56 * 1024 * 1024


def _cparams(*sem):
    return pltpu.CompilerParams(dimension_semantics=sem, vmem_limit_bytes=VMEM_LIMIT)


def _sigmoid(v):
    return 1.0 / (1.0 + jnp.exp(-v))


def _sigmoid_tanh(v):
    return 0.5 * jnp.tanh(0.5 * v) + 0.5


def _silu(v):
    return v * _sigmoid_tanh(v)


def _dot(a, b):
    return jnp.dot(a, b, preferred_element_type=F32)


def _dot_nt(a, b):
    return lax.dot_general(a, b, (((1,), (1,)), ((), ())), preferred_element_type=F32)


def _dot_tn(a, b):
    return lax.dot_general(a, b, (((0,), (0,)), ((), ())), preferred_element_type=F32)


def _mod_row(i, tile, n_lat, seq, nb):
    return jnp.where(i < n_lat // tile, (i * tile) // seq, nb)


def _cast_kernel(w_ref, o_ref):
    o_ref[...] = w_ref[...].astype(BF16)


def _to_bf16(w, index):
    inner = w.shape[1:]
    cols = inner[-1]
    rows = 1
    for s in inner[:-1]:
        rows *= s
    tr = 2 * SUBLANES
    while tr * 2 * cols <= CAST_BLOCK_ELEMS and rows % (tr * 2) == 0:
        tr *= 2
    assert rows % tr == 0
    steps = rows // tr
    out = pl.pallas_call(
        _cast_kernel,
        grid=(steps,),
        in_specs=[pl.BlockSpec((tr, cols), lambda i: (index * steps + i, 0))],
        out_specs=pl.BlockSpec((tr, cols), lambda i: (i, 0)),
        out_shape=jax.ShapeDtypeStruct((rows, cols), BF16),
        compiler_params=_cparams("parallel"),
        name="to_bf16",
    )(w.reshape(w.shape[0] * rows, cols))
    return out.reshape(inner)


def _mod_kernel(c_ref, w_ref, b_ref, o_ref):
    s = _silu(c_ref[...]).astype(BF16)
    o_ref[0] = _dot(s, w_ref[0].astype(BF16)) + b_ref[0]


def _mod_table(cc, w_mod, b_mod):
    nl, d, md = w_mod.shape
    rows = cc.shape[0]
    tn = 1024 if md % 1024 == 0 else md
    return pl.pallas_call(
        _mod_kernel,
        grid=(nl, md // tn),
        in_specs=[
            pl.BlockSpec((rows, d), lambda l, j: (0, 0)),
            pl.BlockSpec((1, d, tn), lambda l, j: (l, 0, j)),
            pl.BlockSpec((1, 1, tn), lambda l, j: (l, 0, j)),
        ],
        out_specs=pl.BlockSpec((1, rows, tn), lambda l, j: (l, 0, j)),
        out_shape=jax.ShapeDtypeStruct((nl, rows, md), F32),
        compiler_params=_cparams("parallel", "parallel"),
        name="mod_table",
    )(cc, w_mod, b_mod.reshape(nl, 1, md))


def _stream_specs(tm, d, lat_tiles, ctx_base):
    lat = pl.BlockSpec((tm, d), lambda i, *_: (jnp.minimum(i, lat_tiles - 1), 0))
    ctx = pl.BlockSpec((tm, d), lambda i, *_: (ctx_base + jnp.maximum(i - lat_tiles, 0), 0))
    return lat, ctx


def _inproj_kernel(lat_tiles, *refs):
    if lat_tiles is None:
        x_ref, mod_ref, g_ref, w_ref, o_ref, h_ref = refs
    else:
        xl_ref, xc_ref, mod_ref, g_ref, w_ref, o_ref, h_ref = refs
    i = pl.program_id(0)
    first = pl.program_id(1) == 0

    def norm_mod(x_ref):
        x = x_ref[...]
        y = x * lax.rsqrt(jnp.mean(x * x, axis=-1, keepdims=True) + EPS) * g_ref[...]
        m = mod_ref[0]
        h_ref[...] = (y * (1.0 + m[1:2]) + m[0:1]).astype(BF16)

    if lat_tiles is None:
        @pl.when(first)
        def _():
            norm_mod(x_ref)
    else:
        @pl.when(first & (i < lat_tiles))
        def _():
            norm_mod(xl_ref)

        @pl.when(first & (i >= lat_tiles))
        def _():
            norm_mod(xc_ref)

    o_ref[...] = _dot(h_ref[...], w_ref[...])


def _inproj(x_lat, x_ctx, ctx_base, n_tok, mod, g, w, n_lat, seq, nb):
    d = x_lat.shape[1]
    width = w.shape[1]
    tn = 1024
    flat = x_lat is x_ctx
    if flat:
        tm = INPROJ_FLAT_TILE if (seq % INPROJ_FLAT_TILE == 0 and (n_tok - n_lat) % INPROJ_FLAT_TILE == 0) \
            else TOKEN_TILE
        x_specs = [pl.BlockSpec((tm, d), lambda i, j: (i, 0))]
        x_args = [x_lat]
    else:
        tm = TOKEN_TILE
        x_specs = list(_stream_specs(tm, d, n_lat // tm, ctx_base // tm))
        x_args = [x_lat, x_ctx]
    row = functools.partial(_mod_row, tile=tm, n_lat=n_lat, seq=seq, nb=nb)
    return pl.pallas_call(
        functools.partial(_inproj_kernel, None if flat else n_lat // tm),
        grid=(n_tok // tm, width // tn),
        in_specs=x_specs + [
            pl.BlockSpec((1, N_MOD, d), lambda i, j: (row(i), 0, 0)),
            pl.BlockSpec((1, d), lambda i, j: (0, 0)),
            pl.BlockSpec((d, tn), lambda i, j: (0, j)),
        ],
        out_specs=pl.BlockSpec((tm, tn), lambda i, j: (i, j)),
        out_shape=jax.ShapeDtypeStruct((n_tok, width), F32),
        scratch_shapes=[pltpu.VMEM((tm, d), BF16)],
        compiler_params=_cparams("parallel", "arbitrary"),
        name="inproj",
    )(*x_args, mod, g.reshape(1, d), w)


def _seq_pos(j, rev, ncc, nlc):
    is_ctx = j < ncc
    jl = j - ncc
    cc = (ncc - 1 - j) if rev else j
    cl = (nlc - 1 - jl) if rev else jl
    return is_ctx, cc, cl


def _seq_blk(b, j, rev, ncc, nlc, nb):
    is_ctx, cc, cl = _seq_pos(j, rev, ncc, nlc)
    return jnp.where(is_ctx, nb * nlc + b * ncc + cc, b * nlc + cl)


def _scan_rows(a_ref, b_ref, h_ref, hc_ref, rev):
    n_groups = a_ref.shape[0] // SUBLANES
    width = a_ref.shape[1]
    rows = lax.broadcasted_iota(jnp.int32, (SUBLANES, width), 0)

    def body(g, hc):
        gi = (n_groups - 1 - g) if rev else g
        r0 = pl.multiple_of(gi * SUBLANES, SUBLANES)
        a = a_ref[pl.ds(r0, SUBLANES), :]
        b = b_ref[pl.ds(r0, SUBLANES), :]
        for s in (1, 2, 4):
            if rev:
                a_s = pltpu.roll(a, SUBLANES - s, 0)
                b_s = pltpu.roll(b, SUBLANES - s, 0)
                keep = rows < SUBLANES - s
            else:
                a_s = pltpu.roll(a, s, 0)
                b_s = pltpu.roll(b, s, 0)
                keep = rows >= s
            b = a * jnp.where(keep, b_s, 0.0) + b
            a = a * jnp.where(keep, a_s, 1.0)
        h = b + a * hc
        h_ref[pl.ds(r0, SUBLANES), :] = h
        return h[0:1] if rev else h[SUBLANES - 1:SUBLANES]

    hc_ref[...] = lax.fori_loop(0, n_groups, body, hc_ref[...], unroll=4)


def _lru_kernel(rev, ncc, nlc, *refs):
    if rev:
        (u_ref, prev_ref, next_ref, cw_ref, cb_ref, wa_ref, ba_ref, wi_ref, bi_ref, lam_ref,
         hf_ref, o_ref, xe_ref, a_ref, b_ref, h_ref, hc_ref) = refs
    else:
        (u_ref, prev_ref, next_ref, cw_ref, cb_ref, wa_ref, ba_ref, wi_ref, bi_ref, lam_ref,
         o_ref, xe_ref, a_ref, b_ref, hc_ref) = refs
        h_ref = o_ref
    j = pl.program_id(1)
    is_ctx, cc, cl = _seq_pos(j, rev, ncc, nlc)
    c = jnp.where(is_ctx, cc, cl)
    nseq = jnp.where(is_ctx, ncc, nlc)

    @pl.when(j == 0)
    def _():
        hc_ref[...] = jnp.zeros_like(hc_ref)

    ch = SEQ_CHUNK
    w = LRU_WIDTH
    xe_ref[SUBLANES:SUBLANES + ch, :] = u_ref[:, w:2 * w]
    xe_ref[0:SUBLANES, :] = jnp.where(c > 0, prev_ref[...], 0.0)
    xe_ref[SUBLANES + ch:2 * SUBLANES + ch, :] = jnp.where(c < nseq - 1, next_ref[...], 0.0)
    cw = cw_ref[...]
    xc = cb_ref[...]
    for tap in range(4):
        off = SUBLANES - 2 + tap
        xc = xc + cw[tap:tap + 1] * xe_ref[off:off + ch, :]
    xb = xc.astype(BF16)
    ra = jnp.concatenate(
        [_dot(xb[:, n * LRU_BLOCK:(n + 1) * LRU_BLOCK], wa_ref[n]) for n in range(LRU_BLOCKS)], axis=-1)
    ri = jnp.concatenate(
        [_dot(xb[:, n * LRU_BLOCK:(n + 1) * LRU_BLOCK], wi_ref[n]) for n in range(LRU_BLOCKS)], axis=-1)
    r = _sigmoid_tanh(ra + ba_ref[...])
    gate_i = _sigmoid_tanh(ri + bi_ref[...])
    neg_lam = -lam_ref[...]
    softplus = jnp.maximum(neg_lam, 0.0) + jnp.log1p(jnp.exp(-jnp.abs(neg_lam)))
    log_a = -LRU_C * r * softplus
    a = jnp.exp(log_a)
    a_ref[...] = a
    b_ref[...] = jnp.sqrt(-jnp.tanh(log_a) * (a * a + 1.0)) * (gate_i * xc)
    _scan_rows(a_ref, b_ref, h_ref, hc_ref, rev)
    if rev:
        y = u_ref[:, 0:w]
        gelu = 0.5 * y * (1.0 + jnp.tanh(0.7978845608028654 * (y + 0.044715 * (y * y * y))))
        o_ref[...] = (gelu * (hf_ref[...] + h_ref[...])).astype(BF16)


def _lru_pass(rev, u, cw, cb, wa, ba, wi, bi, lam, hf, nb, ncc, nlc):
    t = u.shape[0]
    ch = SEQ_CHUNK
    w = LRU_WIDTH
    r8 = ch // SUBLANES
    blk = functools.partial(_seq_blk, rev=rev, ncc=ncc, nlc=nlc, nb=nb)
    full2 = lambda b, j: (0, 0)
    in_specs = [
        pl.BlockSpec((ch, 2 * w), lambda b, j: (blk(b, j), 0)),
        pl.BlockSpec((SUBLANES, w), lambda b, j: (jnp.maximum(blk(b, j) * r8 - 1, 0), 1)),
        pl.BlockSpec((SUBLANES, w), lambda b, j: (jnp.minimum(blk(b, j) * r8 + r8, t // SUBLANES - 1), 1)),
        pl.BlockSpec((4, w), full2),
        pl.BlockSpec((1, w), full2),
        pl.BlockSpec((LRU_BLOCKS, LRU_BLOCK, LRU_BLOCK), lambda b, j: (0, 0, 0)),
        pl.BlockSpec((1, w), full2),
        pl.BlockSpec((LRU_BLOCKS, LRU_BLOCK, LRU_BLOCK), lambda b, j: (0, 0, 0)),
        pl.BlockSpec((1, w), full2),
        pl.BlockSpec((1, w), full2),
    ]
    args = [u, u, u, cw, cb.reshape(1, w), wa.astype(BF16), ba.reshape(1, w), wi.astype(BF16),
            bi.reshape(1, w), lam.reshape(1, w)]
    scratch = [pltpu.VMEM((ch + 2 * SUBLANES, w), F32), pltpu.VMEM((ch, w), F32), pltpu.VMEM((ch, w), F32)]
    if rev:
        in_specs.append(pl.BlockSpec((ch, w), lambda b, j: (blk(b, j), 0)))
        args.append(hf)
        scratch.append(pltpu.VMEM((ch, w), F32))
        out_dtype = BF16
    else:
        out_dtype = F32
    scratch.append(pltpu.VMEM((1, w), F32))
    return pl.pallas_call(
        functools.partial(_lru_kernel, rev, ncc, nlc),
        grid=(nb, ncc + nlc),
        in_specs=in_specs,
        out_specs=pl.BlockSpec((ch, w), lambda b, j: (blk(b, j), 0)),
        out_shape=jax.ShapeDtypeStruct((t, w), out_dtype),
        scratch_shapes=scratch,
        compiler_params=_cparams("parallel", "arbitrary"),
        name="lru_bwd" if rev else "lru_fwd",
    )(*args)


def _cumsum_rows(v, rev):
    n = v.shape[0]
    rows = lax.broadcasted_iota(jnp.int32, v.shape, 0)
    s = 1
    while s < n:
        if rev:
            v = v + jnp.where(rows < n - s, pltpu.roll(v, n - s, 0), 0.0)
        else:
            v = v + jnp.where(rows >= s, pltpu.roll(v, s, 0), 0.0)
        s *= 2
    return v


def _hgrn_diag_att(q, k, bc, lo, width, rev):
    n = q.shape[0]
    lane = lax.broadcasted_iota(jnp.int32, (n, width), 1)
    row = lax.broadcasted_iota(jnp.int32, (n, width), 0)
    att = jnp.zeros((n, width), F32)
    for s in range(n):
        w = q * (k[s:s + 1] * jnp.exp2(bc - bc[s:s + 1]))
        att = jnp.where(lane == lo + s, jnp.sum(w, axis=-1, keepdims=True), att)
    keep = (row + lo <= lane) if rev else (row + lo >= lane)
    return jnp.where(keep, att, 0.0)


def _hgrn_chunk(qraw, fl, v, lbh, st, rev):
    cs = HG_CHUNK
    sb = HG_SUB
    q = _silu(qraw)
    f = lbh + (1.0 - lbh) * _sigmoid(fl)
    k = 1.0 - f
    bc = _cumsum_rows(jnp.log(f) * LOG2E, rev)
    blast = bc[0:1] if rev else bc[cs - 1:cs]
    o = _dot_nt((q * jnp.exp2(bc)).astype(BF16), st.astype(BF16))
    kdec = k * jnp.exp2(blast - bc)
    st_new = st * jnp.exp2(blast) + _dot_tn(v.astype(BF16), kdec.astype(BF16))
    att_rows = []
    for i in range(cs // sb):
        lo, hi = i * sb, (i + 1) * sb
        att_i = _hgrn_diag_att(q[lo:hi], k[lo:hi], bc[lo:hi], lo, cs, rev)
        if (hi < cs) if rev else (lo > 0):
            mref = bc[hi:hi + 1] if rev else bc[lo - 1:lo]
            plo, phi = (hi, cs) if rev else (0, lo)
            qhat = q[lo:hi] * jnp.exp2(bc[lo:hi] - mref)
            kp = k[plo:phi] * jnp.exp2(mref - bc[plo:phi])
            pad = jnp.zeros((cs - (phi - plo), HG_DK), F32)
            kp = jnp.concatenate([pad, kp] if rev else [kp, pad], axis=0)
            att_i = att_i + _dot_nt(qhat.astype(BF16), kp.astype(BF16))
        att_rows.append(att_i)
    att = jnp.concatenate(att_rows, axis=0)
    return o + _dot(att.astype(BF16), v.astype(BF16)), st_new


def _hgrn_kernel(rev, layer, *refs):
    if rev:
        q_ref, f_ref, v_ref, lbl_ref, og_ref, of_ref, gn_ref, o_ref, st_ref = refs
    else:
        q_ref, f_ref, v_ref, lbl_ref, o_ref, st_ref = refs

    @pl.when(pl.program_id(1) == 0)
    def _():
        st_ref[...] = jnp.zeros_like(st_ref)

    lg = lbl_ref[...]
    e = jnp.exp(lg - jnp.max(lg, axis=0, keepdims=True))
    p = e / jnp.sum(e, axis=0, keepdims=True)
    lb = jnp.zeros_like(p[0:1])
    for jl in range(1, layer + 1):
        lb = lb + p[jl:jl + 1]

    cs = HG_CHUNK
    n_chunks = q_ref.shape[0] // cs

    def body(ci, carry):
        cidx = (n_chunks - 1 - ci) if rev else ci
        r0 = pl.multiple_of(cidx * cs, cs)
        for h in range(HG_HEADS):
            cols = slice(h * HG_DK, (h + 1) * HG_DK)
            o, st_new = _hgrn_chunk(q_ref[pl.ds(r0, cs), cols], f_ref[pl.ds(r0, cs), cols],
                                    v_ref[pl.ds(r0, cs), cols], lb[:, cols], st_ref[h], rev)
            st_ref[h] = st_new
            if rev:
                o = o + of_ref[pl.ds(r0, cs), cols]
                y = o * lax.rsqrt(jnp.mean(o * o, axis=-1, keepdims=True) + EPS) * gn_ref[...]
                o_ref[pl.ds(r0, cs), cols] = (y * _silu(og_ref[pl.ds(r0, cs), cols])).astype(BF16)
            else:
                o_ref[pl.ds(r0, cs), cols] = o
        return carry

    lax.fori_loop(0, n_chunks, body, 0, unroll=4)


def _hgrn_pass(rev, layer, u, lb_logits, gnorm, of, nb, ncc, nlc):
    t = u.shape[0]
    ch = SEQ_CHUNK
    w = HG_WIDTH
    nl = lb_logits.shape[0]
    blk = functools.partial(_seq_blk, rev=rev, ncc=ncc, nlc=nlc, nb=nb)
    base = A_IN // w
    col = lambda cb: pl.BlockSpec((ch, w), lambda b, j: (blk(b, j), cb))
    in_specs = [col(base), col(base + 1 + (1 if rev else 0)), col(base + 3),
                pl.BlockSpec((nl, w), lambda b, j: (0, 0))]
    args = [u, u, u, lb_logits]
    if rev:
        in_specs += [col(base + 4), pl.BlockSpec((ch, w), lambda b, j: (blk(b, j), 0)),
                     pl.BlockSpec((1, HG_DK), lambda b, j: (0, 0))]
        args += [u, of, gnorm.reshape(1, HG_DK)]
    return pl.pallas_call(
        functools.partial(_hgrn_kernel, rev, layer),
        grid=(nb, ncc + nlc),
        in_specs=in_specs,
        out_specs=pl.BlockSpec((ch, w), lambda b, j: (blk(b, j), 0)),
        out_shape=jax.ShapeDtypeStruct((t, w), BF16 if rev else F32),
        scratch_shapes=[pltpu.VMEM((HG_HEADS, HG_DK, HG_DK), F32)],
        compiler_params=_cparams("parallel", "arbitrary"),
        name="hgrn_bwd" if rev else "hgrn_fwd",
    )(*args)


def _rope_tables(n):
    rows = n // GRID_W
    row = jnp.repeat(jnp.arange(rows, dtype=F32), GRID_W)
    colp = jnp.tile(jnp.arange(GRID_W, dtype=F32), rows)
    inv = ROPE_THETA ** (-jnp.arange(ROPE_FREQS, dtype=F32) / ROPE_FREQS)
    ar = row[:, None] * inv
    ac = colp[:, None] * inv
    cos = jnp.concatenate([jnp.cos(ar), jnp.cos(ar), jnp.cos(ac), jnp.cos(ac)], axis=-1)
    sin = jnp.concatenate([-jnp.sin(ar), jnp.sin(ar), -jnp.sin(ac), jnp.sin(ac)], axis=-1)
    cos = jnp.concatenate([cos, jnp.ones((SEQ_CHUNK, HEAD_DIM), F32)], axis=0)
    sin = jnp.concatenate([sin, jnp.zeros((SEQ_CHUNK, HEAD_DIM), F32)], axis=0)
    return cos, sin


def _qkv_kernel(qa_ref, qb_ref, kv_ref, cos_ref, sin_ref, qn_ref, kn_ref, q_out, k_out, vt_out):
    cos = cos_ref[...]
    sin = sin_ref[...]
    lane = lax.broadcasted_iota(jnp.int32, cos.shape, 1)
    first = (lane % (2 * ROPE_FREQS)) < ROPE_FREQS

    def norm_rope(v, g, scale):
        y = v * lax.rsqrt(jnp.mean(v * v, axis=-1, keepdims=True) + EPS) * g
        partner = jnp.where(first, pltpu.roll(y, HEAD_DIM - ROPE_FREQS, 1), pltpu.roll(y, ROPE_FREQS, 1))
        out = y * cos + partner * sin
        if scale is not None:
            out = out * scale
        return out.astype(BF16)

    half = N_Q_HEADS // 2
    for h in range(half):
        cols = slice(h * HEAD_DIM, (h + 1) * HEAD_DIM)
        q_out[:, cols] = norm_rope(qa_ref[:, cols], qn_ref[...], Q_SCALE)
        cols_b = slice((half + h) * HEAD_DIM, (half + h + 1) * HEAD_DIM)
        q_out[:, cols_b] = norm_rope(qb_ref[:, cols], qn_ref[...], Q_SCALE)
    for h in range(N_KV_HEADS):
        cols = slice(h * HEAD_DIM, (h + 1) * HEAD_DIM)
        k_out[:, cols] = norm_rope(kv_ref[:, cols], kn_ref[...], None)
        vt_out[cols, :] = kv_ref[:, KV_WIDTH + h * HEAD_DIM:KV_WIDTH + (h + 1) * HEAD_DIM].T.astype(BF16)


def _qkv(u, cos, sin, qn, kn, n_lat, nlc):
    t = u.shape[0]
    ch = SEQ_CHUNK
    cw = 512
    base = (A_IN + B_IN) // cw
    tab = lambda i: (jnp.where(i < n_lat // ch, i % nlc, nlc), 0)
    return pl.pallas_call(
        _qkv_kernel,
        grid=(t // ch,),
        in_specs=[
            pl.BlockSpec((ch, cw), lambda i: (i, base)),
            pl.BlockSpec((ch, cw), lambda i: (i, base + 1)),
            pl.BlockSpec((ch, cw), lambda i: (i, base + 2)),
            pl.BlockSpec((ch, HEAD_DIM), tab),
            pl.BlockSpec((ch, HEAD_DIM), tab),
            pl.BlockSpec((1, HEAD_DIM), lambda i: (0, 0)),
            pl.BlockSpec((1, HEAD_DIM), lambda i: (0, 0)),
        ],
        out_specs=[
            pl.BlockSpec((ch, ATT_WIDTH), lambda i: (i, 0)),
            pl.BlockSpec((ch, KV_WIDTH), lambda i: (i, 0)),
            pl.BlockSpec((KV_WIDTH, ch), lambda i: (0, i)),
        ],
        out_shape=[
            jax.ShapeDtypeStruct((t, ATT_WIDTH), BF16),
            jax.ShapeDtypeStruct((t, KV_WIDTH), BF16),
            jax.ShapeDtypeStruct((KV_WIDTH, t), BF16),
        ],
        compiler_params=_cparams("parallel"),
        name="qkv_prep",
    )(u, u, u, cos, sin, qn.reshape(1, HEAD_DIM), kn.reshape(1, HEAD_DIM))


def _row_groups(v, op):
    return op(v.reshape(v.shape[0] // SUBLANES, SUBLANES, v.shape[1]), axis=0)


def _attn_body(with_lat, q_ref, kc_ref, vtc_ref, kl_ref, vtl_ref, o_ref, acc_ref, s_ref):
    tq = q_ref.shape[0]
    tk = s_ref.shape[1]
    n_tiles = kl_ref.shape[0] // tk if with_lat else 0

    for g in range(N_KV_HEADS):
        gcols = slice(g * HEAD_DIM, (g + 1) * HEAD_DIM)
        qs = jnp.concatenate(
            [q_ref[:, (g * Q_PER_KV + h) * HEAD_DIM:(g * Q_PER_KV + h + 1) * HEAD_DIM] for h in range(Q_PER_KV)],
            axis=0)

        def softmax_pv(s, vtt, carry):
            m_new = jnp.max(_row_groups(s, jnp.max), axis=0, keepdims=True)
            if carry is not None:
                m_new = jnp.maximum(carry[0], m_new)
            p = jnp.exp2(s - m_new)
            pv = _dot(vtt, p.astype(BF16))
            if carry is None:
                acc_ref[...] = pv
                return m_new, _row_groups(p, jnp.sum)
            alpha = jnp.exp2(carry[0] - m_new)
            acc_ref[...] = alpha * acc_ref[...] + pv
            return m_new, alpha * carry[1] + _row_groups(p, jnp.sum)

        def scores(t):
            r0 = t * tk if isinstance(t, int) else pl.multiple_of(t * tk, tk)
            return _dot_nt(kl_ref[pl.ds(r0, tk), gcols], qs)

        def consume(slot, t, carry):
            r0 = t * tk if isinstance(t, int) else pl.multiple_of(t * tk, tk)
            return softmax_pv(s_ref[slot], vtl_ref[gcols, pl.ds(r0, tk)], carry)

        if n_tiles:
            s_ref[0] = scores(0)
        carry = softmax_pv(_dot_nt(kc_ref[:, gcols], qs), vtc_ref[gcols, :], None)
        n_pairs = max(n_tiles // 2 - 1, 0)
        if n_pairs:
            def pair(j, carry):
                t0 = 2 * j
                s_ref[1] = scores(t0 + 1)
                carry = consume(0, t0, carry)
                s_ref[0] = scores(t0 + 2)
                return consume(1, t0 + 1, carry)
            carry = lax.fori_loop(0, n_pairs, pair, carry)
        for t in range(2 * n_pairs, n_tiles):
            if t + 1 < n_tiles:
                s_ref[(t + 1) % 2] = scores(t + 1)
            carry = consume(t % 2, t, carry)
        ot = acc_ref[...] * (1.0 / jnp.sum(carry[1], axis=0, keepdims=True))
        for h in range(Q_PER_KV):
            cols = slice((g * Q_PER_KV + h) * HEAD_DIM, (g * Q_PER_KV + h + 1) * HEAD_DIM)
            o_ref[:, cols] = ot[:, h * tq:(h + 1) * tq].T.astype(BF16)


def _attn_kernel(lat_steps, ctx_steps, *refs):
    if ctx_steps == 0:
        _attn_body(True, *refs)
        return
    i = pl.program_id(1)

    @pl.when(i < lat_steps)
    def _():
        _attn_body(True, *refs)

    @pl.when(i >= lat_steps)
    def _():
        _attn_body(False, *refs)


def _attn(q, k, vt, nb, n, nc, with_ctx_queries):
    n_lat = nb * n
    tq = ATT_QUERY_TILE
    lat_steps = n // tq
    ctx_steps = nc // tq if with_ctx_queries else 0
    rows = n_lat + (nb * nc if with_ctx_queries else 0)

    def q_map(b, i):
        return (jnp.where(i < lat_steps, b * lat_steps + i, n_lat // tq + b * ctx_steps + (i - lat_steps)), 0)

    m = Q_PER_KV * tq
    tk = min(ATT_KEY_TILE, n)
    assert n % tk == 0 and n % tq == 0 and nc % tq == 0
    return pl.pallas_call(
        functools.partial(_attn_kernel, lat_steps, ctx_steps),
        grid=(nb, lat_steps + ctx_steps),
        in_specs=[pl.BlockSpec((tq, ATT_WIDTH), q_map),
                  pl.BlockSpec((nc, KV_WIDTH), lambda b, i: (n_lat // nc + b, 0)),
                  pl.BlockSpec((KV_WIDTH, nc), lambda b, i: (0, n_lat // nc + b)),
                  pl.BlockSpec((n, KV_WIDTH), lambda b, i: (b, 0)),
                  pl.BlockSpec((KV_WIDTH, n), lambda b, i: (0, b))],
        out_specs=pl.BlockSpec((tq, ATT_WIDTH), q_map),
        out_shape=jax.ShapeDtypeStruct((rows, ATT_WIDTH), BF16),
        scratch_shapes=[pltpu.VMEM((HEAD_DIM, m), F32), pltpu.VMEM((2, tk, m), F32)],
        compiler_params=_cparams("parallel", "arbitrary"),
        name="attention",
    )(q, k, vt, k, vt)


def _outproj_kernel(routed, lat_tiles, *refs):
    if routed:
        ra_ref, hg_ref, at_ref, xl_ref, xc_ref, mod_ref, w_ref, g_ref, rt_ref, xo_ref, h_ref, lg_ref = refs
    else:
        ra_ref, hg_ref, at_ref, xl_ref, xc_ref, mod_ref, w_ref, g_ref, xo_ref, h_ref = refs
    w1 = ra_ref.shape[1]
    w2 = w1 + hg_ref.shape[1]
    mix = (_dot(ra_ref[...], w_ref[0:w1, :]) + _dot(hg_ref[...], w_ref[w1:w2, :])
           + _dot(at_ref[...], w_ref[w2:, :]))
    m = mod_ref[0]
    x_in = jnp.where(pl.program_id(0) < lat_tiles, xl_ref[...], xc_ref[...])
    x = x_in + m[2:3] * mix
    xo_ref[...] = x
    y = x * lax.rsqrt(jnp.mean(x * x, axis=-1, keepdims=True) + EPS) * g_ref[...]
    h = y * (1.0 + m[4:5]) + m[3:4]
    if routed:
        h_ref[...] = h
        h_hi = h.astype(BF16)
        h_lo = (h - h_hi.astype(F32)).astype(BF16)
        lg = _dot(h_hi, rt_ref[...]) + _dot(h_lo, rt_ref[...])
        lg_ref[...] = lg[:, 0:LANES] + lg[:, LANES:2 * LANES]
    else:
        h_ref[...] = h.astype(BF16)


def _outproj(ra, hg, at, x_lat, x_ctx, ctx_base, mod, w, g, router, n_tok, n_lat, seq, nb):
    d = x_lat.shape[1]
    tm = 256
    routed = router is not None
    row = functools.partial(_mod_row, tile=tm, n_lat=n_lat, seq=seq, nb=nb)
    tok = lambda width: pl.BlockSpec((tm, width), lambda i: (i, 0))
    lat_spec, ctx_spec = _stream_specs(tm, d, n_lat // tm, ctx_base // tm)
    in_specs = [tok(ra.shape[1]), tok(hg.shape[1]), tok(at.shape[1]), lat_spec, ctx_spec,
                pl.BlockSpec((1, N_MOD, d), lambda i: (row(i), 0, 0)),
                pl.BlockSpec(w.shape, lambda i: (0, 0)),
                pl.BlockSpec((1, d), lambda i: (0, 0))]
    args = [ra, hg, at, x_lat, x_ctx, mod, w, g.reshape(1, d)]
    out_specs = [tok(d), tok(d)]
    out_shape = [jax.ShapeDtypeStruct((n_tok, d), F32), jax.ShapeDtypeStruct((n_tok, d), F32 if routed else BF16)]
    if routed:
        in_specs.append(pl.BlockSpec((d, 2 * LANES), lambda i: (0, 0)))
        args.append(router)
        out_specs.append(tok(LANES))
        out_shape.append(jax.ShapeDtypeStruct((n_tok, LANES), F32))
    return pl.pallas_call(
        functools.partial(_outproj_kernel, routed, n_lat // tm),
        grid=(n_tok // tm,),
        in_specs=in_specs,
        out_specs=out_specs,
        out_shape=out_shape,
        compiler_params=_cparams("parallel"),
        name="outproj",
    )(*args)


def _ffn_kernel(h_ref, x_ref, mod_ref, w1_ref, w3_ref, w2_ref, o_ref, acc_ref):
    k = pl.program_id(1)

    @pl.when(k == 0)
    def _():
        acc_ref[...] = jnp.zeros_like(acc_ref)

    h = h_ref[...]
    a = _dot(h, w1_ref[...])
    z = _silu(a) * _dot(h, w3_ref[...])
    acc_ref[...] += _dot(z.astype(BF16), w2_ref[...])

    @pl.when(k == pl.num_programs(1) - 1)
    def _():
        o_ref[...] = x_ref[...] + mod_ref[0][5:6] * acc_ref[...]


def _ffn(h, xs, mod, w1, w3, w2, n_lat, seq, nb):
    t, d = h.shape
    f = w1.shape[1]
    tm = TOKEN_TILE
    tf = 512
    row = functools.partial(_mod_row, tile=tm, n_lat=n_lat, seq=seq, nb=nb)
    return pl.pallas_call(
        _ffn_kernel,
        grid=(t // tm, f // tf),
        in_specs=[
            pl.BlockSpec((tm, d), lambda i, k: (i, 0)),
            pl.BlockSpec((tm, d), lambda i, k: (i, 0)),
            pl.BlockSpec((1, N_MOD, d), lambda i, k: (row(i), 0, 0)),
            pl.BlockSpec((d, tf), lambda i, k: (0, k)),
            pl.BlockSpec((d, tf), lambda i, k: (0, k)),
            pl.BlockSpec((tf, d), lambda i, k: (k, 0)),
        ],
        out_specs=pl.BlockSpec((tm, d), lambda i, k: (i, 0)),
        out_shape=jax.ShapeDtypeStruct((t, d), F32),
        scratch_shapes=[pltpu.VMEM((tm, d), F32)],
        compiler_params=_cparams("parallel", "arbitrary"),
        name="ffn_dense",
    )(h, xs, mod, w1, w3, w2)


def _router_kernel(lg_ref, meta_ref, wts_ref, cnt_ref, run_ref):
    @pl.when(pl.program_id(0) == 0)
    def _():
        run_ref[...] = jnp.zeros_like(run_ref)

    lg = lg_ref[...]
    tm = lg.shape[0]
    lane = lax.broadcasted_iota(jnp.int32, lg.shape, 1)
    lane_f = lane.astype(F32)
    v = jnp.where(lane < N_EXPERTS, lg, -jnp.inf)
    m1 = jnp.max(v, axis=-1, keepdims=True)
    i1 = jnp.min(jnp.where(v == m1, lane_f, float(LANES)), axis=-1, keepdims=True)
    v2 = jnp.where(lane_f == i1, -jnp.inf, v)
    m2 = jnp.max(v2, axis=-1, keepdims=True)
    i2 = jnp.min(jnp.where(v2 == m2, lane_f, float(LANES)), axis=-1, keepdims=True)
    e = jnp.exp(m2 - m1)
    wt1 = 1.0 / (1.0 + e)
    wt2 = e / (1.0 + e)
    hit1 = lane_f == i1
    hit2 = lane_f == i2
    assign = jnp.where(hit1 | hit2, 1.0, 0.0)
    r = lax.broadcasted_iota(jnp.int32, (tm, tm), 0)
    c = lax.broadcasted_iota(jnp.int32, (tm, tm), 1)
    tri = jnp.where(r > c, 1.0, 0.0).astype(BF16)
    rank = _dot(tri, assign.astype(BF16)) + run_ref[0:1, :]
    r1 = jnp.sum(jnp.where(hit1, rank, 0.0), axis=-1, keepdims=True)
    r2 = jnp.sum(jnp.where(hit2, rank, 0.0), axis=-1, keepdims=True)
    run_ref[...] = run_ref[...] + jnp.sum(assign, axis=0, keepdims=True)
    meta = jnp.where(lane == 0, i1, jnp.where(lane == 1, i2, jnp.where(lane == 2, r1, jnp.where(lane == 3, r2, 0.0))))
    meta_ref[...] = meta.astype(jnp.int32)
    wts_ref[...] = jnp.where(lane == 0, wt1, jnp.where(lane == 1, wt2, 0.0))
    cnt_ref[...] = run_ref[...]


def _router(logits):
    t = logits.shape[0]
    tm = TOKEN_TILE
    tok = pl.BlockSpec((tm, LANES), lambda i: (i, 0))
    return pl.pallas_call(
        _router_kernel,
        grid=(t // tm,),
        in_specs=[tok],
        out_specs=[tok, tok, pl.BlockSpec((SUBLANES, LANES), lambda i: (0, 0))],
        out_shape=[jax.ShapeDtypeStruct((t, LANES), jnp.int32), jax.ShapeDtypeStruct((t, LANES), F32),
                   jax.ShapeDtypeStruct((SUBLANES, LANES), F32)],
        scratch_shapes=[pltpu.VMEM((SUBLANES, LANES), F32)],
        compiler_params=_cparams("arbitrary"),
        name="router",
    )(logits)


def _row_copy(src_ref, src_row, dst_ref, dst_row, sem):
    return pltpu.make_async_copy(src_ref.at[pl.ds(src_row, 1)], dst_ref.at[pl.ds(dst_row, 1)], sem)


def _dispatch_kernel(pos_ref, h_ref, xs_in_ref, xs_ref, pos_smem, hbuf, sem_p, sem_h, sem_d):
    del xs_in_ref
    i = pl.program_id(0)
    n = pl.num_programs(0)
    tm = hbuf.shape[1]
    cur = i % 3

    def tile_copy(tile, buf):
        return pltpu.make_async_copy(h_ref.at[pl.ds(pl.multiple_of(tile * tm, tm), tm)], hbuf.at[buf], sem_h.at[buf])

    @pl.when(i == 0)
    def _():
        tile_copy(0, 0).start()

    @pl.when(i < n - 1)
    def _():
        tile_copy(i + 1, (i + 1) % 3).start()

    cp = pltpu.make_async_copy(pos_ref.at[0, 0], pos_smem, sem_p)
    cp.start()
    cp.wait()
    tile_copy(i, cur).wait()

    def issue(t, carry):
        _row_copy(hbuf.at[cur], t, xs_ref, pos_smem[t], sem_d.at[cur]).start(priority=0)
        _row_copy(hbuf.at[cur], t, xs_ref, pos_smem[tm + t], sem_d.at[cur]).start(priority=1)
        return carry

    lax.fori_loop(0, tm, issue, 0, unroll=8)

    def drain(buf):
        def body(t, carry):
            _row_copy(hbuf.at[buf], 0, xs_ref, 0, sem_d.at[buf]).wait()
            _row_copy(hbuf.at[buf], 0, xs_ref, 0, sem_d.at[buf]).wait()
            return carry
        lax.fori_loop(0, tm, body, 0, unroll=8)

    @pl.when(i > 0)
    def _():
        drain((i + 2) % 3)

    @pl.when(i == n - 1)
    def _():
        drain(cur)


def _dispatch(pos, h, xs0):
    t, d = h.shape
    tm = 256
    nt = t // tm
    return pl.pallas_call(
        _dispatch_kernel,
        grid=(nt,),
        in_specs=[
            pl.BlockSpec((1, 1, 2 * tm), lambda i: (i, 0, 0)),
            pl.BlockSpec(memory_space=pl.ANY),
            pl.BlockSpec(memory_space=pl.ANY),
        ],
        out_specs=pl.BlockSpec(memory_space=pl.ANY),
        out_shape=jax.ShapeDtypeStruct(xs0.shape, xs0.dtype),
        scratch_shapes=[pltpu.SMEM((2 * tm,), jnp.int32), pltpu.VMEM((3, tm, d), F32),
                        pltpu.SemaphoreType.DMA(()), pltpu.SemaphoreType.DMA((3,)), pltpu.SemaphoreType.DMA((3,))],
        input_output_aliases={2: 0},
        compiler_params=_cparams("arbitrary"),
        name="moe_dispatch",
    )(pos, h, xs0)


def _expert_kernel(te_ref, tv_ref, x_ref, w1_ref, w3_ref, w2_ref, y_ref, xb_ref):
    del te_ref
    i = pl.program_id(0)
    k = pl.program_id(1)

    @pl.when(k == 0)
    def _():
        y_ref[...] = jnp.zeros_like(y_ref)

    @pl.when(tv_ref[i] > 0)
    def _():
        @pl.when(k == 0)
        def _():
            xb_ref[...] = x_ref[...].astype(BF16)

        xb = xb_ref[...]
        a = _dot(xb, w1_ref[...])
        z = _silu(a) * _dot(xb, w3_ref[...])
        y_ref[...] += _dot(z.astype(BF16), w2_ref[...])


def _experts(tile_expert, tile_valid, tile_row, xs, w1, w3, w2):
    rows, d = xs.shape
    f = w1.shape[2]
    tm = EXPERT_TILE
    tf = 512
    grid_spec = pltpu.PrefetchScalarGridSpec(
        num_scalar_prefetch=3,
        grid=(rows // tm, f // tf),
        in_specs=[
            pl.BlockSpec((tm, d), lambda i, k, te, tv, tr: (tr[i], 0)),
            pl.BlockSpec((None, d, tf), lambda i, k, te, tv, tr: (te[i], 0, jnp.where(tv[i] > 0, k, f // tf - 1))),
            pl.BlockSpec((None, d, tf), lambda i, k, te, tv, tr: (te[i], 0, jnp.where(tv[i] > 0, k, f // tf - 1))),
            pl.BlockSpec((None, tf, d), lambda i, k, te, tv, tr: (te[i], jnp.where(tv[i] > 0, k, f // tf - 1), 0)),
        ],
        out_specs=pl.BlockSpec((tm, d), lambda i, k, te, tv, tr: (i, 0)),
        scratch_shapes=[pltpu.VMEM((tm, d), BF16)],
    )

    def body(te_ref, tv_ref, tr_ref, *rest):
        del tr_ref
        _expert_kernel(te_ref, tv_ref, *rest)

    return pl.pallas_call(
        body,
        grid_spec=grid_spec,
        out_shape=jax.ShapeDtypeStruct((rows, d), F32),
        compiler_params=_cparams("arbitrary", "arbitrary"),
        name="moe_experts",
    )(tile_expert, tile_valid, tile_row, xs, w1, w3, w2)


def _combine_kernel(pos0_ref, posn_ref, wts_ref, x_ref, mod_ref, ys_ref, o_ref, pos_smem, y_ref, sem_p, sem_g):
    i = pl.program_id(0)
    n = pl.num_programs(0)
    tm = x_ref.shape[0]
    cur = i % 2

    def gather(pos_ref, slot):
        cp = pltpu.make_async_copy(pos_ref.at[0, 0], pos_smem, sem_p)
        cp.start()
        cp.wait()

        def issue(t, carry):
            _row_copy(ys_ref, pos_smem[t], y_ref.at[slot, 0], t, sem_g.at[slot]).start(priority=0)
            _row_copy(ys_ref, pos_smem[tm + t], y_ref.at[slot, 1], t, sem_g.at[slot]).start(priority=1)
            return carry

        lax.fori_loop(0, tm, issue, 0, unroll=8)

    @pl.when(i == 0)
    def _():
        gather(pos0_ref, 0)

    @pl.when(i < n - 1)
    def _():
        gather(posn_ref, 1 - cur)

    def drain(t, carry):
        _row_copy(ys_ref, 0, y_ref.at[cur, 0], 0, sem_g.at[cur]).wait()
        _row_copy(ys_ref, 0, y_ref.at[cur, 1], 0, sem_g.at[cur]).wait()
        return carry

    lax.fori_loop(0, tm, drain, 0, unroll=8)
    wts = wts_ref[...]
    f = wts[:, 0:1] * y_ref[cur, 0] + wts[:, 1:2] * y_ref[cur, 1]
    o_ref[...] = x_ref[...] + mod_ref[0][5:6] * f


def _combine(pos, wts, xs, mod, ys, n_lat, seq, nb):
    t, d = xs.shape
    tm = 256
    row = functools.partial(_mod_row, tile=tm, n_lat=n_lat, seq=seq, nb=nb)
    return pl.pallas_call(
        _combine_kernel,
        grid=(t // tm,),
        in_specs=[
            pl.BlockSpec((1, 1, 2 * tm), lambda i: (0, 0, 0)),
            pl.BlockSpec((1, 1, 2 * tm), lambda i: (jnp.minimum(i + 1, t // tm - 1), 0, 0)),
            pl.BlockSpec((tm, LANES), lambda i: (i, 0)),
            pl.BlockSpec((tm, d), lambda i: (i, 0)),
            pl.BlockSpec((1, N_MOD, d), lambda i: (row(i), 0, 0)),
            pl.BlockSpec(memory_space=pl.ANY),
        ],
        out_specs=pl.BlockSpec((tm, d), lambda i: (i, 0)),
        out_shape=jax.ShapeDtypeStruct((t, d), F32),
        scratch_shapes=[pltpu.SMEM((2 * tm,), jnp.int32), pltpu.VMEM((2, 2, tm, d), F32),
                        pltpu.SemaphoreType.DMA(()), pltpu.SemaphoreType.DMA((2,))],
        compiler_params=_cparams("arbitrary"),
        name="moe_combine",
    )(pos, pos, wts, xs, mod, ys)


def _moe(h, logits, xs, mod, w1, w3, w2, n_lat, seq, nb):
    t, d = h.shape
    meta, wts, counts = _router(logits)
    te_rows = EXPERT_TILE
    cnt = counts[0, :N_EXPERTS].astype(jnp.int32)
    padded = ((cnt + te_rows - 1) // te_rows) * te_rows
    ends = jnp.cumsum(padded)
    starts = ends - padded
    pos1 = jnp.take(starts, meta[:, 0]) + meta[:, 2]
    pos2 = jnp.take(starts, meta[:, 1]) + meta[:, 3]
    tmd = 256
    pos = jnp.concatenate([pos1.reshape(t // tmd, 1, tmd), pos2.reshape(t // tmd, 1, tmd)], axis=-1)
    n_tiles = (2 * t) // te_rows + N_EXPERTS
    tile_start = jnp.arange(n_tiles, dtype=jnp.int32) * te_rows
    n_valid = ends[-1] // te_rows
    tile_valid = (tile_start < ends[-1]).astype(jnp.int32)
    tile_row = jnp.minimum(jnp.arange(n_tiles, dtype=jnp.int32), n_valid - 1)
    tile_expert = jnp.sum((tile_row[:, None] * te_rows >= ends[None, :]).astype(jnp.int32), axis=1)
    xs0 = jnp.zeros((n_tiles * te_rows, d), F32)
    x_sorted = _dispatch(pos, h, xs0)
    y_sorted = _experts(tile_expert, tile_valid, tile_row, x_sorted, w1, w3, w2)
    return _combine(pos, wts, xs, mod, y_sorted, n_lat, seq, nb)


def kernel(x, c, ctx, c_ctx, w_mod, b_mod, norm1, norm2, w_in, w_out, conv_w, conv_b, lru_wa, lru_ba, lru_wi, lru_bi, lru_lambda, hgrn_lb_logits, hgrn_gnorm, q_norm, k_norm, ffn_w1, ffn_w3, ffn_w2, router, moe_w1, moe_w3, moe_w2):
    nb, n, d = x.shape
    nc = ctx.shape[1]
    depth = w_mod.shape[0]
    assert n % SEQ_CHUNK == 0 and nc % SEQ_CHUNK == 0 and n % TOKEN_TILE == 0
    assert (nb * nc) % TOKEN_TILE == 0 and n % GRID_W == 0
    n_lat = nb * n
    nlc = n // SEQ_CHUNK
    ncc = nc // SEQ_CHUNK

    n_all = n_lat + nb * nc
    x_lat, x_ctx, ctx_base = x.reshape(n_lat, d), ctx.reshape(nb * nc, d), 0
    mod_rows = 2 * SUBLANES * ((nb + 1 + 2 * SUBLANES - 1) // (2 * SUBLANES))
    cc = jnp.zeros((mod_rows, d), F32).at[:nb].set(c).at[nb].set(c_ctx)
    mod_all = _mod_table(cc, w_mod, b_mod).reshape(depth, mod_rows, N_MOD, d)
    cos, sin = _rope_tables(n)
    router_f = jnp.pad(router, ((0, 0), (0, 0), (0, LANES - router.shape[-1])))
    router_hi = router_f.astype(BF16)
    router_lo = (router_f - router_hi.astype(F32)).astype(BF16)
    router_p = jnp.concatenate([router_hi, router_lo], axis=-1)

    for l in range(depth):
        need_ctx = l < depth - 1
        mod = mod_all[l]
        u = _inproj(x_lat, x_ctx, ctx_base, n_all, mod, norm1[l], _to_bf16(w_in, l), n_lat, n, nb)

        lru_args = lambda dd: (conv_w[l], conv_b[l], lru_wa[l, dd], lru_ba[l, dd], lru_wi[l, dd], lru_bi[l, dd],
                               lru_lambda[l, dd])
        hf = _lru_pass(False, u, *lru_args(0), None, nb, ncc, nlc)
        ra = _lru_pass(True, u, *lru_args(1), hf, nb, ncc, nlc)

        of = _hgrn_pass(False, l, u, hgrn_lb_logits[0], hgrn_gnorm[l], None, nb, ncc, nlc)
        hg = _hgrn_pass(True, l, u, hgrn_lb_logits[1], hgrn_gnorm[l], of, nb, ncc, nlc)

        q, k, vt = _qkv(u, cos, sin, q_norm[l], k_norm[l], n_lat, nlc)
        at = _attn(q, k, vt, nb, n, nc, need_ctx)

        n_tok = n_all if need_ctx else n_lat
        j = l // 2
        routed = l % 2 == 1
        outs = _outproj(ra, hg, at, x_lat, x_ctx, ctx_base, mod, _to_bf16(w_out, l), norm2[l],
                        router_p[j] if routed else None, n_tok, n_lat, n, nb)
        if routed:
            xn, h2, logits = outs
            xs = _moe(h2, logits, xn, mod, _to_bf16(moe_w1, j), _to_bf16(moe_w3, j), _to_bf16(moe_w2, j),
                      n_lat, n, nb)
        else:
            xn, h2 = outs
            xs = _ffn(h2, xn, mod, _to_bf16(ffn_w1, j), _to_bf16(ffn_w3, j), _to_bf16(ffn_w2, j), n_lat, n, nb)
        x_lat, x_ctx, ctx_base = xs, xs, n_lat
    return xs[:n_lat].reshape(nb, n, d)
```
